```python
import math
import jax, jax.numpy as jnp
from jax import lax
import numpy as np

D_MODEL = 1024
BATCH = 32
SEQ = 2048
DEPTH = 4

CTX_LEN = 256
GRID_W = 64
EPS = 1e-6
ROPE_BASE = 10000.0
Q_BLOCK = 128

MLA_HEADS = 8
MLA_NOPE = 64
MLA_ROPE = 32
MLA_V = 64
MLA_Q_RANK = 256
MLA_KV_RANK = 128
MLA_OUT = MLA_HEADS * MLA_V
MLA_SCALE = (MLA_NOPE + MLA_ROPE) ** -0.5

HY_WIDTH = 256
HY_SHORT = 3
HY_EMB = 33
HY_BANDS = (HY_EMB - 1) // 2
HY_HIDDEN = 64
HY_TARGET = 1e-2
HY_FAST = 0.3
HY_SLOW = 1.5
HY_SHIFT = 0.05

DF_HEADS = 4
DF_DIM = 32
DF_V = 2 * DF_DIM
DF_OUT = DF_HEADS * DF_V
DF_SCALE = DF_DIM ** -0.5

N_BRANCH = 3
D_FF = 4 * D_MODEL

COLS = (MLA_Q_RANK, MLA_KV_RANK, MLA_ROPE, 3 * HY_WIDTH,
        2 * DF_HEADS * DF_DIM, 2 * DF_HEADS * DF_DIM, DF_OUT, N_BRANCH * D_MODEL)
D_IN = sum(COLS)

kernel_name = "hybrid_mla_hyena_diffattn_dit"

f32 = jnp.float32


def rmsnorm(x, g):
    xf = x.astype(f32)
    y = xf * lax.rsqrt(jnp.mean(xf * xf, axis=-1, keepdims=True) + EPS)
    return (y * g.astype(f32)).astype(x.dtype)


def modulate(h, shift, scale):
    return h * (1 + scale) + shift


def split_cols(p):
    parts, off = [], 0
    for n in COLS:
        parts.append(p[..., off:off + n])
        off += n
    return parts


def to_heads(t, n_heads):
    B, L, _ = t.shape
    return t.reshape(B, L, n_heads, -1).transpose(0, 2, 1, 3)


def from_heads(t):
    B, H, L, d = t.shape
    return t.transpose(0, 2, 1, 3).reshape(B, L, H * d)


def axial_cos_sin(n_tokens, d_rot):
    rows = n_tokens // GRID_W
    row = jnp.repeat(jnp.arange(rows), GRID_W).astype(f32)
    col = jnp.tile(jnp.arange(GRID_W), rows).astype(f32)
    nf = d_rot // 4
    inv = ROPE_BASE ** (-jnp.arange(nf, dtype=f32) / nf)
    ang = jnp.concatenate([row[:, None] * inv, col[:, None] * inv], axis=-1)
    return jnp.cos(ang), jnp.sin(ang)


def apply_rope(x, cos, sin):
    half = x.shape[-1] // 2
    x1 = x[..., :half].astype(f32)
    x2 = x[..., half:].astype(f32)
    return jnp.concatenate([x1 * cos - x2 * sin, x2 * cos + x1 * sin], axis=-1).astype(x.dtype)


def attend(q, k, v, scale):
    B, H, Lq, dk = q.shape
    nb = Lq // Q_BLOCK
    qb = jnp.moveaxis(q.reshape(B, H, nb, Q_BLOCK, dk), 2, 0)

    def block(qi):
        s = jnp.einsum('bhqd,bhkd->bhqk', qi, k).astype(f32) * scale
        p = jax.nn.softmax(s, axis=-1).astype(v.dtype)
        return jnp.einsum('bhqk,bhkv->bhqv', p, v)

    o = lax.map(block, qb)
    return jnp.moveaxis(o, 0, 2).reshape(B, H, Lq, v.shape[-1])


def mla_qkv(pq, pkv, pkr, lp, rope):
    q = to_heads(rmsnorm(pq, lp['mla_q_norm_g']) @ lp['mla_w_uq'], MLA_HEADS)
    kv = to_heads(rmsnorm(pkv, lp['mla_kv_norm_g']) @ lp['mla_w_ukv'], MLA_HEADS)
    q_nope, q_rope = q[..., :MLA_NOPE], q[..., MLA_NOPE:]
    k_nope, v = kv[..., :MLA_NOPE], kv[..., MLA_NOPE:]
    k_rope = pkr[:, None]
    if rope is not None:
        q_rope = apply_rope(q_rope, *rope)
        k_rope = apply_rope(k_rope, *rope)
    k_rope = jnp.broadcast_to(k_rope, k_nope.shape[:-1] + (MLA_ROPE,))
    q = jnp.concatenate([q_nope, q_rope], axis=-1)
    k = jnp.concatenate([k_nope, k_rope], axis=-1)
    return q, k, v


def short_conv(u, w, b):
    C = u.shape[-1]
    y = lax.conv_general_dilated(u, w[:, None, :].astype(u.dtype), window_strides=(1,),
                                 padding=[(HY_SHORT // 2, HY_SHORT // 2)],
                                 dimension_numbers=('NWC', 'WIO', 'NWC'),
                                 feature_group_count=C)
    return y + b


def hyena_filters(L, lp):
    t = jnp.linspace(0.0, 1.0, L, dtype=f32)[:, None]
    w = (2.0 * math.pi / L) * jnp.arange(L, dtype=f32)[:, None]
    bands = jnp.linspace(1e-4, HY_BANDS - 1, HY_BANDS, dtype=f32)[None]
    z = jnp.concatenate([t, jnp.cos(bands * w), -jnp.sin(bands * w)], axis=-1)
    freq = lp['hy_freq'].astype(f32)
    a = jnp.sin(freq * (z @ lp['hy_w1'].astype(f32) + lp['hy_b1'].astype(f32)))
    a = jnp.sin(freq * (a @ lp['hy_w2'].astype(f32) + lp['hy_b2'].astype(f32)))
    h = a @ lp['hy_w3'].astype(f32) + lp['hy_b3'].astype(f32)
    deltas = jnp.linspace(math.log(HY_TARGET) / HY_SLOW, math.log(HY_TARGET) / HY_FAST,
                          HY_WIDTH, dtype=f32)
    window = jnp.exp(-t * jnp.abs(deltas)) + HY_SHIFT
    h = h.reshape(L, 2, HY_WIDTH) * window[:, None, :]
    return h[:, 0], h[:, 1]


def bidir_longconv(u, h_f, h_b, skip):
    L = u.shape[1]
    f = jnp.concatenate([h_f, jnp.zeros((1, h_f.shape[1]), f32), h_b[:0:-1]], axis=0)
    uf = u.astype(f32)
    y = jnp.fft.irfft(jnp.fft.rfft(uf, n=2 * L, axis=1) * jnp.fft.rfft(f, axis=0)[None],
                      n=2 * L, axis=1)[:, :L]
    return (y + uf * skip.astype(f32)).astype(u.dtype)


def hyena_branch(p, lp):
    L = p.shape[1]
    uc = short_conv(p, lp['hy_conv_w'], lp['hy_conv_b'])
    x0, x1, v = jnp.split(uc, 3, axis=-1)
    h_f, h_b = hyena_filters(L, lp)
    return x0 * bidir_longconv(v * x1, h_f, h_b, lp['hy_skip'])


def diff_qkv(pq, pk, pv, rope):
    q = to_heads(pq, 2 * DF_HEADS)
    k = to_heads(pk, 2 * DF_HEADS)
    if rope is not None:
        q = apply_rope(q, *rope)
        k = apply_rope(k, *rope)
    B, _, L, _ = q.shape
    q = q.reshape(B, DF_HEADS, 2, L, DF_DIM)
    k = k.reshape(B, DF_HEADS, 2, L, DF_DIM)
    return q, k, to_heads(pv, DF_HEADS)


def diff_lambda(lp, lam_init):
    l1 = jnp.exp(jnp.sum(lp['df_lq1'].astype(f32) * lp['df_lk1'].astype(f32)))
    l2 = jnp.exp(jnp.sum(lp['df_lq2'].astype(f32) * lp['df_lk2'].astype(f32)))
    return l1 - l2 + lam_init


def diff_combine(o1, o2, lam, lam_init, g):
    o = o1 - lam.astype(o1.dtype) * o2
    return from_heads(rmsnorm(o, g) * (1.0 - lam_init))


def merge_branches(ya, yb, yc, pg, lp):
    g = jax.nn.sigmoid(pg.astype(f32)).astype(ya.dtype)
    g_a, g_b, g_c = jnp.split(g, N_BRANCH, axis=-1)
    m = g_a * (ya @ lp['w_br_a']) + g_b * (yb @ lp['w_br_b']) + g_c * (yc @ lp['w_br_c'])
    return m @ lp['w_out']


def sqrelu_mlp(h, w1, w2):
    return jnp.square(jax.nn.relu(h @ w1)) @ w2


def token_mixers(h, hc, lp, lam_init, rope_mla, rope_df, with_ctx):
    pq, pkv, pkr, phy, pdq, pdk, pdv, pg = split_cols(h @ lp['w_in'])
    cq, ckv, ckr, chy, cdq, cdk, cdv, cg = split_cols(hc @ lp['w_in'])

    qa, ka, va = mla_qkv(pq, pkv, pkr, lp, rope_mla)
    qa_c, ka_c, va_c = mla_qkv(cq, ckv, ckr, lp, None)
    ya = from_heads(attend(qa, jnp.concatenate([ka_c, ka], axis=2),
                           jnp.concatenate([va_c, va], axis=2), MLA_SCALE))

    yb = hyena_branch(phy, lp)

    lam = diff_lambda(lp, lam_init)
    qd, kd, vd = diff_qkv(pdq, pdk, pdv, rope_df)
    qd_c, kd_c, vd_c = diff_qkv(cdq, cdk, cdv, None)
    v_all = jnp.concatenate([vd_c, vd], axis=2)
    o1 = attend(qd[:, :, 0], jnp.concatenate([kd_c[:, :, 0], kd[:, :, 0]], axis=2), v_all, DF_SCALE)
    o2 = attend(qd[:, :, 1], jnp.concatenate([kd_c[:, :, 1], kd[:, :, 1]], axis=2), v_all, DF_SCALE)
    yc = diff_combine(o1, o2, lam, lam_init, lp['df_subln_g'])

    y = merge_branches(ya, yb, yc, pg, lp)
    if not with_ctx:
        return y, None

    ya_c = from_heads(attend(qa_c, ka_c, va_c, MLA_SCALE))
    yb_c = hyena_branch(chy, lp)
    o1c = attend(qd_c[:, :, 0], kd_c[:, :, 0], vd_c, DF_SCALE)
    o2c = attend(qd_c[:, :, 1], kd_c[:, :, 1], vd_c, DF_SCALE)
    yc_c = diff_combine(o1c, o2c, lam, lam_init, lp['df_subln_g'])
    y_c = merge_branches(ya_c, yb_c, yc_c, cg, lp)
    return y, y_c


def setup_inputs(seed: int = 0) -> dict:
    key = jax.random.key(seed)
    it = iter(jax.random.split(key, 40))

    def nrm(shape, scale):
        return jax.random.normal(next(it), shape, f32) * scale

    def gain(shape):
        return 1.0 + nrm(shape, 0.05)

    L = DEPTH
    return {
        'x': nrm((BATCH, SEQ, D_MODEL), 1.0),
        'c': nrm((BATCH, D_MODEL), 1.0),
        'ctx': nrm((BATCH, CTX_LEN, D_MODEL), 1.0),
        'c_ctx': nrm((D_MODEL,), 1.0),
        'norm_mix_g': gain((L, D_MODEL)),
        'norm_ffn_g': gain((L, D_MODEL)),
        'w_mod': nrm((L, D_MODEL, 6 * D_MODEL), 0.02),
        'b_mod': nrm((L, 6 * D_MODEL), 0.01),
        'w_in': nrm((L, D_MODEL, D_IN), D_MODEL ** -0.5),
        'mla_q_norm_g': gain((L, MLA_Q_RANK)),
        'mla_w_uq': nrm((L, MLA_Q_RANK, MLA_HEADS * (MLA_NOPE + MLA_ROPE)), MLA_Q_RANK ** -0.5),
        'mla_kv_norm_g': gain((L, MLA_KV_RANK)),
        'mla_w_ukv': nrm((L, MLA_KV_RANK, MLA_HEADS * (MLA_NOPE + MLA_V)), MLA_KV_RANK ** -0.5),
        'hy_conv_w': nrm((L, HY_SHORT, 3 * HY_WIDTH), 0.5),
        'hy_conv_b': nrm((L, 3 * HY_WIDTH), 0.01),
        'hy_w1': nrm((L, HY_EMB, HY_HIDDEN), HY_EMB ** -0.5),
        'hy_b1': nrm((L, HY_HIDDEN), 0.1),
        'hy_freq': gain((L, HY_HIDDEN)),
        'hy_w2': nrm((L, HY_HIDDEN, HY_HIDDEN), HY_HIDDEN ** -0.5),
        'hy_b2': nrm((L, HY_HIDDEN), 0.1),
        'hy_w3': nrm((L, HY_HIDDEN, 2 * HY_WIDTH), 0.005),
        'hy_b3': nrm((L, 2 * HY_WIDTH), 0.001),
        'hy_skip': nrm((L, HY_WIDTH), 0.5),
        'df_lq1': nrm((L, DF_DIM), 0.1),
        'df_lk1': nrm((L, DF_DIM), 0.1),
        'df_lq2': nrm((L, DF_DIM), 0.1),
        'df_lk2': nrm((L, DF_DIM), 0.1),
        'df_subln_g': gain((L, DF_V)),
        'w_br_a': nrm((L, MLA_OUT, D_MODEL), MLA_OUT ** -0.5),
        'w_br_b': nrm((L, HY_WIDTH, D_MODEL), HY_WIDTH ** -0.5),
        'w_br_c': nrm((L, DF_OUT, D_MODEL), DF_OUT ** -0.5),
        'w_out': nrm((L, D_MODEL, D_MODEL), D_MODEL ** -0.5),
        'w_fc1': nrm((L, D_MODEL, D_FF), D_MODEL ** -0.5),
        'w_fc2': nrm((L, D_FF, D_MODEL), D_FF ** -0.5),
        'final_norm_g': gain((D_MODEL,)),
    }


def reference(x, c, ctx, c_ctx, norm_mix_g, norm_ffn_g, w_mod, b_mod, w_in,
              mla_q_norm_g, mla_w_uq, mla_kv_norm_g, mla_w_ukv,
              hy_conv_w, hy_conv_b, hy_w1, hy_b1, hy_freq, hy_w2, hy_b2, hy_w3, hy_b3, hy_skip,
              df_lq1, df_lk1, df_lq2, df_lk2, df_subln_g,
              w_br_a, w_br_b, w_br_c, w_out, w_fc1, w_fc2, final_norm_g):
    n_lat = x.shape[1]
    rope_mla = axial_cos_sin(n_lat, MLA_ROPE)
    rope_df = axial_cos_sin(n_lat, DF_DIM)
    s_c = jax.nn.silu(c)
    s_cc = jax.nn.silu(c_ctx)
    xc = ctx
    for l in range(DEPTH):
        with_ctx = l < DEPTH - 1
        lam_init = 0.8 - 0.6 * math.exp(-0.3 * l)
        lp = {
            'w_in': w_in[l],
            'mla_q_norm_g': mla_q_norm_g[l], 'mla_w_uq': mla_w_uq[l],
            'mla_kv_norm_g': mla_kv_norm_g[l], 'mla_w_ukv': mla_w_ukv[l],
            'hy_conv_w': hy_conv_w[l], 'hy_conv_b': hy_conv_b[l],
            'hy_w1': hy_w1[l], 'hy_b1': hy_b1[l], 'hy_freq': hy_freq[l],
            'hy_w2': hy_w2[l], 'hy_b2': hy_b2[l], 'hy_w3': hy_w3[l], 'hy_b3': hy_b3[l],
            'hy_skip': hy_skip[l],
            'df_lq1': df_lq1[l], 'df_lk1': df_lk1[l], 'df_lq2': df_lq2[l], 'df_lk2': df_lk2[l],
            'df_subln_g': df_subln_g[l],
            'w_br_a': w_br_a[l], 'w_br_b': w_br_b[l], 'w_br_c': w_br_c[l], 'w_out': w_out[l],
        }
        sh_m, sc_m, g_m, sh_f, sc_f, g_f = [t[:, None] for t in
                                            jnp.split(s_c @ w_mod[l] + b_mod[l], 6, axis=-1)]
        csh_m, csc_m, cg_m, csh_f, csc_f, cg_f = jnp.split(s_cc @ w_mod[l] + b_mod[l], 6, axis=-1)

        h = modulate(rmsnorm(x, norm_mix_g[l]), sh_m, sc_m)
        hc = modulate(rmsnorm(xc, norm_mix_g[l]), csh_m, csc_m)
        y, y_c = token_mixers(h, hc, lp, lam_init, rope_mla, rope_df, with_ctx)

        x = x + g_m * y
        x = x + g_f * sqrelu_mlp(modulate(rmsnorm(x, norm_ffn_g[l]), sh_f, sc_f), w_fc1[l], w_fc2[l])
        if with_ctx:
            xc = xc + cg_m * y_c
            xc = xc + cg_f * sqrelu_mlp(modulate(rmsnorm(xc, norm_ffn_g[l]), csh_f, csc_f),
                                        w_fc1[l], w_fc2[l])
    return rmsnorm(x, final_norm_g)
```

```python
import functools
import math

import jax
import jax.numpy as jnp
from jax import lax
from jax.experimental import pallas as pl
from jax.experimental.pallas import tpu as pltpu

f32 = jnp.float32
bf16 = jnp.bfloat16

D_MODEL = 1024
GRID_W = 64
EPS = 1e-6
ROPE_BASE = 10000.0

MLA_HEADS = 8
MLA_NOPE = 64
MLA_ROPE = 32
MLA_V = 64
MLA_Q_RANK = 256
MLA_KV_RANK = 128
MLA_OUT = MLA_HEADS * MLA_V
MLA_SCALE = (MLA_NOPE + MLA_ROPE) ** -0.5

HY_WIDTH = 256
HY_EMB = 33
HY_BANDS = (HY_EMB - 1) // 2
HY_HIDDEN = 64
HY_TARGET = 1e-2
HY_FAST = 0.3
HY_SLOW = 1.5
HY_SHIFT = 0.05

DF_HEADS = 4
DF_DIM = 32
DF_V = 2 * DF_DIM
DF_OUT = DF_HEADS * DF_V
DF_SCALE = DF_DIM ** -0.5

N_BRANCH = 3
D_FF = 4 * D_MODEL

LANES = 128
LOG2E = 1.4426950408889634
VMEM_LIMIT = 56 * 1024 * 1024

C_Q, C_KV, C_KR, C_HY, C_DQ, C_DK, C_DV, C_END = 0, 256, 384, 512, 1280, 1536, 1792, 2048


def _cparams(*sem):
    return pltpu.CompilerParams(dimension_semantics=sem, vmem_limit_bytes=VMEM_LIMIT)


def _dot(a, b):
    return jnp.dot(a, b, preferred_element_type=f32)


def _dot_nt(a, b):
    return lax.dot_general(a, b, (((1,), (1,)), ((), ())), preferred_element_type=f32)


def _rms(x):
    return x * lax.rsqrt(jnp.mean(x * x, axis=-1, keepdims=True) + EPS)


def _const_spec(shape):
    nd = len(shape)
    return pl.BlockSpec(shape, lambda *_: (0,) * nd)


def _mod_kernel(c_ref, w_ref, b_ref, o_ref):
    c = c_ref[...]
    s = c * jax.nn.sigmoid(c)
    o_ref[0] = _dot(s.astype(bf16), w_ref[0]) + b_ref[0]


def _modulation(cc, w_mod, b_mod):
    depth, d, n = w_mod.shape
    r = cc.shape[0]
    tn = 2048
    return pl.pallas_call(
        _mod_kernel,
        grid=(depth, n // tn),
        in_specs=[pl.BlockSpec((r, d), lambda l, j: (0, 0)),
                  pl.BlockSpec((1, d, tn), lambda l, j: (l, 0, j)),
                  pl.BlockSpec((1, 1, tn), lambda l, j: (l, 0, j))],
        out_specs=pl.BlockSpec((1, r, tn), lambda l, j: (l, 0, j)),
        out_shape=jax.ShapeDtypeStruct((depth, r, n), f32),
        compiler_params=_cparams("parallel", "parallel"),
        name="modulation",
    )(cc, w_mod.astype(bf16), b_mod.reshape(depth, 1, n))


def _in_proj_kernel(*refs, rope):
    (x_ref, mod_ref, g_ref, w_ref, qg_ref, kvg_ref, wq_ref, wqs_ref, wk_ref, wv_ref,
     e_ref, es_ref, p_ref) = refs[:13]
    if rope:
        cm_ref, sm_ref, cd_ref, sd_ref = refs[13:17]
        outs = refs[17:]
    else:
        outs = refs[13:]
    qm_ref, km_ref, vm_ref, qd_ref, kd_ref, vd_ref, hy_ref = outs

    x = x_ref[0]
    md = mod_ref[0]
    h = _rms(x) * g_ref[...] * (1.0 + md[1:2]) + md[0:1]
    p = _dot(h.astype(bf16), w_ref[...])

    qn = (_rms(p[:, C_Q:C_KV]) * qg_ref[...]).astype(bf16)
    kvn = (_rms(p[:, C_KV:C_KR]) * kvg_ref[...]).astype(bf16)
    kr = p[:, C_KR:C_HY].astype(bf16)
    q = _dot(qn, wq_ref[...])
    k = _dot(kvn, wk_ref[...]) + _dot(kr, e_ref[...])
    dq = p[:, C_DQ:C_DK]
    dk = p[:, C_DK:C_DV]
    if rope:
        cm = jnp.concatenate([cm_ref[...]] * MLA_HEADS, axis=-1)
        sm = jnp.concatenate([sm_ref[...]] * MLA_HEADS, axis=-1)
        q = q * cm + _dot(qn, wqs_ref[...]) * sm
        k = k * cm + _dot(kr, es_ref[...]) * sm
        cd = jnp.concatenate([cd_ref[...]] * 2, axis=-1)
        sd = jnp.concatenate([sd_ref[...]] * 2, axis=-1)
        dq = dq * cd + _dot(dq.astype(bf16), p_ref[...]) * sd
        dk = dk * cd + _dot(dk.astype(bf16), p_ref[...]) * sd
    qm_ref[0] = (q * (MLA_SCALE * LOG2E)).astype(bf16)
    km_ref[0] = k.astype(bf16)
    vm_ref[0] = _dot(kvn, wv_ref[...]).astype(bf16)
    qd_ref[0] = (dq * (DF_SCALE * LOG2E)).astype(bf16)
    kd_ref[0] = dk.astype(bf16)
    vd_ref[0] = p[:, C_DV:C_END].astype(bf16)
    hy_ref[0] = p[:, C_HY:C_DQ].astype(bf16)


def _in_proj(x, mod, per_batch_mod, g, lw, tables):
    b, lx, d = x.shape
    tm = min(512, lx)
    rope = tables is not None
    tok = lambda n: pl.BlockSpec((1, tm, n), lambda i, j: (j, i, 0))
    mod_map = (lambda i, j: (j, 0, 0)) if per_batch_mod else (lambda i, j: (0, 0, 0))
    weights = [lw['w_in1'], lw['q_norm_g'], lw['kv_norm_g'], lw['wq'], lw['wqs'], lw['wk'], lw['wv'],
               lw['e'], lw['es'], lw['p']]
    in_specs = ([tok(d), pl.BlockSpec((1, 6, d), mod_map), _const_spec(g.shape)]
                + [_const_spec(w.shape) for w in weights])
    args = [x, mod, g] + weights
    if rope:
        in_specs += [pl.BlockSpec((tm, LANES), lambda i, j: (i, 0))] * 4
        args += list(tables)
    widths = (MLA_HEADS * LANES, MLA_HEADS * LANES, MLA_OUT, 2 * DF_HEADS * DF_DIM, 2 * DF_HEADS * DF_DIM,
              DF_OUT, 3 * HY_WIDTH)
    return pl.pallas_call(
        functools.partial(_in_proj_kernel, rope=rope),
        grid=(lx // tm, b),
        in_specs=in_specs,
        out_specs=[tok(n) for n in widths],
        out_shape=[jax.ShapeDtypeStruct((b, lx, n), bf16) for n in widths],
        compiler_params=_cparams("parallel", "parallel"),
        name="in_proj",
    )(*args)


def _softmax_parts(q, k_tiles):
    ss = [_dot_nt(q, k) for k in k_tiles]
    m = functools.reduce(jnp.maximum, [jnp.max(s, axis=-1, keepdims=True) for s in ss])
    ps = [jnp.exp2(s - m) for s in ss]
    l = functools.reduce(jnp.add, [jnp.sum(p, axis=-1, keepdims=True) for p in ps])
    return ps, l


def _pv(ps, v_tiles):
    return functools.reduce(jnp.add, [_dot(p.astype(bf16), v) for p, v in zip(ps, v_tiles)])


def _mla_attn_kernel(*refs, n_src):
    q_ref = refs[0]
    k_refs = refs[1:1 + n_src]
    v_refs = refs[1 + n_src:1 + 2 * n_src]
    o_ref = refs[-1]
    tq = q_ref.shape[1]
    lane = lax.broadcasted_iota(jnp.int32, (tq, LANES), 1)
    for j in range(MLA_HEADS // 2):
        v_tiles = [v[0, :, LANES * j:LANES * (j + 1)] for v in v_refs]
        pair = []
        for e in range(2):
            h = 2 * j + e
            hs = slice(LANES * h, LANES * (h + 1))
            ps, l = _softmax_parts(q_ref[0, :, hs], [k[0, :, hs] for k in k_refs])
            pair.append(_pv(ps, v_tiles) / l)
        o_ref[0, :, LANES * j:LANES * (j + 1)] = jnp.where(lane < MLA_V, pair[0], pair[1]).astype(bf16)


def _diff_attn_kernel(*refs, n_src, lam_init):
    lq1, lk1, lq2, lk2, g_ref, q_ref = refs[:6]
    k_refs = refs[6:6 + n_src]
    v_refs = refs[6 + n_src:6 + 2 * n_src]
    o_ref = refs[-1]
    tq = q_ref.shape[1]
    lam = (jnp.exp(jnp.sum(lq1[...] * lk1[...], axis=-1, keepdims=True))
           - jnp.exp(jnp.sum(lq2[...] * lk2[...], axis=-1, keepdims=True)) + lam_init)
    lane = lax.broadcasted_iota(jnp.int32, (tq, LANES), 1)
    for t in range(DF_OUT // LANES):
        ts = slice(LANES * t, LANES * (t + 1))
        q = q_ref[0, :, ts]
        k_tiles = [k[0, :, ts] for k in k_refs]
        v_tiles = [v[0, :, ts] for v in v_refs]
        pair = []
        for e in range(2):
            sub = []
            for s in range(2):
                lo = DF_V * e + DF_DIM * s
                qs = jnp.where(lane < lo, 0.0, jnp.where(lane < lo + DF_DIM, q, 0.0)).astype(bf16)
                ps, l = _softmax_parts(qs, k_tiles)
                sub.append(_pv(ps, v_tiles) / l)
            pair.append(sub[0] - lam * sub[1])
        o = jnp.where(lane < DF_V, pair[0], pair[1])
        o2 = o * o
        ms_lo = jnp.sum(jnp.where(lane < DF_V, o2, 0.0), axis=-1, keepdims=True) * (1.0 / DF_V)
        ms_hi = jnp.sum(jnp.where(lane < DF_V, 0.0, o2), axis=-1, keepdims=True) * (1.0 / DF_V)
        ms = jnp.where(lane < DF_V, ms_lo, ms_hi)
        y = o * lax.rsqrt(ms + EPS) * g_ref[...] * (1.0 - lam_init)
        o_ref[0, :, ts] = y.astype(bf16)


def _attn_specs(q, ks, vs, tq):
    b, lq, _ = q.shape
    spec_q = pl.BlockSpec((1, tq, q.shape[2]), lambda i, j: (i, j, 0))
    full = lambda a: pl.BlockSpec((1,) + a.shape[1:], lambda i, j: (i, 0, 0))
    return (b, lq // tq), spec_q, [full(a) for a in ks] + [full(a) for a in vs]


def _mla_attn(q, ks, vs):
    b, lq, _ = q.shape
    tq = min(256, lq)
    grid, spec_q, kv_specs = _attn_specs(q, ks, vs, tq)
    return pl.pallas_call(
        functools.partial(_mla_attn_kernel, n_src=len(ks)),
        grid=grid,
        in_specs=[spec_q] + kv_specs,
        out_specs=pl.BlockSpec((1, tq, MLA_OUT), lambda i, j: (i, j, 0)),
        out_shape=jax.ShapeDtypeStruct((b, lq, MLA_OUT), bf16),
        compiler_params=_cparams("parallel", "parallel"),
        name="mla_attn",
    )(q, *ks, *vs)


def _diff_attn(lams, g, q, ks, vs, lam_init):
    b, lq, _ = q.shape
    tq = min(256, lq)
    grid, spec_q, kv_specs = _attn_specs(q, ks, vs, tq)
    small = [_const_spec(a.shape) for a in lams] + [_const_spec(g.shape)]
    return pl.pallas_call(
        functools.partial(_diff_attn_kernel, n_src=len(ks), lam_init=lam_init),
        grid=grid,
        in_specs=small + [spec_q] + kv_specs,
        out_specs=pl.BlockSpec((1, tq, DF_OUT), lambda i, j: (i, j, 0)),
        out_shape=jax.ShapeDtypeStruct((b, lq, DF_OUT), bf16),
        compiler_params=_cparams("parallel", "parallel"),
        name="diff_attn",
    )(*lams, g, q, *ks, *vs)


def _hy_filter_kernel(z_ref, w1_ref, b1_ref, fr_ref, w2_ref, b2_ref, w3_ref, b3_ref, win_ref,
                      fc_ref, fs_ref, a_ref, a2_ref, bq_ref, h_scr):
    kt = pl.program_id(0)
    hp = lax.Precision.HIGHEST

    @pl.when(kt == 0)
    def _():
        fr = fr_ref[...]
        a = jnp.sin(fr * (jnp.dot(z_ref[...], w1_ref[...], precision=hp, preferred_element_type=f32)
                          + b1_ref[...]))
        a = jnp.sin(fr * (jnp.dot(a, w2_ref[...], precision=hp, preferred_element_type=f32) + b2_ref[...]))
        h = jnp.dot(a, w3_ref[...], precision=hp, preferred_element_type=f32) + b3_ref[...]
        win = win_ref[...]
        row = lax.broadcasted_iota(jnp.int32, win.shape, 0)
        h_scr[:, :HY_WIDTH] = (h[:, :HY_WIDTH] * win).astype(bf16)
        h_scr[:, HY_WIDTH:] = jnp.where(row == 0, 0.0, h[:, HY_WIDTH:] * win).astype(bf16)

    hh = h_scr[...]
    c = _dot(fc_ref[...], hh)
    s = _dot(fs_ref[...], hh)
    tk = c.shape[0]
    row = lax.broadcasted_iota(jnp.int32, (tk, HY_WIDTH), 0) + kt * tk
    first = row == 0
    a = c[:, :HY_WIDTH] + c[:, HY_WIDTH:]
    second = jnp.where(first, s[:, :HY_WIDTH] + s[:, HY_WIDTH:], s[:, :HY_WIDTH] - s[:, HY_WIDTH:])
    a_ref[...] = a
    a2_ref[...] = jnp.where(first, second, a)
    bq_ref[...] = jnp.where(first, 0.0, second)


def _hy_filter(z, lw, win, fmat):
    l = z.shape[0]
    tk = min(512, l)
    nk = l // tk
    small = [lw['hy_w1'], lw['hy_b1'], lw['hy_freq'], lw['hy_w2'], lw['hy_b2'], lw['hy_w3'], lw['hy_b3']]
    out = jax.ShapeDtypeStruct((l, HY_WIDTH), f32)
    return pl.pallas_call(
        _hy_filter_kernel,
        grid=(nk,),
        in_specs=([_const_spec(z.shape)] + [_const_spec(a.shape) for a in small] + [_const_spec(win.shape)]
                  + [pl.BlockSpec((tk, l), lambda k: (k, 0)), pl.BlockSpec((tk, l), lambda k: (k + nk, 0))]),
        out_specs=[pl.BlockSpec((tk, HY_WIDTH), lambda k: (k, 0))] * 3,
        out_shape=[out] * 3,
        scratch_shapes=[pltpu.VMEM((l, 2 * HY_WIDTH), bf16)],
        compiler_params=_cparams("arbitrary"),
        name="hy_filter",
    )(z, *small, win, fmat, fmat)


def _hy_conv_kernel(p_ref, cw_ref, cb_ref, skip_ref, fc_ref, fs_ref, gc_ref, gs_ref,
                    a_ref, a2_ref, bq_ref, o_ref, u_scr, x0_scr, acc_scr, *, bb):
    kt = pl.program_id(1)
    l = p_ref.shape[1]

    @pl.when(kt == 0)
    def _():
        row = lax.broadcasted_iota(jnp.int32, (l, 3 * HY_WIDTH), 0)
        cw = cw_ref[...]
        for i in range(bb):
            p = p_ref[i].astype(f32)
            prev = jnp.where(row == 0, 0.0, pltpu.roll(p, 1, axis=0))
            nxt = jnp.where(row == l - 1, 0.0, pltpu.roll(p, l - 1, axis=0))
            uc = cw[0:1] * prev + cw[1:2] * p + cw[2:3] * nxt + cb_ref[...]
            x0_scr[i] = uc[:, :HY_WIDTH]
            u_scr[i] = uc[:, 2 * HY_WIDTH:] * uc[:, HY_WIDTH:2 * HY_WIDTH]
            acc_scr[i] = jnp.zeros((l, HY_WIDTH), f32)

    a, a2, bq = a_ref[...], a2_ref[...], bq_ref[...]
    for i in range(bb):
        u = u_scr[i].astype(bf16)
        ur = _dot(fc_ref[...], u)
        ui = _dot(fs_ref[...], u)
        zr = (ur * a - ui * bq).astype(bf16)
        zi = (ur * bq + ui * a2).astype(bf16)
        acc_scr[i] += _dot(gc_ref[...], zr) + _dot(gs_ref[...], zi)

    @pl.when(kt == pl.num_programs(1) - 1)
    def _():
        for i in range(bb):
            o_ref[i] = (x0_scr[i] * (acc_scr[i] + skip_ref[...] * u_scr[i])).astype(bf16)


def _hy_conv(p_hy, lw, spec, fmat, gmat):
    b, l, _ = p_hy.shape
    bb = 2
    tk = min(512, l)
    nk = l // tk
    a, a2, bq = spec
    return pl.pallas_call(
        functools.partial(_hy_conv_kernel, bb=bb),
        grid=(b // bb, nk),
        in_specs=[pl.BlockSpec((bb, l, 3 * HY_WIDTH), lambda i, k: (i, 0, 0)),
                  _const_spec(lw['hy_conv_w'].shape), _const_spec(lw['hy_conv_b'].shape),
                  _const_spec(lw['hy_skip'].shape),
                  pl.BlockSpec((tk, l), lambda i, k: (k, 0)), pl.BlockSpec((tk, l), lambda i, k: (k + nk, 0)),
                  pl.BlockSpec((l, tk), lambda i, k: (0, k)), pl.BlockSpec((l, tk), lambda i, k: (0, k + nk))]
                 + [pl.BlockSpec((tk, HY_WIDTH), lambda i, k: (k, 0))] * 3,
        out_specs=pl.BlockSpec((bb, l, HY_WIDTH), lambda i, k: (i, 0, 0)),
        out_shape=jax.ShapeDtypeStruct((b, l, HY_WIDTH), bf16),
        scratch_shapes=[pltpu.VMEM((bb, l, HY_WIDTH), f32)] * 3,
        compiler_params=_cparams("parallel", "arbitrary"),
        name="hy_conv",
    )(p_hy, lw['hy_conv_w'], lw['hy_conv_b'], lw['hy_skip'], fmat, fmat, gmat, gmat, a, a2, bq)


def _merge_kernel(x_ref, mod_ref, g_ref, ya_ref, yb_ref, yc_ref, wg_ref, wa_ref, wb_ref, wc_ref, wo_ref,
                  o_ref):
    x = x_ref[0]
    md = mod_ref[0]
    h = (_rms(x) * g_ref[...] * (1.0 + md[1:2]) + md[0:1]).astype(bf16)
    d = D_MODEL
    m = (jax.nn.sigmoid(_dot(h, wg_ref[:, 0:d])) * _dot(ya_ref[0], wa_ref[...])
         + jax.nn.sigmoid(_dot(h, wg_ref[:, d:2 * d])) * _dot(yb_ref[0], wb_ref[...])
         + jax.nn.sigmoid(_dot(h, wg_ref[:, 2 * d:3 * d])) * _dot(yc_ref[0], wc_ref[...]))
    o_ref[0] = x + md[2:3] * _dot(m.astype(bf16), wo_ref[...])


def _merge(x, mod, per_batch_mod, g, ya, yb, yc, lw):
    b, lx, d = x.shape
    tm = min(512, lx)
    tok = lambda n: pl.BlockSpec((1, tm, n), lambda i, j: (i, j, 0))
    mod_map = (lambda i, j: (i, 0, 0)) if per_batch_mod else (lambda i, j: (0, 0, 0))
    weights = [lw['w_gate'], lw['w_br_a'], lw['w_br_b'], lw['w_br_c'], lw['w_out']]
    return pl.pallas_call(
        _merge_kernel,
        grid=(b, lx // tm),
        in_specs=[tok(d), pl.BlockSpec((1, 6, d), mod_map), _const_spec(g.shape),
                  tok(MLA_OUT), tok(HY_WIDTH), tok(DF_OUT)] + [_const_spec(w.shape) for w in weights],
        out_specs=tok(d),
        out_shape=jax.ShapeDtypeStruct(x.shape, f32),
        compiler_params=_cparams("parallel", "parallel"),
        name="merge",
    )(x, mod, g, ya, yb, yc, *weights)


def _mlp_kernel(*refs, final):
    x_ref, mod_ref, g_ref, w1_ref, w2_ref = refs[:5]
    o_ref = refs[-1]
    x = x_ref[0]
    md = mod_ref[0]
    h = (_rms(x) * g_ref[...] * (1.0 + md[4:5]) + md[3:4]).astype(bf16)
    a = jnp.maximum(_dot(h, w1_ref[...]), 0.0)
    y = x + md[5:6] * _dot((a * a).astype(bf16), w2_ref[...])
    if final:
        y = _rms(y) * refs[5][...]
    o_ref[0] = y


def _mlp(x, mod, per_batch_mod, g, w1, w2, final_g=None):
    b, lx, d = x.shape
    tm = min(256, lx)
    tok = pl.BlockSpec((1, tm, d), lambda i, j: (i, j, 0))
    mod_map = (lambda i, j: (i, 0, 0)) if per_batch_mod else (lambda i, j: (0, 0, 0))
    final = final_g is not None
    in_specs = [tok, pl.BlockSpec((1, 6, d), mod_map), _const_spec(g.shape),
                _const_spec(w1.shape), _const_spec(w2.shape)]
    args = [x, mod, g, w1, w2]
    if final:
        in_specs.append(_const_spec(final_g.shape))
        args.append(final_g)
    return pl.pallas_call(
        functools.partial(_mlp_kernel, final=final),
        grid=(b, lx // tm),
        in_specs=in_specs,
        out_specs=tok,
        out_shape=jax.ShapeDtypeStruct(x.shape, f32),
        compiler_params=_cparams("parallel", "parallel"),
        name="mlp",
    )(*args)


def _rope_tables(n_tokens):
    rows = n_tokens // GRID_W
    row = jnp.repeat(jnp.arange(rows), GRID_W).astype(f32)
    col = jnp.tile(jnp.arange(GRID_W), rows).astype(f32)
    nf = MLA_ROPE // 4
    inv = ROPE_BASE ** (-jnp.arange(nf, dtype=f32) / nf)
    ang = jnp.concatenate([row[:, None] * inv, col[:, None] * inv], axis=-1)
    cos, sin = jnp.cos(ang), jnp.sin(ang)
    one = jnp.ones((n_tokens, MLA_NOPE), f32)
    zero = jnp.zeros((n_tokens, MLA_NOPE), f32)
    pad = jnp.zeros((n_tokens, LANES - MLA_NOPE - MLA_ROPE), f32)
    cm = jnp.concatenate([one, cos, cos, pad], axis=-1)
    sm = jnp.concatenate([zero, -sin, sin, pad], axis=-1)
    cd = jnp.tile(jnp.concatenate([cos, cos], axis=-1), (1, LANES // DF_DIM))
    sd = jnp.tile(jnp.concatenate([-sin, sin], axis=-1), (1, LANES // DF_DIM))
    return cm, sm, cd, sd


def _hy_positional(l):
    t = jnp.linspace(0.0, 1.0, l, dtype=f32)[:, None]
    w = (2.0 * math.pi / l) * jnp.arange(l, dtype=f32)[:, None]
    bands = jnp.linspace(1e-4, HY_BANDS - 1, HY_BANDS, dtype=f32)[None]
    z = jnp.concatenate([t, jnp.cos(bands * w), -jnp.sin(bands * w)], axis=-1)
    z = jnp.pad(z, ((0, 0), (0, LANES - HY_EMB)))
    deltas = jnp.linspace(math.log(HY_TARGET) / HY_SLOW, math.log(HY_TARGET) / HY_FAST, HY_WIDTH, dtype=f32)
    win = jnp.exp(-t * jnp.abs(deltas)) + HY_SHIFT
    return z, win


def _dft_tables(l):
    n2 = 2 * l
    k = jnp.arange(l, dtype=jnp.int32)[:, None]
    n = jnp.arange(l, dtype=jnp.int32)[None, :]
    theta = ((k * n) % n2).astype(f32) * (2.0 * math.pi / n2)
    c = jnp.cos(theta)
    s = -jnp.sin(theta)
    s = s.at[0].set(jnp.where(n[0] % 2 == 0, 1.0, -1.0))
    fmat = jnp.concatenate([c, s], axis=0)
    wgt = jnp.full((n2,), 2.0 / n2, f32).at[0].set(1.0 / n2).at[l].set(1.0 / n2)
    gmat = fmat.T * wgt[None, :]
    return fmat.astype(bf16), gmat.astype(bf16)


def _prep_weights(w_in, mla_q_norm_g, mla_w_uq, mla_kv_norm_g, mla_w_ukv, hy_conv_w, hy_conv_b, hy_w1, hy_b1,
                  hy_freq, hy_w2, hy_b2, hy_w3, hy_b3, hy_skip, df_subln_g, w_br_a, w_br_b, w_br_c, w_out,
                  w_fc1, w_fc2):
    depth = w_in.shape[0]
    o = [0]
    for n in (MLA_Q_RANK, MLA_KV_RANK, MLA_ROPE, 3 * HY_WIDTH, 2 * DF_HEADS * DF_DIM, 2 * DF_HEADS * DF_DIM,
              DF_OUT, N_BRANCH * D_MODEL):
        o.append(o[-1] + n)
    kr_pad = jnp.zeros((depth, D_MODEL, LANES - MLA_ROPE), f32)
    w_in1 = jnp.concatenate([w_in[..., o[0]:o[3]], kr_pad, w_in[..., o[3]:o[7]]], axis=-1).astype(bf16)
    w_gate = w_in[..., o[7]:o[8]].astype(bf16)

    uq = mla_w_uq.reshape(depth, MLA_Q_RANK, MLA_HEADS, MLA_NOPE + MLA_ROPE)
    nope, rope = uq[..., :MLA_NOPE], uq[..., MLA_NOPE:]
    half = MLA_ROPE // 2
    zn = jnp.zeros_like(nope)
    zp = jnp.zeros(uq.shape[:3] + (LANES - MLA_NOPE - MLA_ROPE,), f32)
    wq = jnp.concatenate([nope, rope, zp], axis=-1).reshape(depth, MLA_Q_RANK, MLA_HEADS * LANES)
    wqs = jnp.concatenate([zn, rope[..., half:], rope[..., :half], zp], axis=-1)
    wqs = wqs.reshape(depth, MLA_Q_RANK, MLA_HEADS * LANES)
    ukv = mla_w_ukv.reshape(depth, MLA_KV_RANK, MLA_HEADS, MLA_NOPE + MLA_V)
    wk = jnp.concatenate([ukv[..., :MLA_NOPE], jnp.zeros_like(ukv[..., MLA_NOPE:])], axis=-1)
    wk = wk.reshape(depth, MLA_KV_RANK, MLA_HEADS * LANES)
    wv = ukv[..., MLA_NOPE:].reshape(depth, MLA_KV_RANK, MLA_OUT)

    i = jnp.arange(LANES)[:, None]
    j = jnp.arange(MLA_HEADS * LANES)[None, :]
    js = j % LANES - MLA_NOPE
    e = ((i < MLA_ROPE) & (js == i)).astype(bf16)
    es = ((i < MLA_ROPE) & (js >= 0) & (js < MLA_ROPE) & (js == (i + half) % MLA_ROPE)).astype(bf16)
    a = jnp.arange(2 * DF_HEADS * DF_DIM)
    dh = DF_DIM // 2
    swap = (a // DF_DIM) * DF_DIM + (a % DF_DIM + dh) % DF_DIM
    p = (a[:, None] == swap[None, :]).astype(bf16)

    pad_rows = lambda w, r: jnp.pad(w, ((0, 0), (0, r - w.shape[1]), (0, 0)))
    pad_cols = lambda w, c: jnp.pad(w, ((0, 0),) * (w.ndim - 1) + ((0, c - w.shape[-1]),))
    layers = []
    for l in range(depth):
        layers.append({
            'w_in1': w_in1[l], 'w_gate': w_gate[l],
            'q_norm_g': mla_q_norm_g[l][None], 'kv_norm_g': mla_kv_norm_g[l][None],
            'wq': wq[l].astype(bf16), 'wqs': wqs[l].astype(bf16), 'wk': wk[l].astype(bf16),
            'wv': wv[l].astype(bf16), 'e': e, 'es': es, 'p': p,
            'hy_conv_w': hy_conv_w[l], 'hy_conv_b': hy_conv_b[l][None], 'hy_skip': hy_skip[l][None],
            'hy_w1': pad_cols(pad_rows(hy_w1, LANES), LANES)[l],
            'hy_b1': pad_cols(hy_b1, LANES)[l][None], 'hy_freq': pad_cols(hy_freq, LANES)[l][None],
            'hy_w2': pad_cols(pad_rows(hy_w2, LANES), LANES)[l],
            'hy_b2': pad_cols(hy_b2, LANES)[l][None],
            'hy_w3': pad_rows(hy_w3, LANES)[l], 'hy_b3': hy_b3[l][None],
            'subln_g': jnp.tile(df_subln_g[l], LANES // DF_V)[None],
            'w_br_a': w_br_a[l].astype(bf16), 'w_br_b': w_br_b[l].astype(bf16),
            'w_br_c': w_br_c[l].astype(bf16), 'w_out': w_out[l].astype(bf16),
            'w_fc1': w_fc1[l].astype(bf16), 'w_fc2': w_fc2[l].astype(bf16),
        })
    return layers


def _token_mixers(q_parts, ctx_parts, lw, lams, lam_init, hy_lat, hy_ctx, with_ctx):
    qm, km, vm, qd, kd, vd, hy = q_parts
    cqm, ckm, cvm, cqd, ckd, cvd, chy = ctx_parts
    ya = _mla_attn(qm, [ckm, km], [cvm, vm])
    yc = _diff_attn(lams, lw['subln_g'], qd, [ckd, kd], [cvd, vd], lam_init)
    yb = _hy_conv(hy, lw, *hy_lat)
    if not with_ctx:
        return (ya, yb, yc), None
    ya_c = _mla_attn(cqm, [ckm], [cvm])
    yc_c = _diff_attn(lams, lw['subln_g'], cqd, [ckd], [cvd], lam_init)
    yb_c = _hy_conv(chy, lw, *hy_ctx)
    return (ya, yb, yc), (ya_c, yb_c, yc_c)


def kernel(x, c, ctx, c_ctx, norm_mix_g, norm_ffn_g, w_mod, b_mod, w_in, mla_q_norm_g, mla_w_uq, mla_kv_norm_g, mla_w_ukv, hy_conv_w, hy_conv_b, hy_w1, hy_b1, hy_freq, hy_w2, hy_b2, hy_w3, hy_b3, hy_skip, df_lq1, df_lk1, df_lq2, df_lk2, df_subln_g, w_br_a, w_br_b, w_br_c, w_out, w_fc1, w_fc2, final_norm_g):
    b, n_lat, d = x.shape
    n_ctx = ctx.shape[1]
    depth = w_in.shape[0]
    layers = _prep_weights(w_in, mla_q_norm_g, mla_w_uq, mla_kv_norm_g, mla_w_ukv, hy_conv_w, hy_conv_b,
                           hy_w1, hy_b1, hy_freq, hy_w2, hy_b2, hy_w3, hy_b3, hy_skip, df_subln_g,
                           w_br_a, w_br_b, w_br_c, w_out, w_fc1, w_fc2)
    tables = _rope_tables(n_lat)
    z_lat, win_lat = _hy_positional(n_lat)
    z_ctx, win_ctx = _hy_positional(n_ctx)
    f_lat, g_lat = _dft_tables(n_lat)
    f_ctx, g_ctx = _dft_tables(n_ctx)

    rows = -(-(b + 1) // 8) * 8
    cc = jnp.concatenate([c, c_ctx[None], jnp.zeros((rows - b - 1, d), f32)], axis=0)
    mod = _modulation(cc, w_mod, b_mod).reshape(depth, rows, 6, d)

    xc = ctx
    for l in range(depth):
        lw = layers[l]
        with_ctx = l < depth - 1
        lam_init = 0.8 - 0.6 * math.exp(-0.3 * l)
        mod_lat, mod_ctx = mod[l, :b], mod[l, b:b + 1]
        g_mix, g_ffn = norm_mix_g[l][None], norm_ffn_g[l][None]
        lams = [df_lq1[l][None], df_lk1[l][None], df_lq2[l][None], df_lk2[l][None]]

        hy_lat = (_hy_filter(z_lat, lw, win_lat, f_lat), f_lat, g_lat)
        hy_ctx = (_hy_filter(z_ctx, lw, win_ctx, f_ctx), f_ctx, g_ctx) if with_ctx else None

        parts = _in_proj(x, mod_lat, True, g_mix, lw, tables)
        parts_c = _in_proj(xc, mod_ctx, False, g_mix, lw, None)
        ys, ys_c = _token_mixers(parts, parts_c, lw, lams, lam_init, hy_lat, hy_ctx, with_ctx)

        x = _merge(x, mod_lat, True, g_mix, *ys, lw)
        x = _mlp(x, mod_lat, True, g_ffn, lw['w_fc1'], lw['w_fc2'],
                 final_norm_g[None] if l == depth - 1 else None)
        if with_ctx:
            xc = _merge(xc, mod_ctx, False, g_mix, *ys_c, lw)
            xc = _mlp(xc, mod_ctx, False, g_ffn, lw['w_fc1'], lw['w_fc2'])
    return x
```

```python
import functools
import math

import jax
import jax.numpy as jnp
from jax import lax
from jax.experimental import pallas as pl
from jax.experimental.pallas import tpu as pltpu

f32 = jnp.float32
bf16 = jnp.bfloat16

D_MODEL = 1024
GRID_W = 64
EPS = 1e-6
ROPE_BASE = 10000.0

MLA_HEADS = 8
MLA_NOPE = 64
MLA_ROPE = 32
MLA_V = 64
MLA_Q_RANK = 256
MLA_KV_RANK = 128
MLA_OUT = MLA_HEADS * MLA_V
MLA_SCALE = (MLA_NOPE + MLA_ROPE) ** -0.5

HY_WIDTH = 256
HY_EMB = 33
HY_BANDS = (HY_EMB - 1) // 2
HY_HIDDEN = 64
HY_TARGET = 1e-2
HY_FAST = 0.3
HY_SLOW = 1.5
HY_SHIFT = 0.05

DF_HEADS = 4
DF_DIM = 32
DF_V = 2 * DF_DIM
DF_OUT = DF_HEADS * DF_V
DF_SCALE = DF_DIM ** -0.5

N_BRANCH = 3
D_FF = 4 * D_MODEL

LANES = 128
BF16_ROWS = 16
KEY_CHUNK = 256
LOG2E = 1.4426950408889634
VMEM_LIMIT = 56 * 1024 * 1024

C_Q, C_KV, C_KR, C_HY, C_DQ, C_DK, C_DV, C_END = 0, 256, 384, 512, 1280, 1536, 1792, 2048


def _cparams(*sem):
    return pltpu.CompilerParams(dimension_semantics=sem, vmem_limit_bytes=VMEM_LIMIT)


def _dot(a, b):
    return jnp.dot(a, b, preferred_element_type=f32)


def _dot_nt(a, b):
    return lax.dot_general(a, b, (((1,), (1,)), ((), ())), preferred_element_type=f32)


def _rms(x):
    return x * lax.rsqrt(jnp.mean(x * x, axis=-1, keepdims=True) + EPS)


def _const_spec(shape):
    nd = len(shape)
    return pl.BlockSpec(shape, lambda *_: (0,) * nd)


def _mod_kernel(c_ref, w_ref, b_ref, o_ref):
    c = c_ref[...]
    s = c * jax.nn.sigmoid(c)
    o_ref[0] = _dot(s.astype(bf16), w_ref[0]) + b_ref[0]


def _modulation(cc, w_mod, b_mod):
    depth, d, n = w_mod.shape
    r = cc.shape[0]
    tn = 2048
    return pl.pallas_call(
        _mod_kernel,
        grid=(depth, n // tn),
        in_specs=[pl.BlockSpec((r, d), lambda l, j: (0, 0)),
                  pl.BlockSpec((1, d, tn), lambda l, j: (l, 0, j)),
                  pl.BlockSpec((1, 1, tn), lambda l, j: (l, 0, j))],
        out_specs=pl.BlockSpec((1, r, tn), lambda l, j: (l, 0, j)),
        out_shape=jax.ShapeDtypeStruct((depth, r, n), f32),
        compiler_params=_cparams("parallel", "parallel"),
        name="modulation",
    )(cc, w_mod.astype(bf16), b_mod.reshape(depth, 1, n))


def _in_proj_kernel(*refs, rope):
    (x_ref, mod_ref, g_ref, w_ref, qg_ref, kvg_ref, wq_ref, wqs_ref, wk_ref, wv_ref,
     e_ref, es_ref, p_ref) = refs[:13]
    if rope:
        cm_ref, sm_ref, cd_ref, sd_ref = refs[13:17]
        outs = refs[17:]
    else:
        outs = refs[13:]
    qm_ref, km_ref, vm_ref, qd_ref, kd_ref, vd_ref, hy_ref = outs

    x = x_ref[0]
    md = mod_ref[0]
    h = _rms(x) * g_ref[...] * (1.0 + md[1:2]) + md[0:1]
    p = _dot(h.astype(bf16), w_ref[...])

    qn = (_rms(p[:, C_Q:C_KV]) * qg_ref[...]).astype(bf16)
    kvn = (_rms(p[:, C_KV:C_KR]) * kvg_ref[...]).astype(bf16)
    kr = p[:, C_KR:C_HY].astype(bf16)
    q = _dot(qn, wq_ref[...])
    k = _dot(kvn, wk_ref[...]) + _dot(kr, e_ref[...])
    dq = p[:, C_DQ:C_DK]
    dk = p[:, C_DK:C_DV]
    if rope:
        cm = jnp.concatenate([cm_ref[...]] * MLA_HEADS, axis=-1)
        sm = jnp.concatenate([sm_ref[...]] * MLA_HEADS, axis=-1)
        q = q * cm + _dot(qn, wqs_ref[...]) * sm
        k = k * cm + _dot(kr, es_ref[...]) * sm
        cd = jnp.concatenate([cd_ref[...]] * 2, axis=-1)
        sd = jnp.concatenate([sd_ref[...]] * 2, axis=-1)
        dq = dq * cd + _dot(dq.astype(bf16), p_ref[...]) * sd
        dk = dk * cd + _dot(dk.astype(bf16), p_ref[...]) * sd
    qm_ref[0] = (q * (MLA_SCALE * LOG2E)).T.astype(bf16)
    km_ref[0] = k.astype(bf16)
    vm_ref[0] = _dot(kvn, wv_ref[...]).T.astype(bf16)
    qd_ref[0] = (dq * (DF_SCALE * LOG2E)).T.astype(bf16)
    kd_ref[0] = dk.astype(bf16)
    vd_ref[0] = p[:, C_DV:C_END].T.astype(bf16)
    hy_ref[0] = p[:, C_HY:C_DQ].astype(bf16)


def _in_proj(x, mod, per_batch_mod, g, lw, tables):
    b, lx, d = x.shape
    tm = min(512, lx)
    rope = tables is not None
    tok = lambda n: pl.BlockSpec((1, tm, n), lambda i, j: (j, i, 0))
    mod_map = (lambda i, j: (j, 0, 0)) if per_batch_mod else (lambda i, j: (0, 0, 0))
    weights = [lw['w_in1'], lw['q_norm_g'], lw['kv_norm_g'], lw['wq'], lw['wqs'], lw['wk'], lw['wv'],
               lw['e'], lw['es'], lw['p']]
    in_specs = ([tok(d), pl.BlockSpec((1, 6, d), mod_map), _const_spec(g.shape)]
                + [_const_spec(w.shape) for w in weights])
    args = [x, mod, g] + weights
    if rope:
        in_specs += [pl.BlockSpec((tm, LANES), lambda i, j: (i, 0))] * 4
        args += list(tables)
    outs = ((MLA_HEADS * LANES, True), (MLA_HEADS * LANES, False), (MLA_OUT, True),
            (2 * DF_HEADS * DF_DIM, True), (2 * DF_HEADS * DF_DIM, False), (DF_OUT, True),
            (3 * HY_WIDTH, False))
    feat = lambda n: pl.BlockSpec((1, n, tm), lambda i, j: (j, 0, i))
    return pl.pallas_call(
        functools.partial(_in_proj_kernel, rope=rope),
        grid=(lx // tm, b),
        in_specs=in_specs,
        out_specs=[feat(n) if fm else tok(n) for n, fm in outs],
        out_shape=[jax.ShapeDtypeStruct((b, n, lx) if fm else (b, lx, n), bf16) for n, fm in outs],
        compiler_params=_cparams("parallel", "parallel"),
        name="in_proj",
    )(*args)


def _gather_sources(k_refs, vt_refs, k_all, vt_all):
    @pl.when(pl.program_id(1) == 0)
    def _():
        o = 0
        for k, vt in zip(k_refs, vt_refs):
            n = k.shape[1]
            for t in range(k_all.shape[0]):
                k_all[t, o:o + n, :] = k[0, :, LANES * t:LANES * (t + 1)]
            vt_all[:, o:o + n] = vt[0]
            o += n


def _softmax_maps(n_maps, qt_of, k_of, vt_of, s_scr, o_scr):
    _, lk, tq = s_scr.shape
    dv = o_scr.shape[0] // n_maps
    kc = KEY_CHUNK if lk % KEY_CHUNK == 0 else KEY_CHUNK // 2
    assert lk % kc == 0
    ones = jnp.ones((BF16_ROWS, kc), bf16)

    def scores(n, slot):
        qt = qt_of(n)
        pm = None
        for off in range(0, lk, kc):
            s = _dot(k_of(n, off, kc), qt)
            s_scr[slot, off:off + kc, :] = s
            cm = jnp.max(s.reshape(kc // 8, 8, tq), axis=0)
            pm = cm if pm is None else jnp.maximum(pm, cm)
        return jnp.max(pm, axis=0, keepdims=True)

    def values(n, slot, m):
        acc = None
        for off in range(0, lk, kc):
            p = jnp.exp2(s_scr[slot, off:off + kc, :] - m).astype(bf16)
            part = _dot(jnp.concatenate([vt_of(n, off, kc), ones], axis=0), p)
            acc = part if acc is None else acc + part
        row0 = n * dv if isinstance(n, int) else pl.multiple_of(n * dv, dv)
        o_scr[pl.ds(row0, dv), :] = acc[:dv] / acc[dv:dv + 1]

    def step(i, m_even):
        m_odd = scores(2 * i + 1, 1)
        values(2 * i, 0, m_even)
        m_even = scores(2 * i + 2, 0)
        values(2 * i + 1, 1, m_odd)
        return m_even

    assert n_maps % 2 == 0 and n_maps >= 6
    m_even = lax.fori_loop(0, n_maps // 2 - 1, step, scores(0, 0))
    m_odd = scores(n_maps - 1, 1)
    values(n_maps - 2, 0, m_even)
    values(n_maps - 1, 1, m_odd)


def _mla_attn_kernel(*refs, n_src):
    qt_ref = refs[0]
    k_refs = refs[1:1 + n_src]
    vt_refs = refs[1 + n_src:1 + 2 * n_src]
    o_ref, k_all, vt_all, s_scr, o_scr = refs[1 + 2 * n_src:]
    _gather_sources(k_refs, vt_refs, k_all, vt_all)
    _softmax_maps(
        MLA_HEADS,
        lambda n: qt_ref[0, pl.ds(pl.multiple_of(n * LANES, LANES), LANES), :],
        lambda n, off, kc: k_all[n, off:off + kc, :],
        lambda n, off, kc: vt_all[pl.ds(pl.multiple_of(n * MLA_V, MLA_V), MLA_V), off:off + kc],
        s_scr, o_scr)
    for j in range(MLA_OUT // LANES):
        o_ref[0, :, LANES * j:LANES * (j + 1)] = o_scr[LANES * j:LANES * (j + 1), :].T.astype(bf16)


def _diff_attn_kernel(*refs, n_src, lam_init):
    lq1, lk1, lq2, lk2, g_ref, qt_ref = refs[:6]
    k_refs = refs[6:6 + n_src]
    vt_refs = refs[6 + n_src:6 + 2 * n_src]
    o_ref, k_all, vt_all, s_scr, o_scr = refs[6 + 2 * n_src:]
    _gather_sources(k_refs, vt_refs, k_all, vt_all)
    tq = qt_ref.shape[2]
    row = lax.broadcasted_iota(jnp.int32, (LANES, tq), 0)
    per_tile = LANES // DF_DIM
    tile_of = lambda n: lax.shift_right_logical(n, per_tile.bit_length() - 1)

    def qt_of(n):
        lo = DF_DIM * (n & (per_tile - 1))
        qt = qt_ref[0, pl.ds(pl.multiple_of(tile_of(n) * LANES, LANES), LANES), :]
        return jnp.where(row < lo, 0.0, jnp.where(row < lo + DF_DIM, qt, 0.0)).astype(bf16)

    def vt_of(n, off, kc):
        h = lax.shift_right_logical(n, 1)
        return vt_all[pl.ds(pl.multiple_of(h * DF_V, DF_V), DF_V), off:off + kc]

    _softmax_maps(2 * DF_HEADS, qt_of, lambda n, off, kc: k_all[tile_of(n), off:off + kc, :], vt_of, s_scr, o_scr)

    lam = (jnp.exp(jnp.sum(lq1[...] * lk1[...], axis=-1, keepdims=True))
           - jnp.exp(jnp.sum(lq2[...] * lk2[...], axis=-1, keepdims=True)) + lam_init)
    g = jnp.concatenate([g_ref[...]] * (tq // LANES), axis=-1)
    for t in range(DF_OUT // LANES):
        pair = []
        for h in (2 * t, 2 * t + 1):
            o = o_scr[2 * h * DF_V:(2 * h + 1) * DF_V, :] - lam * o_scr[(2 * h + 1) * DF_V:(2 * h + 2) * DF_V, :]
            ms = jnp.mean(o * o, axis=0, keepdims=True)
            pair.append(o * lax.rsqrt(ms + EPS) * g * (1.0 - lam_init))
        o_ref[0, :, LANES * t:LANES * (t + 1)] = jnp.concatenate(pair, axis=0).T.astype(bf16)


def _attn_specs(qt, ks, vs, tq, n_maps, dv):
    b, _, lq = qt.shape
    spec_q = pl.BlockSpec((1, qt.shape[1], tq), lambda i, j: (i, 0, j))
    full = lambda a: pl.BlockSpec((1,) + a.shape[1:], lambda i, j: (i, 0, 0))
    lk = sum(k.shape[1] for k in ks)
    scratch = [pltpu.VMEM((ks[0].shape[2] // LANES, lk, LANES), bf16), pltpu.VMEM((vs[0].shape[1], lk), bf16),
               pltpu.VMEM((2, lk, tq), f32), pltpu.VMEM((n_maps * dv, tq), f32)]
    return (b, lq // tq), spec_q, [full(a) for a in ks] + [full(a) for a in vs], scratch


def _mla_attn(q, ks, vs):
    b, _, lq = q.shape
    tq = min(256, lq)
    grid, spec_q, kv_specs, scratch = _attn_specs(q, ks, vs, tq, MLA_HEADS, MLA_V)
    return pl.pallas_call(
        functools.partial(_mla_attn_kernel, n_src=len(ks)),
        grid=grid,
        in_specs=[spec_q] + kv_specs,
        out_specs=pl.BlockSpec((1, tq, MLA_OUT), lambda i, j: (i, j, 0)),
        out_shape=jax.ShapeDtypeStruct((b, lq, MLA_OUT), bf16),
        scratch_shapes=scratch,
        compiler_params=_cparams("parallel", "arbitrary"),
        name="mla_attn",
    )(q, *ks, *vs)


def _diff_attn(lams, g, q, ks, vs, lam_init):
    b, _, lq = q.shape
    tq = min(256, lq)
    grid, spec_q, kv_specs, scratch = _attn_specs(q, ks, vs, tq, 2 * DF_HEADS, DF_V)
    small = [_const_spec(a.shape) for a in lams] + [_const_spec(g.shape)]
    return pl.pallas_call(
        functools.partial(_diff_attn_kernel, n_src=len(ks), lam_init=lam_init),
        grid=grid,
        in_specs=small + [spec_q] + kv_specs,
        out_specs=pl.BlockSpec((1, tq, DF_OUT), lambda i, j: (i, j, 0)),
        out_shape=jax.ShapeDtypeStruct((b, lq, DF_OUT), bf16),
        scratch_shapes=scratch,
        compiler_params=_cparams("parallel", "arbitrary"),
        name="diff_attn",
    )(*lams, g, q, *ks, *vs)


def _hy_filter_kernel(z_ref, w1_ref, b1_ref, fr_ref, w2_ref, b2_ref, w3_ref, b3_ref, win_ref,
                      fc_ref, fs_ref, a_ref, a2_ref, bq_ref, h_scr):
    kt = pl.program_id(0)
    hp = lax.Precision.HIGHEST

    @pl.when(kt == 0)
    def _():
        fr = fr_ref[...]
        a = jnp.sin(fr * (jnp.dot(z_ref[...], w1_ref[...], precision=hp, preferred_element_type=f32)
                          + b1_ref[...]))
        a = jnp.sin(fr * (jnp.dot(a, w2_ref[...], precision=hp, preferred_element_type=f32) + b2_ref[...]))
        h = jnp.dot(a, w3_ref[...], precision=hp, preferred_element_type=f32) + b3_ref[...]
        win = win_ref[...]
        row = lax.broadcasted_iota(jnp.int32, win.shape, 0)
        h_scr[:, :HY_WIDTH] = (h[:, :HY_WIDTH] * win).astype(bf16)
        h_scr[:, HY_WIDTH:] = jnp.where(row == 0, 0.0, h[:, HY_WIDTH:] * win).astype(bf16)

    hh = h_scr[...]
    c = _dot(fc_ref[...], hh)
    s = _dot(fs_ref[...], hh)
    tk = c.shape[0]
    row = lax.broadcasted_iota(jnp.int32, (tk, HY_WIDTH), 0) + kt * tk
    first = row == 0
    a = c[:, :HY_WIDTH] + c[:, HY_WIDTH:]
    second = jnp.where(first, s[:, :HY_WIDTH] + s[:, HY_WIDTH:], s[:, :HY_WIDTH] - s[:, HY_WIDTH:])
    a_ref[...] = a
    a2_ref[...] = jnp.where(first, second, a)
    bq_ref[...] = jnp.where(first, 0.0, second)


def _hy_filter(z, lw, win, fmat):
    l = z.shape[0]
    tk = min(512, l)
    nk = l // tk
    small = [lw['hy_w1'], lw['hy_b1'], lw['hy_freq'], lw['hy_w2'], lw['hy_b2'], lw['hy_w3'], lw['hy_b3']]
    out = jax.ShapeDtypeStruct((l, HY_WIDTH), f32)
    return pl.pallas_call(
        _hy_filter_kernel,
        grid=(nk,),
        in_specs=([_const_spec(z.shape)] + [_const_spec(a.shape) for a in small] + [_const_spec(win.shape)]
                  + [pl.BlockSpec((tk, l), lambda k: (k, 0)), pl.BlockSpec((tk, l), lambda k: (k + nk, 0))]),
        out_specs=[pl.BlockSpec((tk, HY_WIDTH), lambda k: (k, 0))] * 3,
        out_shape=[out] * 3,
        scratch_shapes=[pltpu.VMEM((l, 2 * HY_WIDTH), bf16)],
        compiler_params=_cparams("arbitrary"),
        name="hy_filter",
    )(z, *small, win, fmat, fmat)


def _hy_conv_kernel(p_ref, cw_ref, cb_ref, skip_ref, fc_ref, fs_ref, gc_ref, gs_ref,
                    a_ref, a2_ref, bq_ref, o_ref, u_scr, x0_scr, acc_scr, *, bb):
    kt = pl.program_id(1)
    l = p_ref.shape[1]

    @pl.when(kt == 0)
    def _():
        row = lax.broadcasted_iota(jnp.int32, (l, 3 * HY_WIDTH), 0)
        cw = cw_ref[...]
        for i in range(bb):
            p = p_ref[i].astype(f32)
            prev = jnp.where(row == 0, 0.0, pltpu.roll(p, 1, axis=0))
            nxt = jnp.where(row == l - 1, 0.0, pltpu.roll(p, l - 1, axis=0))
            uc = cw[0:1] * prev + cw[1:2] * p + cw[2:3] * nxt + cb_ref[...]
            x0_scr[i] = uc[:, :HY_WIDTH]
            u_scr[i] = uc[:, 2 * HY_WIDTH:] * uc[:, HY_WIDTH:2 * HY_WIDTH]
            acc_scr[i] = jnp.zeros((l, HY_WIDTH), f32)

    a, a2, bq = a_ref[...], a2_ref[...], bq_ref[...]
    for i in range(bb):
        u = u_scr[i].astype(bf16)
        ur = _dot(fc_ref[...], u)
        ui = _dot(fs_ref[...], u)
        zr = (ur * a - ui * bq).astype(bf16)
        zi = (ur * bq + ui * a2).astype(bf16)
        acc_scr[i] += _dot(gc_ref[...], zr) + _dot(gs_ref[...], zi)

    @pl.when(kt == pl.num_programs(1) - 1)
    def _():
        for i in range(bb):
            o_ref[i] = (x0_scr[i] * (acc_scr[i] + skip_ref[...] * u_scr[i])).astype(bf16)


def _hy_conv(p_hy, lw, spec, fmat, gmat):
    b, l, _ = p_hy.shape
    bb = 2
    tk = min(512, l)
    nk = l // tk
    a, a2, bq = spec
    return pl.pallas_call(
        functools.partial(_hy_conv_kernel, bb=bb),
        grid=(b // bb, nk),
        in_specs=[pl.BlockSpec((bb, l, 3 * HY_WIDTH), lambda i, k: (i, 0, 0)),
                  _const_spec(lw['hy_conv_w'].shape), _const_spec(lw['hy_conv_b'].shape),
                  _const_spec(lw['hy_skip'].shape),
                  pl.BlockSpec((tk, l), lambda i, k: (k, 0)), pl.BlockSpec((tk, l), lambda i, k: (k + nk, 0)),
                  pl.BlockSpec((l, tk), lambda i, k: (0, k)), pl.BlockSpec((l, tk), lambda i, k: (0, k + nk))]
                 + [pl.BlockSpec((tk, HY_WIDTH), lambda i, k: (k, 0))] * 3,
        out_specs=pl.BlockSpec((bb, l, HY_WIDTH), lambda i, k: (i, 0, 0)),
        out_shape=jax.ShapeDtypeStruct((b, l, HY_WIDTH), bf16),
        scratch_shapes=[pltpu.VMEM((bb, l, HY_WIDTH), f32)] * 3,
        compiler_params=_cparams("parallel", "arbitrary"),
        name="hy_conv",
    )(p_hy, lw['hy_conv_w'], lw['hy_conv_b'], lw['hy_skip'], fmat, fmat, gmat, gmat, a, a2, bq)


def _merge_kernel(x_ref, mod_ref, g_ref, ya_ref, yb_ref, yc_ref, wg_ref, wa_ref, wb_ref, wc_ref, wo_ref,
                  o_ref):
    x = x_ref[0]
    md = mod_ref[0]
    h = (_rms(x) * g_ref[...] * (1.0 + md[1:2]) + md[0:1]).astype(bf16)
    d = D_MODEL
    m = (jax.nn.sigmoid(_dot(h, wg_ref[:, 0:d])) * _dot(ya_ref[0], wa_ref[...])
         + jax.nn.sigmoid(_dot(h, wg_ref[:, d:2 * d])) * _dot(yb_ref[0], wb_ref[...])
         + jax.nn.sigmoid(_dot(h, wg_ref[:, 2 * d:3 * d])) * _dot(yc_ref[0], wc_ref[...]))
    o_ref[0] = x + md[2:3] * _dot(m.astype(bf16), wo_ref[...])


def _merge(x, mod, per_batch_mod, g, ya, yb, yc, lw):
    b, lx, d = x.shape
    tm = min(512, lx)
    tok = lambda n: pl.BlockSpec((1, tm, n), lambda i, j: (i, j, 0))
    mod_map = (lambda i, j: (i, 0, 0)) if per_batch_mod else (lambda i, j: (0, 0, 0))
    weights = [lw['w_gate'], lw['w_br_a'], lw['w_br_b'], lw['w_br_c'], lw['w_out']]
    return pl.pallas_call(
        _merge_kernel,
        grid=(b, lx // tm),
        in_specs=[tok(d), pl.BlockSpec((1, 6, d), mod_map), _const_spec(g.shape),
                  tok(MLA_OUT), tok(HY_WIDTH), tok(DF_OUT)] + [_const_spec(w.shape) for w in weights],
        out_specs=tok(d),
        out_shape=jax.ShapeDtypeStruct(x.shape, f32),
        compiler_params=_cparams("parallel", "parallel"),
        name="merge",
    )(x, mod, g, ya, yb, yc, *weights)


def _mlp_kernel(*refs, final):
    x_ref, mod_ref, g_ref, w1_ref, w2_ref = refs[:5]
    o_ref = refs[-1]
    x = x_ref[0]
    md = mod_ref[0]
    h = (_rms(x) * g_ref[...] * (1.0 + md[4:5]) + md[3:4]).astype(bf16)
    a = jnp.maximum(_dot(h, w1_ref[...]), 0.0)
    y = x + md[5:6] * _dot((a * a).astype(bf16), w2_ref[...])
    if final:
        y = _rms(y) * refs[5][...]
    o_ref[0] = y


def _mlp(x, mod, per_batch_mod, g, w1, w2, final_g=None):
    b, lx, d = x.shape
    tm = min(256, lx)
    tok = pl.BlockSpec((1, tm, d), lambda i, j: (i, j, 0))
    mod_map = (lambda i, j: (i, 0, 0)) if per_batch_mod else (lambda i, j: (0, 0, 0))
    final = final_g is not None
    in_specs = [tok, pl.BlockSpec((1, 6, d), mod_map), _const_spec(g.shape),
                _const_spec(w1.shape), _const_spec(w2.shape)]
    args = [x, mod, g, w1, w2]
    if final:
        in_specs.append(_const_spec(final_g.shape))
        args.append(final_g)
    return pl.pallas_call(
        functools.partial(_mlp_kernel, final=final),
        grid=(b, lx // tm),
        in_specs=in_specs,
        out_specs=tok,
        out_shape=jax.ShapeDtypeStruct(x.shape, f32),
        compiler_params=_cparams("parallel", "parallel"),
        name="mlp",
    )(*args)


def _rope_tables(n_tokens):
    rows = n_tokens // GRID_W
    row = jnp.repeat(jnp.arange(rows), GRID_W).astype(f32)
    col = jnp.tile(jnp.arange(GRID_W), rows).astype(f32)
    nf = MLA_ROPE // 4
    inv = ROPE_BASE ** (-jnp.arange(nf, dtype=f32) / nf)
    ang = jnp.concatenate([row[:, None] * inv, col[:, None] * inv], axis=-1)
    cos, sin = jnp.cos(ang), jnp.sin(ang)
    one = jnp.ones((n_tokens, MLA_NOPE), f32)
    zero = jnp.zeros((n_tokens, MLA_NOPE), f32)
    pad = jnp.zeros((n_tokens, LANES - MLA_NOPE - MLA_ROPE), f32)
    cm = jnp.concatenate([one, cos, cos, pad], axis=-1)
    sm = jnp.concatenate([zero, -sin, sin, pad], axis=-1)
    cd = jnp.tile(jnp.concatenate([cos, cos], axis=-1), (1, LANES // DF_DIM))
    sd = jnp.tile(jnp.concatenate([-sin, sin], axis=-1), (1, LANES // DF_DIM))
    return cm, sm, cd, sd


def _hy_positional(l):
    t = jnp.linspace(0.0, 1.0, l, dtype=f32)[:, None]
    w = (2.0 * math.pi / l) * jnp.arange(l, dtype=f32)[:, None]
    bands = jnp.linspace(1e-4, HY_BANDS - 1, HY_BANDS, dtype=f32)[None]
    z = jnp.concatenate([t, jnp.cos(bands * w), -jnp.sin(bands * w)], axis=-1)
    z = jnp.pad(z, ((0, 0), (0, LANES - HY_EMB)))
    deltas = jnp.linspace(math.log(HY_TARGET) / HY_SLOW, math.log(HY_TARGET) / HY_FAST, HY_WIDTH, dtype=f32)
    win = jnp.exp(-t * jnp.abs(deltas)) + HY_SHIFT
    return z, win


def _dft_tables(l):
    n2 = 2 * l
    k = jnp.arange(l, dtype=jnp.int32)[:, None]
    n = jnp.arange(l, dtype=jnp.int32)[None, :]
    theta = ((k * n) % n2).astype(f32) * (2.0 * math.pi / n2)
    c = jnp.cos(theta)
    s = -jnp.sin(theta)
    s = s.at[0].set(jnp.where(n[0] % 2 == 0, 1.0, -1.0))
    fmat = jnp.concatenate([c, s], axis=0)
    wgt = jnp.full((n2,), 2.0 / n2, f32).at[0].set(1.0 / n2).at[l].set(1.0 / n2)
    gmat = fmat.T * wgt[None, :]
    return fmat.astype(bf16), gmat.astype(bf16)


def _prep_weights(w_in, mla_q_norm_g, mla_w_uq, mla_kv_norm_g, mla_w_ukv, hy_conv_w, hy_conv_b, hy_w1, hy_b1,
                  hy_freq, hy_w2, hy_b2, hy_w3, hy_b3, hy_skip, df_subln_g, w_br_a, w_br_b, w_br_c, w_out,
                  w_fc1, w_fc2):
    depth = w_in.shape[0]
    o = [0]
    for n in (MLA_Q_RANK, MLA_KV_RANK, MLA_ROPE, 3 * HY_WIDTH, 2 * DF_HEADS * DF_DIM, 2 * DF_HEADS * DF_DIM,
              DF_OUT, N_BRANCH * D_MODEL):
        o.append(o[-1] + n)
    kr_pad = jnp.zeros((depth, D_MODEL, LANES - MLA_ROPE), f32)
    w_in1 = jnp.concatenate([w_in[..., o[0]:o[3]], kr_pad, w_in[..., o[3]:o[7]]], axis=-1).astype(bf16)
    w_gate = w_in[..., o[7]:o[8]].astype(bf16)

    uq = mla_w_uq.reshape(depth, MLA_Q_RANK, MLA_HEADS, MLA_NOPE + MLA_ROPE)
    nope, rope = uq[..., :MLA_NOPE], uq[..., MLA_NOPE:]
    half = MLA_ROPE // 2
    zn = jnp.zeros_like(nope)
    zp = jnp.zeros(uq.shape[:3] + (LANES - MLA_NOPE - MLA_ROPE,), f32)
    wq = jnp.concatenate([nope, rope, zp], axis=-1).reshape(depth, MLA_Q_RANK, MLA_HEADS * LANES)
    wqs = jnp.concatenate([zn, rope[..., half:], rope[..., :half], zp], axis=-1)
    wqs = wqs.reshape(depth, MLA_Q_RANK, MLA_HEADS * LANES)
    ukv = mla_w_ukv.reshape(depth, MLA_KV_RANK, MLA_HEADS, MLA_NOPE + MLA_V)
    wk = jnp.concatenate([ukv[..., :MLA_NOPE], jnp.zeros_like(ukv[..., MLA_NOPE:])], axis=-1)
    wk = wk.reshape(depth, MLA_KV_RANK, MLA_HEADS * LANES)
    wv = ukv[..., MLA_NOPE:].reshape(depth, MLA_KV_RANK, MLA_OUT)

    i = jnp.arange(LANES)[:, None]
    j = jnp.arange(MLA_HEADS * LANES)[None, :]
    js = j % LANES - MLA_NOPE
    e = ((i < MLA_ROPE) & (js == i)).astype(bf16)
    es = ((i < MLA_ROPE) & (js >= 0) & (js < MLA_ROPE) & (js == (i + half) % MLA_ROPE)).astype(bf16)
    a = jnp.arange(2 * DF_HEADS * DF_DIM)
    dh = DF_DIM // 2
    swap = (a // DF_DIM) * DF_DIM + (a % DF_DIM + dh) % DF_DIM
    p = (a[:, None] == swap[None, :]).astype(bf16)

    pad_rows = lambda w, r: jnp.pad(w, ((0, 0), (0, r - w.shape[1]), (0, 0)))
    pad_cols = lambda w, c: jnp.pad(w, ((0, 0),) * (w.ndim - 1) + ((0, c - w.shape[-1]),))
    layers = []
    for l in range(depth):
        layers.append({
            'w_in1': w_in1[l], 'w_gate': w_gate[l],
            'q_norm_g': mla_q_norm_g[l][None], 'kv_norm_g': mla_kv_norm_g[l][None],
            'wq': wq[l].astype(bf16), 'wqs': wqs[l].astype(bf16), 'wk': wk[l].astype(bf16),
            'wv': wv[l].astype(bf16), 'e': e, 'es': es, 'p': p,
            'hy_conv_w': hy_conv_w[l], 'hy_conv_b': hy_conv_b[l][None], 'hy_skip': hy_skip[l][None],
            'hy_w1': pad_cols(pad_rows(hy_w1, LANES), LANES)[l],
            'hy_b1': pad_cols(hy_b1, LANES)[l][None], 'hy_freq': pad_cols(hy_freq, LANES)[l][None],
            'hy_w2': pad_cols(pad_rows(hy_w2, LANES), LANES)[l],
            'hy_b2': pad_cols(hy_b2, LANES)[l][None],
            'hy_w3': pad_rows(hy_w3, LANES)[l], 'hy_b3': hy_b3[l][None],
            'subln_g': jnp.broadcast_to(df_subln_g[l][:, None], (DF_V, LANES)),
            'w_br_a': w_br_a[l].astype(bf16), 'w_br_b': w_br_b[l].astype(bf16),
            'w_br_c': w_br_c[l].astype(bf16), 'w_out': w_out[l].astype(bf16),
            'w_fc1': w_fc1[l].astype(bf16), 'w_fc2': w_fc2[l].astype(bf16),
        })
    return layers


def _token_mixers(q_parts, ctx_parts, lw, lams, lam_init, hy_lat, hy_ctx, with_ctx):
    qm, km, vm, qd, kd, vd, hy = q_parts
    cqm, ckm, cvm, cqd, ckd, cvd, chy = ctx_parts
    ya = _mla_attn(qm, [ckm, km], [cvm, vm])
    yc = _diff_attn(lams, lw['subln_g'], qd, [ckd, kd], [cvd, vd], lam_init)
    yb = _hy_conv(hy, lw, *hy_lat)
    if not with_ctx:
        return (ya, yb, yc), None
    ya_c = _mla_attn(cqm, [ckm], [cvm])
    yc_c = _diff_attn(lams, lw['subln_g'], cqd, [ckd], [cvd], lam_init)
    yb_c = _hy_conv(chy, lw, *hy_ctx)
    return (ya, yb, yc), (ya_c, yb_c, yc_c)


def kernel(x, c, ctx, c_ctx, norm_mix_g, norm_ffn_g, w_mod, b_mod, w_in, mla_q_norm_g, mla_w_uq, mla_kv_norm_g, mla_w_ukv, hy_conv_w, hy_conv_b, hy_w1, hy_b1, hy_freq, hy_w2, hy_b2, hy_w3, hy_b3, hy_skip, df_lq1, df_lk1, df_lq2, df_lk2, df_subln_g, w_br_a, w_br_b, w_br_c, w_out, w_fc1, w_fc2, final_norm_g):
    b, n_lat, d = x.shape
    n_ctx = ctx.shape[1]
    depth = w_in.shape[0]
    layers = _prep_weights(w_in, mla_q_norm_g, mla_w_uq, mla_kv_norm_g, mla_w_ukv, hy_conv_w, hy_conv_b,
                           hy_w1, hy_b1, hy_freq, hy_w2, hy_b2, hy_w3, hy_b3, hy_skip, df_subln_g,
                           w_br_a, w_br_b, w_br_c, w_out, w_fc1, w_fc2)
    tables = _rope_tables(n_lat)
    z_lat, win_lat = _hy_positional(n_lat)
    z_ctx, win_ctx = _hy_positional(n_ctx)
    f_lat, g_lat = _dft_tables(n_lat)
    f_ctx, g_ctx = _dft_tables(n_ctx)

    rows = -(-(b + 1) // 8) * 8
    cc = jnp.concatenate([c, c_ctx[None], jnp.zeros((rows - b - 1, d), f32)], axis=0)
    mod = _modulation(cc, w_mod, b_mod).reshape(depth, rows, 6, d)

    xc = ctx
    for l in range(depth):
        lw = layers[l]
        with_ctx = l < depth - 1
        lam_init = 0.8 - 0.6 * math.exp(-0.3 * l)
        mod_lat, mod_ctx = mod[l, :b], mod[l, b:b + 1]
        g_mix, g_ffn = norm_mix_g[l][None], norm_ffn_g[l][None]
        lams = [df_lq1[l][None], df_lk1[l][None], df_lq2[l][None], df_lk2[l][None]]

        hy_lat = (_hy_filter(z_lat, lw, win_lat, f_lat), f_lat, g_lat)
        hy_ctx = (_hy_filter(z_ctx, lw, win_ctx, f_ctx), f_ctx, g_ctx) if with_ctx else None

        parts = _in_proj(x, mod_lat, True, g_mix, lw, tables)
        parts_c = _in_proj(xc, mod_ctx, False, g_mix, lw, None)
        ys, ys_c = _token_mixers(parts, parts_c, lw, lams, lam_init, hy_lat, hy_ctx, with_ctx)

        x = _merge(x, mod_lat, True, g_mix, *ys, lw)
        x = _mlp(x, mod_lat, True, g_ffn, lw['w_fc1'], lw['w_fc2'],
                 final_norm_g[None] if l == depth - 1 else None)
        if with_ctx:
            xc = _merge(xc, mod_ctx, False, g_mix, *ys_c, lw)
            xc = _mlp(xc, mod_ctx, False, g_ffn, lw['w_fc1'], lw['w_fc2'])
    return x
```

```python
import functools
import math

import jax
import jax.numpy as jnp
from jax import lax
from jax.experimental import pallas as pl
from jax.experimental.pallas import tpu as pltpu

f32 = jnp.float32
bf16 = jnp.bfloat16

D_MODEL = 1024
GRID_W = 64
EPS = 1e-6
ROPE_BASE = 10000.0

MLA_HEADS = 8
MLA_NOPE = 64
MLA_ROPE = 32
MLA_V = 64
MLA_Q_RANK = 256
MLA_KV_RANK = 128
MLA_OUT = MLA_HEADS * MLA_V
MLA_SCALE = (MLA_NOPE + MLA_ROPE) ** -0.5

HY_WIDTH = 256
HY_EMB = 33
HY_BANDS = (HY_EMB - 1) // 2
HY_HIDDEN = 64
HY_TARGET = 1e-2
HY_FAST = 0.3
HY_SLOW = 1.5
HY_SHIFT = 0.05

DF_HEADS = 4
DF_DIM = 32
DF_V = 2 * DF_DIM
DF_OUT = DF_HEADS * DF_V
DF_SCALE = DF_DIM ** -0.5

N_BRANCH = 3
D_FF = 4 * D_MODEL

LANES = 128
BF16_ROWS = 16
KEY_CHUNK = 256
TQ = 256
MAPS_PER_TILE = 8
LOG2E = 1.4426950408889634
VMEM_LIMIT = 56 * 1024 * 1024

C_Q, C_KV, C_KR, C_HY, C_DQ, C_DK, C_DV, C_END = 0, 256, 384, 512, 1280, 1536, 1792, 2048


def _cparams(*sem):
    return pltpu.CompilerParams(dimension_semantics=sem, vmem_limit_bytes=VMEM_LIMIT)


def _dot(a, b):
    return jnp.dot(a, b, preferred_element_type=f32)


def _dot_nt(a, b):
    return lax.dot_general(a, b, (((1,), (1,)), ((), ())), preferred_element_type=f32)


def _rms(x):
    return x * lax.rsqrt(jnp.mean(x * x, axis=-1, keepdims=True) + EPS)


def _shr(n, k):
    return n >> k if isinstance(n, int) else lax.shift_right_logical(n, k)


def _aligned(x, m):
    return x if isinstance(x, int) else pl.multiple_of(x, m)


def _const_spec(shape):
    nd = len(shape)
    return pl.BlockSpec(shape, lambda *_: (0,) * nd)


def _mod_kernel(c_ref, w_ref, b_ref, o_ref):
    c = c_ref[...]
    s = c * jax.nn.sigmoid(c)
    o_ref[0] = _dot(s.astype(bf16), w_ref[0]) + b_ref[0]


def _modulation(cc, w_mod, b_mod):
    depth, d, n = w_mod.shape
    r = cc.shape[0]
    tn = 2048
    return pl.pallas_call(
        _mod_kernel,
        grid=(depth, n // tn),
        in_specs=[pl.BlockSpec((r, d), lambda l, j: (0, 0)),
                  pl.BlockSpec((1, d, tn), lambda l, j: (l, 0, j)),
                  pl.BlockSpec((1, 1, tn), lambda l, j: (l, 0, j))],
        out_specs=pl.BlockSpec((1, r, tn), lambda l, j: (l, 0, j)),
        out_shape=jax.ShapeDtypeStruct((depth, r, n), f32),
        compiler_params=_cparams("parallel", "parallel"),
        name="modulation",
    )(cc, w_mod.astype(bf16), b_mod.reshape(depth, 1, n))


def _in_proj_kernel(*refs, rope):
    (x_ref, mod_ref, g_ref, w_ref, qg_ref, kvg_ref, wq_ref, wqs_ref, wk_ref, wv_ref,
     e_ref, es_ref, p_ref) = refs[:13]
    if rope:
        cm_ref, sm_ref, cd_ref, sd_ref = refs[13:17]
        outs = refs[17:]
    else:
        outs = refs[13:]
    qm_ref, km_ref, vm_ref, qd_ref, kd_ref, vd_ref, hy_ref = outs

    x = x_ref[0]
    md = mod_ref[0]
    h = _rms(x) * g_ref[...] * (1.0 + md[1:2]) + md[0:1]
    p = _dot(h.astype(bf16), w_ref[...])

    qn = (_rms(p[:, C_Q:C_KV]) * qg_ref[...]).astype(bf16)
    kvn = (_rms(p[:, C_KV:C_KR]) * kvg_ref[...]).astype(bf16)
    kr = p[:, C_KR:C_HY].astype(bf16)
    q = _dot(qn, wq_ref[...])
    k = _dot(kvn, wk_ref[...]) + _dot(kr, e_ref[...])
    dq = p[:, C_DQ:C_DK]
    dk = p[:, C_DK:C_DV]
    if rope:
        cm = jnp.concatenate([cm_ref[...]] * MLA_HEADS, axis=-1)
        sm = jnp.concatenate([sm_ref[...]] * MLA_HEADS, axis=-1)
        q = q * cm + _dot(qn, wqs_ref[...]) * sm
        k = k * cm + _dot(kr, es_ref[...]) * sm
        cd = jnp.concatenate([cd_ref[...]] * 2, axis=-1)
        sd = jnp.concatenate([sd_ref[...]] * 2, axis=-1)
        dq = dq * cd + _dot(dq.astype(bf16), p_ref[...]) * sd
        dk = dk * cd + _dot(dk.astype(bf16), p_ref[...]) * sd
    _store_query_tiles(qm_ref, (q * (MLA_SCALE * LOG2E)).T.astype(bf16))
    km_ref[0] = k.astype(bf16)
    vm_ref[0] = _dot(kvn, wv_ref[...]).T.astype(bf16)
    _store_query_tiles(qd_ref, (dq * (DF_SCALE * LOG2E)).T.astype(bf16))
    kd_ref[0] = dk.astype(bf16)
    vd_ref[0] = p[:, C_DV:C_END].T.astype(bf16)
    hy_ref[0] = p[:, C_HY:C_DQ].astype(bf16)


def _store_query_tiles(ref, qt):
    for t in range(ref.shape[1]):
        ref[0, t] = qt[:, TQ * t:TQ * (t + 1)]


def _in_proj(x, mod, per_batch_mod, g, lw, tables):
    b, lx, d = x.shape
    tm = min(512, lx)
    rope = tables is not None
    tok = lambda n: pl.BlockSpec((1, tm, n), lambda i, j: (j, i, 0))
    mod_map = (lambda i, j: (j, 0, 0)) if per_batch_mod else (lambda i, j: (0, 0, 0))
    weights = [lw['w_in1'], lw['q_norm_g'], lw['kv_norm_g'], lw['wq'], lw['wqs'], lw['wk'], lw['wv'],
               lw['e'], lw['es'], lw['p']]
    in_specs = ([tok(d), pl.BlockSpec((1, 6, d), mod_map), _const_spec(g.shape)]
                + [_const_spec(w.shape) for w in weights])
    args = [x, mod, g] + weights
    if rope:
        in_specs += [pl.BlockSpec((tm, LANES), lambda i, j: (i, 0))] * 4
        args += list(tables)
    outs = ((MLA_HEADS * LANES, 'q'), (MLA_HEADS * LANES, 't'), (MLA_OUT, 'f'),
            (2 * DF_HEADS * DF_DIM, 'q'), (2 * DF_HEADS * DF_DIM, 't'), (DF_OUT, 'f'),
            (3 * HY_WIDTH, 't'))
    spec = {'t': tok,
            'f': lambda n: pl.BlockSpec((1, n, tm), lambda i, j: (j, 0, i)),
            'q': lambda n: pl.BlockSpec((1, tm // TQ, n, TQ), lambda i, j: (j, i, 0, 0))}
    shape = {'t': lambda n: (b, lx, n), 'f': lambda n: (b, n, lx), 'q': lambda n: (b, lx // TQ, n, TQ)}
    return pl.pallas_call(
        functools.partial(_in_proj_kernel, rope=rope),
        grid=(lx // tm, b),
        in_specs=in_specs,
        out_specs=[spec[lay](n) for n, lay in outs],
        out_shape=[jax.ShapeDtypeStruct(shape[lay](n), bf16) for n, lay in outs],
        compiler_params=_cparams("parallel", "parallel"),
        name="in_proj",
    )(*args)


def _gather_sources(k_refs, vt_refs, k_all, vt_all):
    @pl.when(pl.program_id(1) == 0)
    def _():
        o = 0
        for k, vt in zip(k_refs, vt_refs):
            n = k.shape[1]
            for t in range(k_all.shape[0]):
                k_all[t, o:o + n, :] = k[0, :, LANES * t:LANES * (t + 1)]
            vt_all[:, o:o + n] = vt[0]
            o += n


def _softmax_maps(n_maps, qt_of, k_of, vt_of, s_scr, o_scr):
    _, lk, tq = s_scr.shape
    dv = o_scr.shape[0] // n_maps
    kc = KEY_CHUNK if lk % KEY_CHUNK == 0 else KEY_CHUNK // 2
    assert lk % kc == 0
    ones = jnp.ones((BF16_ROWS, kc), bf16)

    def scores(n, slot):
        qt = qt_of(n)
        pm = None
        for off in range(0, lk, kc):
            s = _dot(k_of(n, off, kc), qt)
            s_scr[slot, off:off + kc, :] = s
            cm = jnp.max(s.reshape(kc // 8, 8, tq), axis=0)
            pm = cm if pm is None else jnp.maximum(pm, cm)
        return jnp.max(pm, axis=0, keepdims=True)

    def values(n, slot, m):
        acc = None
        for off in range(0, lk, kc):
            p = jnp.exp2(s_scr[slot, off:off + kc, :] - m).astype(bf16)
            part = _dot(jnp.concatenate([vt_of(n, off, kc), ones], axis=0), p)
            acc = part if acc is None else acc + part
        o_scr[pl.ds(_aligned(n * dv, dv), dv), :] = acc[:dv] / acc[dv:dv + 1]

    def group(base, m, count):
        for e in range(count):
            m_next = scores(base + e + 1, (e + 1) & 1)
            values(base + e, e & 1, m)
            m = m_next
        return m

    per_iter = 4 if n_maps >= 16 else 2
    trips = (n_maps - 1) // per_iter
    m = lax.fori_loop(0, trips, lambda i, m: group(i * per_iter, m, per_iter), scores(0, 0))
    m = group(trips * per_iter, m, n_maps - 1 - trips * per_iter)
    values(n_maps - 1, (n_maps - 1) & 1, m)


def _mla_attn_kernel(*refs, n_src):
    qt_ref = refs[0]
    k_refs = refs[1:1 + n_src]
    vt_refs = refs[1 + n_src:1 + 2 * n_src]
    o_ref, k_all, vt_all, s_scr, o_scr = refs[1 + 2 * n_src:]
    _gather_sources(k_refs, vt_refs, k_all, vt_all)
    n_tiles = qt_ref.shape[1]
    tile_of = lambda n: _shr(n, 3)
    head_of = lambda n: n & (MAPS_PER_TILE - 1)
    _softmax_maps(
        n_tiles * MLA_HEADS,
        lambda n: qt_ref[0, tile_of(n), pl.ds(_aligned(head_of(n) * LANES, LANES), LANES), :],
        lambda n, off, kc: k_all[head_of(n), off:off + kc, :],
        lambda n, off, kc: vt_all[pl.ds(_aligned(head_of(n) * MLA_V, MLA_V), MLA_V), off:off + kc],
        s_scr, o_scr)
    for t in range(n_tiles):
        for j in range(MLA_OUT // LANES):
            r = t * MLA_OUT + LANES * j
            o_ref[0, TQ * t:TQ * (t + 1), LANES * j:LANES * (j + 1)] = o_scr[r:r + LANES, :].T.astype(bf16)


def _diff_attn_kernel(*refs, n_src, lam_init):
    lq1, lk1, lq2, lk2, g_ref, qt_ref = refs[:6]
    k_refs = refs[6:6 + n_src]
    vt_refs = refs[6 + n_src:6 + 2 * n_src]
    o_ref, k_all, vt_all, s_scr, o_scr = refs[6 + 2 * n_src:]
    _gather_sources(k_refs, vt_refs, k_all, vt_all)
    n_tiles = qt_ref.shape[1]
    row = lax.broadcasted_iota(jnp.int32, (LANES, TQ), 0)
    per_lanes = LANES // DF_DIM
    tile_of = lambda n: _shr(n, 3)
    map_of = lambda n: n & (MAPS_PER_TILE - 1)
    lanes_of = lambda n: _shr(map_of(n), 2)

    def qt_of(n):
        lo = DF_DIM * (n & (per_lanes - 1))
        qt = qt_ref[0, tile_of(n), pl.ds(_aligned(lanes_of(n) * LANES, LANES), LANES), :]
        return jnp.where(row < lo, 0.0, jnp.where(row < lo + DF_DIM, qt, 0.0)).astype(bf16)

    def vt_of(n, off, kc):
        h = _shr(map_of(n), 1)
        return vt_all[pl.ds(_aligned(h * DF_V, DF_V), DF_V), off:off + kc]

    _softmax_maps(n_tiles * 2 * DF_HEADS, qt_of, lambda n, off, kc: k_all[lanes_of(n), off:off + kc, :], vt_of,
                  s_scr, o_scr)

    lam = (jnp.exp(jnp.sum(lq1[...] * lk1[...], axis=-1, keepdims=True))
           - jnp.exp(jnp.sum(lq2[...] * lk2[...], axis=-1, keepdims=True)) + lam_init)
    g = jnp.concatenate([g_ref[...]] * (TQ // LANES), axis=-1)
    for t in range(n_tiles):
        for j in range(DF_OUT // LANES):
            pair = []
            for h in (2 * j, 2 * j + 1):
                r = (t * MAPS_PER_TILE + 2 * h) * DF_V
                o = o_scr[r:r + DF_V, :] - lam * o_scr[r + DF_V:r + 2 * DF_V, :]
                ms = jnp.mean(o * o, axis=0, keepdims=True)
                pair.append(o * lax.rsqrt(ms + EPS) * g * (1.0 - lam_init))
            o_ref[0, TQ * t:TQ * (t + 1), LANES * j:LANES * (j + 1)] = (
                jnp.concatenate(pair, axis=0).T.astype(bf16))


def _attn_call(kernel, name, small, qt, ks, vs, dv):
    b, n_qt, f, _ = qt.shape
    tiles = 2 if n_qt % 2 == 0 else 1
    lk = sum(k.shape[1] for k in ks)
    f_out = vs[0].shape[1]
    full = lambda a: pl.BlockSpec((1,) + a.shape[1:], lambda i, j: (i, 0, 0))
    scratch = [pltpu.VMEM((f // LANES, lk, LANES), bf16), pltpu.VMEM((f_out, lk), bf16),
               pltpu.VMEM((2, lk, TQ), f32), pltpu.VMEM((tiles * MAPS_PER_TILE * dv, TQ), f32)]
    return pl.pallas_call(
        kernel,
        grid=(b, n_qt // tiles),
        in_specs=([_const_spec(a.shape) for a in small]
                  + [pl.BlockSpec((1, tiles, f, TQ), lambda i, j: (i, j, 0, 0))]
                  + [full(a) for a in ks] + [full(a) for a in vs]),
        out_specs=pl.BlockSpec((1, tiles * TQ, f_out), lambda i, j: (i, j, 0)),
        out_shape=jax.ShapeDtypeStruct((b, n_qt * TQ, f_out), bf16),
        scratch_shapes=scratch,
        compiler_params=_cparams("parallel", "arbitrary"),
        name=name,
    )(*small, qt, *ks, *vs)


def _mla_attn(q, ks, vs):
    return _attn_call(functools.partial(_mla_attn_kernel, n_src=len(ks)), "mla_attn", [], q, ks, vs, MLA_V)


def _diff_attn(lams, g, q, ks, vs, lam_init):
    return _attn_call(functools.partial(_diff_attn_kernel, n_src=len(ks), lam_init=lam_init), "diff_attn",
                      list(lams) + [g], q, ks, vs, DF_V)


def _hy_filter_kernel(z_ref, w1_ref, b1_ref, fr_ref, w2_ref, b2_ref, w3_ref, b3_ref, win_ref,
                      fc_ref, fs_ref, a_ref, a2_ref, bq_ref, h_scr):
    kt = pl.program_id(0)
    hp = lax.Precision.HIGHEST

    @pl.when(kt == 0)
    def _():
        fr = fr_ref[...]
        a = jnp.sin(fr * (jnp.dot(z_ref[...], w1_ref[...], precision=hp, preferred_element_type=f32)
                          + b1_ref[...]))
        a = jnp.sin(fr * (jnp.dot(a, w2_ref[...], precision=hp, preferred_element_type=f32) + b2_ref[...]))
        h = jnp.dot(a, w3_ref[...], precision=hp, preferred_element_type=f32) + b3_ref[...]
        win = win_ref[...]
        row = lax.broadcasted_iota(jnp.int32, win.shape, 0)
        h_scr[:, :HY_WIDTH] = (h[:, :HY_WIDTH] * win).astype(bf16)
        h_scr[:, HY_WIDTH:] = jnp.where(row == 0, 0.0, h[:, HY_WIDTH:] * win).astype(bf16)

    hh = h_scr[...]
    c = _dot(fc_ref[...], hh)
    s = _dot(fs_ref[...], hh)
    tk = c.shape[0]
    row = lax.broadcasted_iota(jnp.int32, (tk, HY_WIDTH), 0) + kt * tk
    first = row == 0
    a = c[:, :HY_WIDTH] + c[:, HY_WIDTH:]
    second = jnp.where(first, s[:, :HY_WIDTH] + s[:, HY_WIDTH:], s[:, :HY_WIDTH] - s[:, HY_WIDTH:])
    a_ref[...] = a
    a2_ref[...] = jnp.where(first, second, a)
    bq_ref[...] = jnp.where(first, 0.0, second)


def _hy_filter(z, lw, win, fmat):
    l = z.shape[0]
    tk = min(512, l)
    nk = l // tk
    small = [lw['hy_w1'], lw['hy_b1'], lw['hy_freq'], lw['hy_w2'], lw['hy_b2'], lw['hy_w3'], lw['hy_b3']]
    out = jax.ShapeDtypeStruct((l, HY_WIDTH), f32)
    return pl.pallas_call(
        _hy_filter_kernel,
        grid=(nk,),
        in_specs=([_const_spec(z.shape)] + [_const_spec(a.shape) for a in small] + [_const_spec(win.shape)]
                  + [pl.BlockSpec((tk, l), lambda k: (k, 0)), pl.BlockSpec((tk, l), lambda k: (k + nk, 0))]),
        out_specs=[pl.BlockSpec((tk, HY_WIDTH), lambda k: (k, 0))] * 3,
        out_shape=[out] * 3,
        scratch_shapes=[pltpu.VMEM((l, 2 * HY_WIDTH), bf16)],
        compiler_params=_cparams("arbitrary"),
        name="hy_filter",
    )(z, *small, win, fmat, fmat)


def _hy_conv_kernel(p_ref, cw_ref, cb_ref, skip_ref, fc_ref, fs_ref, gc_ref, gs_ref,
                    a_ref, a2_ref, bq_ref, o_ref, u_scr, ub_scr, x0_scr, acc_scr, *, bb):
    kt = pl.program_id(1)
    l = p_ref.shape[1]

    @pl.when(kt == 0)
    def _():
        row = lax.broadcasted_iota(jnp.int32, (l, 3 * HY_WIDTH), 0)
        cw = cw_ref[...]
        for i in range(bb):
            p = p_ref[i].astype(f32)
            prev = jnp.where(row == 0, 0.0, pltpu.roll(p, 1, axis=0))
            nxt = jnp.where(row == l - 1, 0.0, pltpu.roll(p, l - 1, axis=0))
            uc = cw[0:1] * prev + cw[1:2] * p + cw[2:3] * nxt + cb_ref[...]
            x0_scr[i] = uc[:, :HY_WIDTH]
            u = uc[:, 2 * HY_WIDTH:] * uc[:, HY_WIDTH:2 * HY_WIDTH]
            u_scr[i] = u
            ub_scr[i] = u.astype(bf16)
            acc_scr[i] = jnp.zeros((l, HY_WIDTH), f32)

    a, a2, bq = a_ref[...], a2_ref[...], bq_ref[...]
    for i in range(bb):
        u = ub_scr[i]
        ur = _dot(fc_ref[...], u)
        ui = _dot(fs_ref[...], u)
        zr = (ur * a - ui * bq).astype(bf16)
        zi = (ur * bq + ui * a2).astype(bf16)
        acc_scr[i] += _dot(gc_ref[...], zr) + _dot(gs_ref[...], zi)

    @pl.when(kt == pl.num_programs(1) - 1)
    def _():
        for i in range(bb):
            o_ref[i] = (x0_scr[i] * (acc_scr[i] + skip_ref[...] * u_scr[i])).astype(bf16)


def _hy_conv(p_hy, lw, spec, fmat, gmat):
    b, l, _ = p_hy.shape
    bb = 2
    tk = min(512, l)
    nk = l // tk
    a, a2, bq = spec
    return pl.pallas_call(
        functools.partial(_hy_conv_kernel, bb=bb),
        grid=(b // bb, nk),
        in_specs=[pl.BlockSpec((bb, l, 3 * HY_WIDTH), lambda i, k: (i, 0, 0)),
                  _const_spec(lw['hy_conv_w'].shape), _const_spec(lw['hy_conv_b'].shape),
                  _const_spec(lw['hy_skip'].shape),
                  pl.BlockSpec((tk, l), lambda i, k: (k, 0)), pl.BlockSpec((tk, l), lambda i, k: (k + nk, 0)),
                  pl.BlockSpec((l, tk), lambda i, k: (0, k)), pl.BlockSpec((l, tk), lambda i, k: (0, k + nk))]
                 + [pl.BlockSpec((tk, HY_WIDTH), lambda i, k: (k, 0))] * 3,
        out_specs=pl.BlockSpec((bb, l, HY_WIDTH), lambda i, k: (i, 0, 0)),
        out_shape=jax.ShapeDtypeStruct((b, l, HY_WIDTH), bf16),
        scratch_shapes=[pltpu.VMEM((bb, l, HY_WIDTH), f32), pltpu.VMEM((bb, l, HY_WIDTH), bf16),
                        pltpu.VMEM((bb, l, HY_WIDTH), f32), pltpu.VMEM((bb, l, HY_WIDTH), f32)],
        compiler_params=_cparams("parallel", "arbitrary"),
        name="hy_conv",
    )(p_hy, lw['hy_conv_w'], lw['hy_conv_b'], lw['hy_skip'], fmat, fmat, gmat, gmat, a, a2, bq)


def _merge_kernel(x_ref, mod_ref, g_ref, ya_ref, yb_ref, yc_ref, wg_ref, wa_ref, wb_ref, wc_ref, wo_ref,
                  o_ref):
    x = x_ref[0]
    md = mod_ref[0]
    h = (_rms(x) * g_ref[...] * (1.0 + md[1:2]) + md[0:1]).astype(bf16)
    d = D_MODEL
    m = (jax.nn.sigmoid(_dot(h, wg_ref[:, 0:d])) * _dot(ya_ref[0], wa_ref[...])
         + jax.nn.sigmoid(_dot(h, wg_ref[:, d:2 * d])) * _dot(yb_ref[0], wb_ref[...])
         + jax.nn.sigmoid(_dot(h, wg_ref[:, 2 * d:3 * d])) * _dot(yc_ref[0], wc_ref[...]))
    o_ref[0] = x + md[2:3] * _dot(m.astype(bf16), wo_ref[...])


def _merge(x, mod, per_batch_mod, g, ya, yb, yc, lw):
    b, lx, d = x.shape
    tm = min(512, lx)
    tok = lambda n: pl.BlockSpec((1, tm, n), lambda i, j: (i, j, 0))
    mod_map = (lambda i, j: (i, 0, 0)) if per_batch_mod else (lambda i, j: (0, 0, 0))
    weights = [lw['w_gate'], lw['w_br_a'], lw['w_br_b'], lw['w_br_c'], lw['w_out']]
    return pl.pallas_call(
        _merge_kernel,
        grid=(b, lx // tm),
        in_specs=[tok(d), pl.BlockSpec((1, 6, d), mod_map), _const_spec(g.shape),
                  tok(MLA_OUT), tok(HY_WIDTH), tok(DF_OUT)] + [_const_spec(w.shape) for w in weights],
        out_specs=tok(d),
        out_shape=jax.ShapeDtypeStruct(x.shape, f32),
        compiler_params=_cparams("parallel", "parallel"),
        name="merge",
    )(x, mod, g, ya, yb, yc, *weights)


def _mlp_kernel(*refs, final):
    x_ref, mod_ref, g_ref, w1_ref, w2_ref = refs[:5]
    o_ref = refs[-1]
    x = x_ref[0]
    md = mod_ref[0]
    h = (_rms(x) * g_ref[...] * (1.0 + md[4:5]) + md[3:4]).astype(bf16)
    a = jnp.maximum(_dot(h, w1_ref[...]), 0.0)
    y = x + md[5:6] * _dot((a * a).astype(bf16), w2_ref[...])
    if final:
        y = _rms(y) * refs[5][...]
    o_ref[0] = y


def _mlp(x, mod, per_batch_mod, g, w1, w2, final_g=None):
    b, lx, d = x.shape
    tm = min(256, lx)
    tok = pl.BlockSpec((1, tm, d), lambda i, j: (i, j, 0))
    mod_map = (lambda i, j: (i, 0, 0)) if per_batch_mod else (lambda i, j: (0, 0, 0))
    final = final_g is not None
    in_specs = [tok, pl.BlockSpec((1, 6, d), mod_map), _const_spec(g.shape),
                _const_spec(w1.shape), _const_spec(w2.shape)]
    args = [x, mod, g, w1, w2]
    if final:
        in_specs.append(_const_spec(final_g.shape))
        args.append(final_g)
    return pl.pallas_call(
        functools.partial(_mlp_kernel, final=final),
        grid=(b, lx // tm),
        in_specs=in_specs,
        out_specs=tok,
        out_shape=jax.ShapeDtypeStruct(x.shape, f32),
        compiler_params=_cparams("parallel", "parallel"),
        name="mlp",
    )(*args)


def _rope_tables(n_tokens):
    rows = n_tokens // GRID_W
    row = jnp.repeat(jnp.arange(rows), GRID_W).astype(f32)
    col = jnp.tile(jnp.arange(GRID_W), rows).astype(f32)
    nf = MLA_ROPE // 4
    inv = ROPE_BASE ** (-jnp.arange(nf, dtype=f32) / nf)
    ang = jnp.concatenate([row[:, None] * inv, col[:, None] * inv], axis=-1)
    cos, sin = jnp.cos(ang), jnp.sin(ang)
    one = jnp.ones((n_tokens, MLA_NOPE), f32)
    zero = jnp.zeros((n_tokens, MLA_NOPE), f32)
    pad = jnp.zeros((n_tokens, LANES - MLA_NOPE - MLA_ROPE), f32)
    cm = jnp.concatenate([one, cos, cos, pad], axis=-1)
    sm = jnp.concatenate([zero, -sin, sin, pad], axis=-1)
    cd = jnp.tile(jnp.concatenate([cos, cos], axis=-1), (1, LANES // DF_DIM))
    sd = jnp.tile(jnp.concatenate([-sin, sin], axis=-1), (1, LANES // DF_DIM))
    return cm, sm, cd, sd


def _hy_positional(l):
    t = jnp.linspace(0.0, 1.0, l, dtype=f32)[:, None]
    w = (2.0 * math.pi / l) * jnp.arange(l, dtype=f32)[:, None]
    bands = jnp.linspace(1e-4, HY_BANDS - 1, HY_BANDS, dtype=f32)[None]
    z = jnp.concatenate([t, jnp.cos(bands * w), -jnp.sin(bands * w)], axis=-1)
    z = jnp.pad(z, ((0, 0), (0, LANES - HY_EMB)))
    deltas = jnp.linspace(math.log(HY_TARGET) / HY_SLOW, math.log(HY_TARGET) / HY_FAST, HY_WIDTH, dtype=f32)
    win = jnp.exp(-t * jnp.abs(deltas)) + HY_SHIFT
    return z, win


def _dft_tables(l):
    n2 = 2 * l
    n = jnp.arange(l, dtype=jnp.int32)[None, :]
    a = jnp.arange(l // GRID_W, dtype=jnp.int32)[:, None]
    b = jnp.arange(GRID_W, dtype=jnp.int32)[:, None]
    ang_a = ((GRID_W * a * n) % n2).astype(f32) * (2.0 * math.pi / n2)
    ang_b = ((b * n) % n2).astype(f32) * (2.0 * math.pi / n2)
    ca, sa, cb, sb = jnp.cos(ang_a), jnp.sin(ang_a), jnp.cos(ang_b), jnp.sin(ang_b)
    nyq = jnp.where(n[0] % 2 == 0, 1.0, -1.0)
    c = (ca[:, None, :] * cb[None] - sa[:, None, :] * sb[None]).reshape(l, l)
    s = -(sa[:, None, :] * cb[None] + ca[:, None, :] * sb[None]).reshape(l, l)
    fmat = jnp.concatenate([c, s.at[0].set(nyq)], axis=0)
    wgt = jnp.full((l,), 2.0 / n2, f32).at[0].set(1.0 / n2)
    cat, sat, cbt, sbt = ca.T, sa.T, cb.T, sb.T
    ct = (cat[:, :, None] * cbt[:, None, :] - sat[:, :, None] * sbt[:, None, :]).reshape(l, l)
    st = -(sat[:, :, None] * cbt[:, None, :] + cat[:, :, None] * sbt[:, None, :]).reshape(l, l)
    gmat = jnp.concatenate([ct * wgt[None], st.at[:, 0].set(nyq) * wgt[None]], axis=1)
    return fmat.astype(bf16), gmat.astype(bf16)


def _prep_weights(w_in, mla_q_norm_g, mla_w_uq, mla_kv_norm_g, mla_w_ukv, hy_conv_w, hy_conv_b, hy_w1, hy_b1,
                  hy_freq, hy_w2, hy_b2, hy_w3, hy_b3, hy_skip, df_subln_g, w_br_a, w_br_b, w_br_c, w_out,
                  w_fc1, w_fc2):
    depth = w_in.shape[0]
    o = [0]
    for n in (MLA_Q_RANK, MLA_KV_RANK, MLA_ROPE, 3 * HY_WIDTH, 2 * DF_HEADS * DF_DIM, 2 * DF_HEADS * DF_DIM,
              DF_OUT, N_BRANCH * D_MODEL):
        o.append(o[-1] + n)
    kr_pad = jnp.zeros((depth, D_MODEL, LANES - MLA_ROPE), f32)
    w_in1 = jnp.concatenate([w_in[..., o[0]:o[3]], kr_pad, w_in[..., o[3]:o[7]]], axis=-1).astype(bf16)
    w_gate = w_in[..., o[7]:o[8]].astype(bf16)

    uq = mla_w_uq.reshape(depth, MLA_Q_RANK, MLA_HEADS, MLA_NOPE + MLA_ROPE)
    nope, rope = uq[..., :MLA_NOPE], uq[..., MLA_NOPE:]
    half = MLA_ROPE // 2
    zn = jnp.zeros_like(nope)
    zp = jnp.zeros(uq.shape[:3] + (LANES - MLA_NOPE - MLA_ROPE,), f32)
    wq = jnp.concatenate([nope, rope, zp], axis=-1).reshape(depth, MLA_Q_RANK, MLA_HEADS * LANES)
    wqs = jnp.concatenate([zn, rope[..., half:], rope[..., :half], zp], axis=-1)
    wqs = wqs.reshape(depth, MLA_Q_RANK, MLA_HEADS * LANES)
    ukv = mla_w_ukv.reshape(depth, MLA_KV_RANK, MLA_HEADS, MLA_NOPE + MLA_V)
    wk = jnp.concatenate([ukv[..., :MLA_NOPE], jnp.zeros_like(ukv[..., MLA_NOPE:])], axis=-1)
    wk = wk.reshape(depth, MLA_KV_RANK, MLA_HEADS * LANES)
    wv = ukv[..., MLA_NOPE:].reshape(depth, MLA_KV_RANK, MLA_OUT)

    i = jnp.arange(LANES)[:, None]
    j = jnp.arange(MLA_HEADS * LANES)[None, :]
    js = j % LANES - MLA_NOPE
    e = ((i < MLA_ROPE) & (js == i)).astype(bf16)
    es = ((i < MLA_ROPE) & (js >= 0) & (js < MLA_ROPE) & (js == (i + half) % MLA_ROPE)).astype(bf16)
    a = jnp.arange(2 * DF_HEADS * DF_DIM)
    dh = DF_DIM // 2
    swap = (a // DF_DIM) * DF_DIM + (a % DF_DIM + dh) % DF_DIM
    p = (a[:, None] == swap[None, :]).astype(bf16)

    pad_rows = lambda w, r: jnp.pad(w, ((0, 0), (0, r - w.shape[1]), (0, 0)))
    pad_cols = lambda w, c: jnp.pad(w, ((0, 0),) * (w.ndim - 1) + ((0, c - w.shape[-1]),))
    layers = []
    for l in range(depth):
        layers.append({
            'w_in1': w_in1[l], 'w_gate': w_gate[l],
            'q_norm_g': mla_q_norm_g[l][None], 'kv_norm_g': mla_kv_norm_g[l][None],
            'wq': wq[l].astype(bf16), 'wqs': wqs[l].astype(bf16), 'wk': wk[l].astype(bf16),
            'wv': wv[l].astype(bf16), 'e': e, 'es': es, 'p': p,
            'hy_conv_w': hy_conv_w[l], 'hy_conv_b': hy_conv_b[l][None], 'hy_skip': hy_skip[l][None],
            'hy_w1': pad_cols(pad_rows(hy_w1, LANES), LANES)[l],
            'hy_b1': pad_cols(hy_b1, LANES)[l][None], 'hy_freq': pad_cols(hy_freq, LANES)[l][None],
            'hy_w2': pad_cols(pad_rows(hy_w2, LANES), LANES)[l],
            'hy_b2': pad_cols(hy_b2, LANES)[l][None],
            'hy_w3': pad_rows(hy_w3, LANES)[l], 'hy_b3': hy_b3[l][None],
            'subln_g': jnp.broadcast_to(df_subln_g[l][:, None], (DF_V, LANES)),
            'w_br_a': w_br_a[l].astype(bf16), 'w_br_b': w_br_b[l].astype(bf16),
            'w_br_c': w_br_c[l].astype(bf16), 'w_out': w_out[l].astype(bf16),
            'w_fc1': w_fc1[l].astype(bf16), 'w_fc2': w_fc2[l].astype(bf16),
        })
    return layers


def _token_mixers(q_parts, ctx_parts, lw, lams, lam_init, hy_lat, hy_ctx, with_ctx):
    qm, km, vm, qd, kd, vd, hy = q_parts
    cqm, ckm, cvm, cqd, ckd, cvd, chy = ctx_parts
    ya = _mla_attn(qm, [ckm, km], [cvm, vm])
    yc = _diff_attn(lams, lw['subln_g'], qd, [ckd, kd], [cvd, vd], lam_init)
    yb = _hy_conv(hy, lw, *hy_lat)
    if not with_ctx:
        return (ya, yb, yc), None
    ya_c = _mla_attn(cqm, [ckm], [cvm])
    yc_c = _diff_attn(lams, lw['subln_g'], cqd, [ckd], [cvd], lam_init)
    yb_c = _hy_conv(chy, lw, *hy_ctx)
    return (ya, yb, yc), (ya_c, yb_c, yc_c)


def kernel(x, c, ctx, c_ctx, norm_mix_g, norm_ffn_g, w_mod, b_mod, w_in, mla_q_norm_g, mla_w_uq, mla_kv_norm_g, mla_w_ukv, hy_conv_w, hy_conv_b, hy_w1, hy_b1, hy_freq, hy_w2, hy_b2, hy_w3, hy_b3, hy_skip, df_lq1, df_lk1, df_lq2, df_lk2, df_subln_g, w_br_a, w_br_b, w_br_c, w_out, w_fc1, w_fc2, final_norm_g):
    b, n_lat, d = x.shape
    n_ctx = ctx.shape[1]
    depth = w_in.shape[0]
    layers = _prep_weights(w_in, mla_q_norm_g, mla_w_uq, mla_kv_norm_g, mla_w_ukv, hy_conv_w, hy_conv_b,
                           hy_w1, hy_b1, hy_freq, hy_w2, hy_b2, hy_w3, hy_b3, hy_skip, df_subln_g,
                           w_br_a, w_br_b, w_br_c, w_out, w_fc1, w_fc2)
    tables = _rope_tables(n_lat)
    z_lat, win_lat = _hy_positional(n_lat)
    z_ctx, win_ctx = _hy_positional(n_ctx)
    f_lat, g_lat = _dft_tables(n_lat)
    f_ctx, g_ctx = _dft_tables(n_ctx)

    rows = -(-(b + 1) // 8) * 8
    cc = jnp.concatenate([c, c_ctx[None], jnp.zeros((rows - b - 1, d), f32)], axis=0)
    mod = _modulation(cc, w_mod, b_mod).reshape(depth, rows, 6, d)

    xc = ctx
    for l in range(depth):
        lw = layers[l]
        with_ctx = l < depth - 1
        lam_init = 0.8 - 0.6 * math.exp(-0.3 * l)
        mod_lat, mod_ctx = mod[l, :b], mod[l, b:b + 1]
        g_mix, g_ffn = norm_mix_g[l][None], norm_ffn_g[l][None]
        lams = [df_lq1[l][None], df_lk1[l][None], df_lq2[l][None], df_lk2[l][None]]

        hy_lat = (_hy_filter(z_lat, lw, win_lat, f_lat), f_lat, g_lat)
        hy_ctx = (_hy_filter(z_ctx, lw, win_ctx, f_ctx), f_ctx, g_ctx) if with_ctx else None

        parts = _in_proj(x, mod_lat, True, g_mix, lw, tables)
        parts_c = _in_proj(xc, mod_ctx, False, g_mix, lw, None)
        ys, ys_c = _token_mixers(parts, parts_c, lw, lams, lam_init, hy_lat, hy_ctx, with_ctx)

        x = _merge(x, mod_lat, True, g_mix, *ys, lw)
        x = _mlp(x, mod_lat, True, g_ffn, lw['w_fc1'], lw['w_fc2'],
                 final_norm_g[None] if l == depth - 1 else None)
        if with_ctx:
            xc = _merge(xc, mod_ctx, False, g_mix, *ys_c, lw)
            xc = _mlp(xc, mod_ctx, False, g_ffn, lw['w_fc1'], lw['w_fc2'])
    return x
```

```python
import functools
import math

import jax
import jax.numpy as jnp
from jax import lax
from jax.experimental import pallas as pl
from jax.experimental.pallas import tpu as pltpu

f32 = jnp.float32
bf16 = jnp.bfloat16

D_MODEL = 1024
GRID_W = 64
EPS = 1e-6
ROPE_BASE = 10000.0

MLA_HEADS = 8
MLA_NOPE = 64
MLA_ROPE = 32
MLA_V = 64
MLA_Q_RANK = 256
MLA_KV_RANK = 128
MLA_OUT = MLA_HEADS * MLA_V
MLA_SCALE = (MLA_NOPE + MLA_ROPE) ** -0.5

HY_WIDTH = 256
HY_EMB = 33
HY_BANDS = (HY_EMB - 1) // 2
HY_HIDDEN = 64
HY_TARGET = 1e-2
HY_FAST = 0.3
HY_SLOW = 1.5
HY_SHIFT = 0.05

DF_HEADS = 4
DF_DIM = 32
DF_V = 2 * DF_DIM
DF_OUT = DF_HEADS * DF_V
DF_SCALE = DF_DIM ** -0.5

N_BRANCH = 3
D_FF = 4 * D_MODEL

LANES = 128
BF16_ROWS = 16
KEY_CHUNK = 256
TQ = 256
MAPS_PER_TILE = 8
LOG2E = 1.4426950408889634
VMEM_LIMIT = 56 * 1024 * 1024

C_Q, C_KV, C_KR, C_HY, C_DQ, C_DK, C_DV, C_END = 0, 256, 384, 512, 1280, 1536, 1792, 2048


def _cparams(*sem):
    return pltpu.CompilerParams(dimension_semantics=sem, vmem_limit_bytes=VMEM_LIMIT)


def _dot(a, b):
    return jnp.dot(a, b, preferred_element_type=f32)


def _dot_nt(a, b):
    return lax.dot_general(a, b, (((1,), (1,)), ((), ())), preferred_element_type=f32)


def _rms(x):
    return x * lax.rsqrt(jnp.mean(x * x, axis=-1, keepdims=True) + EPS)


def _shr(n, k):
    return n >> k if isinstance(n, int) else lax.shift_right_logical(n, k)


def _aligned(x, m):
    return x if isinstance(x, int) else pl.multiple_of(x, m)


def _const_spec(shape):
    nd = len(shape)
    return pl.BlockSpec(shape, lambda *_: (0,) * nd)


def _mod_kernel(c_ref, w_ref, b_ref, o_ref):
    c = c_ref[...]
    s = c * jax.nn.sigmoid(c)
    o_ref[0] = _dot(s.astype(bf16), w_ref[0]) + b_ref[0]


def _modulation(cc, w_mod, b_mod):
    depth, d, n = w_mod.shape
    r = cc.shape[0]
    tn = 2048
    return pl.pallas_call(
        _mod_kernel,
        grid=(depth, n // tn),
        in_specs=[pl.BlockSpec((r, d), lambda l, j: (0, 0)),
                  pl.BlockSpec((1, d, tn), lambda l, j: (l, 0, j)),
                  pl.BlockSpec((1, 1, tn), lambda l, j: (l, 0, j))],
        out_specs=pl.BlockSpec((1, r, tn), lambda l, j: (l, 0, j)),
        out_shape=jax.ShapeDtypeStruct((depth, r, n), f32),
        compiler_params=_cparams("parallel", "parallel"),
        name="modulation",
    )(cc, w_mod.astype(bf16), b_mod.reshape(depth, 1, n))


def _in_proj_kernel(*refs, rope):
    (x_ref, mod_ref, g_ref, w_ref, qg_ref, kvg_ref, wq_ref, wqs_ref, wk_ref, wv_ref,
     e_ref, es_ref, p_ref) = refs[:13]
    if rope:
        cm_ref, sm_ref, cd_ref, sd_ref = refs[13:17]
        outs = refs[17:]
    else:
        outs = refs[13:]
    qm_ref, km_ref, vm_ref, qd_ref, kd_ref, vd_ref, hy_ref = outs

    md = mod_ref[0]
    subs = [slice(TQ * t, TQ * (t + 1)) for t in range(x_ref.shape[1] // TQ)]
    projected = []
    for rows in subs:
        h = _rms(x_ref[0, rows, :]) * g_ref[...] * (1.0 + md[1:2]) + md[0:1]
        projected.append(_dot(h.astype(bf16), w_ref[...]))

    for t, (rows, p) in enumerate(zip(subs, projected)):
        qn = (_rms(p[:, C_Q:C_KV]) * qg_ref[...]).astype(bf16)
        kvn = (_rms(p[:, C_KV:C_KR]) * kvg_ref[...]).astype(bf16)
        kr = p[:, C_KR:C_HY].astype(bf16)
        q = _dot(qn, wq_ref[...])
        k = _dot(kvn, wk_ref[...]) + _dot(kr, e_ref[...])
        dq = p[:, C_DQ:C_DK]
        dk = p[:, C_DK:C_DV]
        if rope:
            cm = jnp.concatenate([cm_ref[rows, :]] * MLA_HEADS, axis=-1)
            sm = jnp.concatenate([sm_ref[rows, :]] * MLA_HEADS, axis=-1)
            q = q * cm + _dot(qn, wqs_ref[...]) * sm
            k = k * cm + _dot(kr, es_ref[...]) * sm
            cd = jnp.concatenate([cd_ref[rows, :]] * 2, axis=-1)
            sd = jnp.concatenate([sd_ref[rows, :]] * 2, axis=-1)
            dq = dq * cd + _dot(dq.astype(bf16), p_ref[...]) * sd
            dk = dk * cd + _dot(dk.astype(bf16), p_ref[...]) * sd
        qm_ref[0, t] = (q * (MLA_SCALE * LOG2E)).T.astype(bf16)
        km_ref[0, rows, :] = k.astype(bf16)
        vm_ref[0, :, rows] = _dot(kvn, wv_ref[...]).T.astype(bf16)
        qd_ref[0, t] = (dq * (DF_SCALE * LOG2E)).T.astype(bf16)
        kd_ref[0, rows, :] = dk.astype(bf16)
        vd_ref[0, :, rows] = p[:, C_DV:C_END].T.astype(bf16)
        hy_ref[0, rows, :] = p[:, C_HY:C_DQ].astype(bf16)


def _in_proj(x, mod, per_batch_mod, g, lw, tables):
    b, lx, d = x.shape
    tm = min(512, lx)
    rope = tables is not None
    tok = lambda n: pl.BlockSpec((1, tm, n), lambda i, j: (j, i, 0))
    mod_map = (lambda i, j: (j, 0, 0)) if per_batch_mod else (lambda i, j: (0, 0, 0))
    weights = [lw['w_in1'], lw['q_norm_g'], lw['kv_norm_g'], lw['wq'], lw['wqs'], lw['wk'], lw['wv'],
               lw['e'], lw['es'], lw['p']]
    in_specs = ([tok(d), pl.BlockSpec((1, 6, d), mod_map), _const_spec(g.shape)]
                + [_const_spec(w.shape) for w in weights])
    args = [x, mod, g] + weights
    if rope:
        in_specs += [pl.BlockSpec((tm, LANES), lambda i, j: (i, 0))] * 4
        args += list(tables)
    outs = ((MLA_HEADS * LANES, 'q'), (MLA_HEADS * LANES, 't'), (MLA_OUT, 'f'),
            (2 * DF_HEADS * DF_DIM, 'q'), (2 * DF_HEADS * DF_DIM, 't'), (DF_OUT, 'f'),
            (3 * HY_WIDTH, 't'))
    spec = {'t': tok,
            'f': lambda n: pl.BlockSpec((1, n, tm), lambda i, j: (j, 0, i)),
            'q': lambda n: pl.BlockSpec((1, tm // TQ, n, TQ), lambda i, j: (j, i, 0, 0))}
    shape = {'t': lambda n: (b, lx, n), 'f': lambda n: (b, n, lx), 'q': lambda n: (b, lx // TQ, n, TQ)}
    return pl.pallas_call(
        functools.partial(_in_proj_kernel, rope=rope),
        grid=(lx // tm, b),
        in_specs=in_specs,
        out_specs=[spec[lay](n) for n, lay in outs],
        out_shape=[jax.ShapeDtypeStruct(shape[lay](n), bf16) for n, lay in outs],
        compiler_params=_cparams("parallel", "parallel"),
        name="in_proj",
    )(*args)


def _gather_sources(k_refs, vt_refs, k_all, vt_all):
    @pl.when(pl.program_id(1) == 0)
    def _():
        o = 0
        for k, vt in zip(k_refs, vt_refs):
            n = k.shape[1]
            for t in range(k_all.shape[0]):
                k_all[t, o:o + n, :] = k[0, :, LANES * t:LANES * (t + 1)]
            vt_all[:, o:o + n] = vt[0]
            o += n


def _softmax_maps(n_maps, qt_of, k_of, vt_of, s_scr, o_scr):
    _, lk, tq = s_scr.shape
    dv = o_scr.shape[0] // n_maps
    kc = KEY_CHUNK if lk % KEY_CHUNK == 0 else KEY_CHUNK // 2
    assert lk % kc == 0
    ones = jnp.ones((BF16_ROWS, kc), bf16)

    def scores(n, slot):
        qt = qt_of(n)
        pm = None
        for off in range(0, lk, kc):
            s = _dot(k_of(n, off, kc), qt)
            s_scr[slot, off:off + kc, :] = s
            cm = jnp.max(s.reshape(kc // 8, 8, tq), axis=0)
            pm = cm if pm is None else jnp.maximum(pm, cm)
        return jnp.max(pm, axis=0, keepdims=True)

    def values(n, slot, m):
        acc = None
        for off in range(0, lk, kc):
            p = jnp.exp2(s_scr[slot, off:off + kc, :] - m).astype(bf16)
            part = _dot(jnp.concatenate([vt_of(n, off, kc), ones], axis=0), p)
            acc = part if acc is None else acc + part
        o_scr[pl.ds(_aligned(n * dv, dv), dv), :] = acc[:dv] / acc[dv:dv + 1]

    def group(base, m, count):
        for e in range(count):
            m_next = scores(base + e + 1, (e + 1) & 1)
            values(base + e, e & 1, m)
            m = m_next
        return m

    per_iter = 4 if n_maps >= 16 else 2
    trips = (n_maps - 1) // per_iter
    m = lax.fori_loop(0, trips, lambda i, m: group(i * per_iter, m, per_iter), scores(0, 0))
    m = group(trips * per_iter, m, n_maps - 1 - trips * per_iter)
    values(n_maps - 1, (n_maps - 1) & 1, m)


def _mla_attn_kernel(*refs, n_src):
    qt_ref = refs[0]
    k_refs = refs[1:1 + n_src]
    vt_refs = refs[1 + n_src:1 + 2 * n_src]
    o_ref, k_all, vt_all, s_scr, o_scr = refs[1 + 2 * n_src:]
    _gather_sources(k_refs, vt_refs, k_all, vt_all)
    n_tiles = qt_ref.shape[1]
    tile_of = lambda n: _shr(n, 3)
    head_of = lambda n: n & (MAPS_PER_TILE - 1)
    _softmax_maps(
        n_tiles * MLA_HEADS,
        lambda n: qt_ref[0, tile_of(n), pl.ds(_aligned(head_of(n) * LANES, LANES), LANES), :],
        lambda n, off, kc: k_all[head_of(n), off:off + kc, :],
        lambda n, off, kc: vt_all[pl.ds(_aligned(head_of(n) * MLA_V, MLA_V), MLA_V), off:off + kc],
        s_scr, o_scr)
    for t in range(n_tiles):
        for j in range(MLA_OUT // LANES):
            r = t * MLA_OUT + LANES * j
            o_ref[0, TQ * t:TQ * (t + 1), LANES * j:LANES * (j + 1)] = o_scr[r:r + LANES, :].T.astype(bf16)


def _diff_attn_kernel(*refs, n_src, lam_init):
    lq1, lk1, lq2, lk2, g_ref, qt_ref = refs[:6]
    k_refs = refs[6:6 + n_src]
    vt_refs = refs[6 + n_src:6 + 2 * n_src]
    o_ref, k_all, vt_all, s_scr, o_scr = refs[6 + 2 * n_src:]
    _gather_sources(k_refs, vt_refs, k_all, vt_all)
    n_tiles = qt_ref.shape[1]
    row = lax.broadcasted_iota(jnp.int32, (LANES, TQ), 0)
    per_lanes = LANES // DF_DIM
    tile_of = lambda n: _shr(n, 3)
    map_of = lambda n: n & (MAPS_PER_TILE - 1)
    lanes_of = lambda n: _shr(map_of(n), 2)

    def qt_of(n):
        lo = DF_DIM * (n & (per_lanes - 1))
        qt = qt_ref[0, tile_of(n), pl.ds(_aligned(lanes_of(n) * LANES, LANES), LANES), :]
        return jnp.where(row < lo, 0.0, jnp.where(row < lo + DF_DIM, qt, 0.0)).astype(bf16)

    def vt_of(n, off, kc):
        h = _shr(map_of(n), 1)
        return vt_all[pl.ds(_aligned(h * DF_V, DF_V), DF_V), off:off + kc]

    _softmax_maps(n_tiles * 2 * DF_HEADS, qt_of, lambda n, off, kc: k_all[lanes_of(n), off:off + kc, :], vt_of,
                  s_scr, o_scr)

    lam = (jnp.exp(jnp.sum(lq1[...] * lk1[...], axis=-1, keepdims=True))
           - jnp.exp(jnp.sum(lq2[...] * lk2[...], axis=-1, keepdims=True)) + lam_init)
    g = jnp.concatenate([g_ref[...]] * (TQ // LANES), axis=-1)
    for t in range(n_tiles):
        for j in range(DF_OUT // LANES):
            pair = []
            for h in (2 * j, 2 * j + 1):
                r = (t * MAPS_PER_TILE + 2 * h) * DF_V
                o = o_scr[r:r + DF_V, :] - lam * o_scr[r + DF_V:r + 2 * DF_V, :]
                ms = jnp.mean(o * o, axis=0, keepdims=True)
                pair.append(o * lax.rsqrt(ms + EPS) * g * (1.0 - lam_init))
            o_ref[0, TQ * t:TQ * (t + 1), LANES * j:LANES * (j + 1)] = (
                jnp.concatenate(pair, axis=0).T.astype(bf16))


def _attn_call(kernel, name, small, qt, ks, vs, dv):
    b, n_qt, f, _ = qt.shape
    tiles = next(t for t in (4, 2, 1) if n_qt % t == 0)
    lk = sum(k.shape[1] for k in ks)
    f_out = vs[0].shape[1]
    full = lambda a: pl.BlockSpec((1,) + a.shape[1:], lambda i, j: (i, 0, 0))
    scratch = [pltpu.VMEM((f // LANES, lk, LANES), bf16), pltpu.VMEM((f_out, lk), bf16),
               pltpu.VMEM((2, lk, TQ), f32), pltpu.VMEM((tiles * MAPS_PER_TILE * dv, TQ), f32)]
    return pl.pallas_call(
        kernel,
        grid=(b, n_qt // tiles),
        in_specs=([_const_spec(a.shape) for a in small]
                  + [pl.BlockSpec((1, tiles, f, TQ), lambda i, j: (i, j, 0, 0))]
                  + [full(a) for a in ks] + [full(a) for a in vs]),
        out_specs=pl.BlockSpec((1, tiles * TQ, f_out), lambda i, j: (i, j, 0)),
        out_shape=jax.ShapeDtypeStruct((b, n_qt * TQ, f_out), bf16),
        scratch_shapes=scratch,
        compiler_params=_cparams("parallel", "arbitrary"),
        name=name,
    )(*small, qt, *ks, *vs)


def _mla_attn(q, ks, vs):
    return _attn_call(functools.partial(_mla_attn_kernel, n_src=len(ks)), "mla_attn", [], q, ks, vs, MLA_V)


def _diff_attn(lams, g, q, ks, vs, lam_init):
    return _attn_call(functools.partial(_diff_attn_kernel, n_src=len(ks), lam_init=lam_init), "diff_attn",
                      list(lams) + [g], q, ks, vs, DF_V)


def _hy_filter_kernel(z_ref, w1_ref, b1_ref, fr_ref, w2_ref, b2_ref, w3_ref, b3_ref, win_ref,
                      fc_ref, fs_ref, a_ref, a2_ref, bq_ref, h_scr):
    kt = pl.program_id(0)
    hp = lax.Precision.HIGHEST

    @pl.when(kt == 0)
    def _():
        fr = fr_ref[...]
        a = jnp.sin(fr * (jnp.dot(z_ref[...], w1_ref[...], precision=hp, preferred_element_type=f32)
                          + b1_ref[...]))
        a = jnp.sin(fr * (jnp.dot(a, w2_ref[...], precision=hp, preferred_element_type=f32) + b2_ref[...]))
        h = jnp.dot(a, w3_ref[...], precision=hp, preferred_element_type=f32) + b3_ref[...]
        win = win_ref[...]
        row = lax.broadcasted_iota(jnp.int32, win.shape, 0)
        h_scr[:, :HY_WIDTH] = (h[:, :HY_WIDTH] * win).astype(bf16)
        h_scr[:, HY_WIDTH:] = jnp.where(row == 0, 0.0, h[:, HY_WIDTH:] * win).astype(bf16)

    hh = h_scr[...]
    c = _dot(fc_ref[...], hh)
    s = _dot(fs_ref[...], hh)
    tk = c.shape[0]
    row = lax.broadcasted_iota(jnp.int32, (tk, HY_WIDTH), 0) + kt * tk
    first = row == 0
    a = c[:, :HY_WIDTH] + c[:, HY_WIDTH:]
    second = jnp.where(first, s[:, :HY_WIDTH] + s[:, HY_WIDTH:], s[:, :HY_WIDTH] - s[:, HY_WIDTH:])
    a_ref[...] = a
    a2_ref[...] = jnp.where(first, second, a)
    bq_ref[...] = jnp.where(first, 0.0, second)


def _hy_filter(z, lw, win, fmat):
    l = z.shape[0]
    tk = min(512, l)
    nk = l // tk
    small = [lw['hy_w1'], lw['hy_b1'], lw['hy_freq'], lw['hy_w2'], lw['hy_b2'], lw['hy_w3'], lw['hy_b3']]
    out = jax.ShapeDtypeStruct((l, HY_WIDTH), f32)
    return pl.pallas_call(
        _hy_filter_kernel,
        grid=(nk,),
        in_specs=([_const_spec(z.shape)] + [_const_spec(a.shape) for a in small] + [_const_spec(win.shape)]
                  + [pl.BlockSpec((tk, l), lambda k: (k, 0)), pl.BlockSpec((tk, l), lambda k: (k + nk, 0))]),
        out_specs=[pl.BlockSpec((tk, HY_WIDTH), lambda k: (k, 0))] * 3,
        out_shape=[out] * 3,
        scratch_shapes=[pltpu.VMEM((l, 2 * HY_WIDTH), bf16)],
        compiler_params=_cparams("arbitrary"),
        name="hy_filter",
    )(z, *small, win, fmat, fmat)


def _hy_conv_kernel(p_ref, cw_ref, cb_ref, skip_ref, fc_ref, fs_ref, gc_ref, gs_ref,
                    a_ref, a2_ref, bq_ref, o_ref, u_scr, ub_scr, x0_scr, acc_scr, *, bb):
    kt = pl.program_id(1)
    l = p_ref.shape[1]

    @pl.when(kt == 0)
    def _():
        row = lax.broadcasted_iota(jnp.int32, (l, 3 * HY_WIDTH), 0)
        cw = cw_ref[...]
        for i in range(bb):
            p = p_ref[i].astype(f32)
            prev = jnp.where(row == 0, 0.0, pltpu.roll(p, 1, axis=0))
            nxt = jnp.where(row == l - 1, 0.0, pltpu.roll(p, l - 1, axis=0))
            uc = cw[0:1] * prev + cw[1:2] * p + cw[2:3] * nxt + cb_ref[...]
            x0_scr[i] = uc[:, :HY_WIDTH]
            u = uc[:, 2 * HY_WIDTH:] * uc[:, HY_WIDTH:2 * HY_WIDTH]
            u_scr[i] = u
            ub_scr[i] = u.astype(bf16)
            acc_scr[i] = jnp.zeros((l, HY_WIDTH), f32)

    a, a2, bq = a_ref[...], a2_ref[...], bq_ref[...]
    spectra = [(_dot(fc_ref[...], ub_scr[i]), _dot(fs_ref[...], ub_scr[i])) for i in range(bb)]
    for i, (ur, ui) in enumerate(spectra):
        zr = (ur * a - ui * bq).astype(bf16)
        zi = (ur * bq + ui * a2).astype(bf16)
        acc_scr[i] += _dot(gc_ref[...], zr) + _dot(gs_ref[...], zi)

    @pl.when(kt == pl.num_programs(1) - 1)
    def _():
        for i in range(bb):
            o_ref[i] = (x0_scr[i] * (acc_scr[i] + skip_ref[...] * u_scr[i])).astype(bf16)


def _hy_conv(p_hy, lw, spec, fmat, gmat):
    b, l, _ = p_hy.shape
    bb = 2
    tk = min(512, l)
    nk = l // tk
    a, a2, bq = spec
    return pl.pallas_call(
        functools.partial(_hy_conv_kernel, bb=bb),
        grid=(b // bb, nk),
        in_specs=[pl.BlockSpec((bb, l, 3 * HY_WIDTH), lambda i, k: (i, 0, 0)),
                  _const_spec(lw['hy_conv_w'].shape), _const_spec(lw['hy_conv_b'].shape),
                  _const_spec(lw['hy_skip'].shape),
                  pl.BlockSpec((tk, l), lambda i, k: (k, 0)), pl.BlockSpec((tk, l), lambda i, k: (k + nk, 0)),
                  pl.BlockSpec((l, tk), lambda i, k: (0, k)), pl.BlockSpec((l, tk), lambda i, k: (0, k + nk))]
                 + [pl.BlockSpec((tk, HY_WIDTH), lambda i, k: (k, 0))] * 3,
        out_specs=pl.BlockSpec((bb, l, HY_WIDTH), lambda i, k: (i, 0, 0)),
        out_shape=jax.ShapeDtypeStruct((b, l, HY_WIDTH), bf16),
        scratch_shapes=[pltpu.VMEM((bb, l, HY_WIDTH), f32), pltpu.VMEM((bb, l, HY_WIDTH), bf16),
                        pltpu.VMEM((bb, l, HY_WIDTH), f32), pltpu.VMEM((bb, l, HY_WIDTH), f32)],
        compiler_params=_cparams("parallel", "arbitrary"),
        name="hy_conv",
    )(p_hy, lw['hy_conv_w'], lw['hy_conv_b'], lw['hy_skip'], fmat, fmat, gmat, gmat, a, a2, bq)


def _merge_kernel(x_ref, mod_ref, g_ref, ya_ref, yb_ref, yc_ref, wg_ref, wa_ref, wb_ref, wc_ref, wo_ref,
                  o_ref):
    md = mod_ref[0]
    d = D_MODEL
    subs = [slice(TQ * t, TQ * (t + 1)) for t in range(x_ref.shape[1] // TQ)]
    staged = []
    for rows in subs:
        h = (_rms(x_ref[0, rows, :]) * g_ref[...] * (1.0 + md[1:2]) + md[0:1]).astype(bf16)
        gates = [_dot(h, wg_ref[:, d * i:d * (i + 1)]) for i in range(N_BRANCH)]
        branches = [_dot(ya_ref[0, rows, :], wa_ref[...]), _dot(yb_ref[0, rows, :], wb_ref[...]),
                    _dot(yc_ref[0, rows, :], wc_ref[...])]
        staged.append((gates, branches))
    for rows, (gates, branches) in zip(subs, staged):
        m = functools.reduce(jnp.add, [jax.nn.sigmoid(gt) * br for gt, br in zip(gates, branches)])
        o_ref[0, rows, :] = x_ref[0, rows, :] + md[2:3] * _dot(m.astype(bf16), wo_ref[...])


def _merge(x, mod, per_batch_mod, g, ya, yb, yc, lw):
    b, lx, d = x.shape
    tm = min(512, lx)
    tok = lambda n: pl.BlockSpec((1, tm, n), lambda i, j: (i, j, 0))
    mod_map = (lambda i, j: (i, 0, 0)) if per_batch_mod else (lambda i, j: (0, 0, 0))
    weights = [lw['w_gate'], lw['w_br_a'], lw['w_br_b'], lw['w_br_c'], lw['w_out']]
    return pl.pallas_call(
        _merge_kernel,
        grid=(b, lx // tm),
        in_specs=[tok(d), pl.BlockSpec((1, 6, d), mod_map), _const_spec(g.shape),
                  tok(MLA_OUT), tok(HY_WIDTH), tok(DF_OUT)] + [_const_spec(w.shape) for w in weights],
        out_specs=tok(d),
        out_shape=jax.ShapeDtypeStruct(x.shape, f32),
        compiler_params=_cparams("parallel", "parallel"),
        name="merge",
    )(x, mod, g, ya, yb, yc, *weights)


def _mlp_kernel(*refs, final):
    x_ref, mod_ref, g_ref, w1_ref, w2_ref = refs[:5]
    o_ref = refs[-1]
    md = mod_ref[0]
    subs = [slice(TQ * t, TQ * (t + 1)) for t in range(x_ref.shape[1] // TQ)]
    hidden = []
    for rows in subs:
        h = (_rms(x_ref[0, rows, :]) * g_ref[...] * (1.0 + md[4:5]) + md[3:4]).astype(bf16)
        hidden.append(_dot(h, w1_ref[...]))
    for rows, a in zip(subs, hidden):
        a = jnp.maximum(a, 0.0)
        y = x_ref[0, rows, :] + md[5:6] * _dot((a * a).astype(bf16), w2_ref[...])
        if final:
            y = _rms(y) * refs[5][...]
        o_ref[0, rows, :] = y


def _mlp(x, mod, per_batch_mod, g, w1, w2, final_g=None):
    b, lx, d = x.shape
    tm = min(512, lx)
    tok = pl.BlockSpec((1, tm, d), lambda i, j: (i, j, 0))
    mod_map = (lambda i, j: (i, 0, 0)) if per_batch_mod else (lambda i, j: (0, 0, 0))
    final = final_g is not None
    resident = lambda w: pl.BlockSpec(w.shape, lambda i, j: (0, 0), pipeline_mode=pl.Buffered(1))
    in_specs = [tok, pl.BlockSpec((1, 6, d), mod_map), _const_spec(g.shape), resident(w1), resident(w2)]
    args = [x, mod, g, w1, w2]
    if final:
        in_specs.append(_const_spec(final_g.shape))
        args.append(final_g)
    return pl.pallas_call(
        functools.partial(_mlp_kernel, final=final),
        grid=(b, lx // tm),
        in_specs=in_specs,
        out_specs=tok,
        out_shape=jax.ShapeDtypeStruct(x.shape, f32),
        compiler_params=_cparams("parallel", "parallel"),
        name="mlp",
    )(*args)


def _rope_tables(n_tokens):
    rows = n_tokens // GRID_W
    row = jnp.repeat(jnp.arange(rows), GRID_W).astype(f32)
    col = jnp.tile(jnp.arange(GRID_W), rows).astype(f32)
    nf = MLA_ROPE // 4
    inv = ROPE_BASE ** (-jnp.arange(nf, dtype=f32) / nf)
    ang = jnp.concatenate([row[:, None] * inv, col[:, None] * inv], axis=-1)
    cos, sin = jnp.cos(ang), jnp.sin(ang)
    one = jnp.ones((n_tokens, MLA_NOPE), f32)
    zero = jnp.zeros((n_tokens, MLA_NOPE), f32)
    pad = jnp.zeros((n_tokens, LANES - MLA_NOPE - MLA_ROPE), f32)
    cm = jnp.concatenate([one, cos, cos, pad], axis=-1)
    sm = jnp.concatenate([zero, -sin, sin, pad], axis=-1)
    cd = jnp.tile(jnp.concatenate([cos, cos], axis=-1), (1, LANES // DF_DIM))
    sd = jnp.tile(jnp.concatenate([-sin, sin], axis=-1), (1, LANES // DF_DIM))
    return cm, sm, cd, sd


def _hy_positional(l):
    t = jnp.linspace(0.0, 1.0, l, dtype=f32)[:, None]
    w = (2.0 * math.pi / l) * jnp.arange(l, dtype=f32)[:, None]
    bands = jnp.linspace(1e-4, HY_BANDS - 1, HY_BANDS, dtype=f32)[None]
    z = jnp.concatenate([t, jnp.cos(bands * w), -jnp.sin(bands * w)], axis=-1)
    z = jnp.pad(z, ((0, 0), (0, LANES - HY_EMB)))
    deltas = jnp.linspace(math.log(HY_TARGET) / HY_SLOW, math.log(HY_TARGET) / HY_FAST, HY_WIDTH, dtype=f32)
    win = jnp.exp(-t * jnp.abs(deltas)) + HY_SHIFT
    return z, win


def _dft_tables(l):
    n2 = 2 * l
    n = jnp.arange(l, dtype=jnp.int32)[None, :]
    a = jnp.arange(l // GRID_W, dtype=jnp.int32)[:, None]
    b = jnp.arange(GRID_W, dtype=jnp.int32)[:, None]
    ang_a = ((GRID_W * a * n) % n2).astype(f32) * (2.0 * math.pi / n2)
    ang_b = ((b * n) % n2).astype(f32) * (2.0 * math.pi / n2)
    ca, sa, cb, sb = jnp.cos(ang_a), jnp.sin(ang_a), jnp.cos(ang_b), jnp.sin(ang_b)
    nyq = jnp.where(n[0] % 2 == 0, 1.0, -1.0)
    c = (ca[:, None, :] * cb[None] - sa[:, None, :] * sb[None]).reshape(l, l)
    s = -(sa[:, None, :] * cb[None] + ca[:, None, :] * sb[None]).reshape(l, l)
    fmat = jnp.concatenate([c, s.at[0].set(nyq)], axis=0)
    wgt = jnp.full((l,), 2.0 / n2, f32).at[0].set(1.0 / n2)
    cat, sat, cbt, sbt = ca.T, sa.T, cb.T, sb.T
    ct = (cat[:, :, None] * cbt[:, None, :] - sat[:, :, None] * sbt[:, None, :]).reshape(l, l)
    st = -(sat[:, :, None] * cbt[:, None, :] + cat[:, :, None] * sbt[:, None, :]).reshape(l, l)
    gmat = jnp.concatenate([ct * wgt[None], st.at[:, 0].set(nyq) * wgt[None]], axis=1)
    return fmat.astype(bf16), gmat.astype(bf16)


def _prep_weights(w_in, mla_q_norm_g, mla_w_uq, mla_kv_norm_g, mla_w_ukv, hy_conv_w, hy_conv_b, hy_w1, hy_b1,
                  hy_freq, hy_w2, hy_b2, hy_w3, hy_b3, hy_skip, df_subln_g, w_br_a, w_br_b, w_br_c, w_out,
                  w_fc1, w_fc2):
    depth = w_in.shape[0]
    o = [0]
    for n in (MLA_Q_RANK, MLA_KV_RANK, MLA_ROPE, 3 * HY_WIDTH, 2 * DF_HEADS * DF_DIM, 2 * DF_HEADS * DF_DIM,
              DF_OUT, N_BRANCH * D_MODEL):
        o.append(o[-1] + n)
    kr_pad = jnp.zeros((depth, D_MODEL, LANES - MLA_ROPE), f32)
    w_in1 = jnp.concatenate([w_in[..., o[0]:o[3]], kr_pad, w_in[..., o[3]:o[7]]], axis=-1).astype(bf16)
    w_gate = w_in[..., o[7]:o[8]].astype(bf16)

    uq = mla_w_uq.reshape(depth, MLA_Q_RANK, MLA_HEADS, MLA_NOPE + MLA_ROPE)
    nope, rope = uq[..., :MLA_NOPE], uq[..., MLA_NOPE:]
    half = MLA_ROPE // 2
    zn = jnp.zeros_like(nope)
    zp = jnp.zeros(uq.shape[:3] + (LANES - MLA_NOPE - MLA_ROPE,), f32)
    wq = jnp.concatenate([nope, rope, zp], axis=-1).reshape(depth, MLA_Q_RANK, MLA_HEADS * LANES)
    wqs = jnp.concatenate([zn, rope[..., half:], rope[..., :half], zp], axis=-1)
    wqs = wqs.reshape(depth, MLA_Q_RANK, MLA_HEADS * LANES)
    ukv = mla_w_ukv.reshape(depth, MLA_KV_RANK, MLA_HEADS, MLA_NOPE + MLA_V)
    wk = jnp.concatenate([ukv[..., :MLA_NOPE], jnp.zeros_like(ukv[..., MLA_NOPE:])], axis=-1)
    wk = wk.reshape(depth, MLA_KV_RANK, MLA_HEADS * LANES)
    wv = ukv[..., MLA_NOPE:].reshape(depth, MLA_KV_RANK, MLA_OUT)

    i = jnp.arange(LANES)[:, None]
    j = jnp.arange(MLA_HEADS * LANES)[None, :]
    js = j % LANES - MLA_NOPE
    e = ((i < MLA_ROPE) & (js == i)).astype(bf16)
    es = ((i < MLA_ROPE) & (js >= 0) & (js < MLA_ROPE) & (js == (i + half) % MLA_ROPE)).astype(bf16)
    a = jnp.arange(2 * DF_HEADS * DF_DIM)
    dh = DF_DIM // 2
    swap = (a // DF_DIM) * DF_DIM + (a % DF_DIM + dh) % DF_DIM
    p = (a[:, None] == swap[None, :]).astype(bf16)

    pad_rows = lambda w, r: jnp.pad(w, ((0, 0), (0, r - w.shape[1]), (0, 0)))
    pad_cols = lambda w, c: jnp.pad(w, ((0, 0),) * (w.ndim - 1) + ((0, c - w.shape[-1]),))
    layers = []
    for l in range(depth):
        layers.append({
            'w_in1': w_in1[l], 'w_gate': w_gate[l],
            'q_norm_g': mla_q_norm_g[l][None], 'kv_norm_g': mla_kv_norm_g[l][None],
            'wq': wq[l].astype(bf16), 'wqs': wqs[l].astype(bf16), 'wk': wk[l].astype(bf16),
            'wv': wv[l].astype(bf16), 'e': e, 'es': es, 'p': p,
            'hy_conv_w': hy_conv_w[l], 'hy_conv_b': hy_conv_b[l][None], 'hy_skip': hy_skip[l][None],
            'hy_w1': pad_cols(pad_rows(hy_w1, LANES), LANES)[l],
            'hy_b1': pad_cols(hy_b1, LANES)[l][None], 'hy_freq': pad_cols(hy_freq, LANES)[l][None],
            'hy_w2': pad_cols(pad_rows(hy_w2, LANES), LANES)[l],
            'hy_b2': pad_cols(hy_b2, LANES)[l][None],
            'hy_w3': pad_rows(hy_w3, LANES)[l], 'hy_b3': hy_b3[l][None],
            'subln_g': jnp.broadcast_to(df_subln_g[l][:, None], (DF_V, LANES)),
            'w_br_a': w_br_a[l].astype(bf16), 'w_br_b': w_br_b[l].astype(bf16),
            'w_br_c': w_br_c[l].astype(bf16), 'w_out': w_out[l].astype(bf16),
            'w_fc1': w_fc1[l].astype(bf16), 'w_fc2': w_fc2[l].astype(bf16),
        })
    return layers


def _token_mixers(q_parts, ctx_parts, lw, lams, lam_init, hy_lat, hy_ctx, with_ctx):
    qm, km, vm, qd, kd, vd, hy = q_parts
    cqm, ckm, cvm, cqd, ckd, cvd, chy = ctx_parts
    ya = _mla_attn(qm, [ckm, km], [cvm, vm])
    yc = _diff_attn(lams, lw['subln_g'], qd, [ckd, kd], [cvd, vd], lam_init)
    yb = _hy_conv(hy, lw, *hy_lat)
    if not with_ctx:
        return (ya, yb, yc), None
    ya_c = _mla_attn(cqm, [ckm], [cvm])
    yc_c = _diff_attn(lams, lw['subln_g'], cqd, [ckd], [cvd], lam_init)
    yb_c = _hy_conv(chy, lw, *hy_ctx)
    return (ya, yb, yc), (ya_c, yb_c, yc_c)


def kernel(x, c, ctx, c_ctx, norm_mix_g, norm_ffn_g, w_mod, b_mod, w_in, mla_q_norm_g, mla_w_uq, mla_kv_norm_g, mla_w_ukv, hy_conv_w, hy_conv_b, hy_w1, hy_b1, hy_freq, hy_w2, hy_b2, hy_w3, hy_b3, hy_skip, df_lq1, df_lk1, df_lq2, df_lk2, df_subln_g, w_br_a, w_br_b, w_br_c, w_out, w_fc1, w_fc2, final_norm_g):
    b, n_lat, d = x.shape
    n_ctx = ctx.shape[1]
    depth = w_in.shape[0]
    layers = _prep_weights(w_in, mla_q_norm_g, mla_w_uq, mla_kv_norm_g, mla_w_ukv, hy_conv_w, hy_conv_b,
                           hy_w1, hy_b1, hy_freq, hy_w2, hy_b2, hy_w3, hy_b3, hy_skip, df_subln_g,
                           w_br_a, w_br_b, w_br_c, w_out, w_fc1, w_fc2)
    tables = _rope_tables(n_lat)
    z_lat, win_lat = _hy_positional(n_lat)
    z_ctx, win_ctx = _hy_positional(n_ctx)
    f_lat, g_lat = _dft_tables(n_lat)
    f_ctx, g_ctx = _dft_tables(n_ctx)

    rows = -(-(b + 1) // 8) * 8
    cc = jnp.concatenate([c, c_ctx[None], jnp.zeros((rows - b - 1, d), f32)], axis=0)
    mod = _modulation(cc, w_mod, b_mod).reshape(depth, rows, 6, d)

    xc = ctx
    for l in range(depth):
        lw = layers[l]
        with_ctx = l < depth - 1
        lam_init = 0.8 - 0.6 * math.exp(-0.3 * l)
        mod_lat, mod_ctx = mod[l, :b], mod[l, b:b + 1]
        g_mix, g_ffn = norm_mix_g[l][None], norm_ffn_g[l][None]
        lams = [df_lq1[l][None], df_lk1[l][None], df_lq2[l][None], df_lk2[l][None]]

        hy_lat = (_hy_filter(z_lat, lw, win_lat, f_lat), f_lat, g_lat)
        hy_ctx = (_hy_filter(z_ctx, lw, win_ctx, f_ctx), f_ctx, g_ctx) if with_ctx else None

        parts = _in_proj(x, mod_lat, True, g_mix, lw, tables)
        parts_c = _in_proj(xc, mod_ctx, False, g_mix, lw, None)
        ys, ys_c = _token_mixers(parts, parts_c, lw, lams, lam_init, hy_lat, hy_ctx, with_ctx)

        x = _merge(x, mod_lat, True, g_mix, *ys, lw)
        x = _mlp(x, mod_lat, True, g_ffn, lw['w_fc1'], lw['w_fc2'],
                 final_norm_g[None] if l == depth - 1 else None)
        if with_ctx:
            xc = _merge(xc, mod_ctx, False, g_mix, *ys_c, lw)
            xc = _mlp(xc, mod_ctx, False, g_ffn, lw['w_fc1'], lw['w_fc2'])
    return x
```

```python
import functools
import math

import jax
import jax.numpy as jnp
from jax import lax
from jax.experimental import pallas as pl
from jax.experimental.pallas import tpu as pltpu

f32 = jnp.float32
bf16 = jnp.bfloat16

D_MODEL = 1024
GRID_W = 64
EPS = 1e-6
ROPE_BASE = 10000.0

MLA_HEADS = 8
MLA_NOPE = 64
MLA_ROPE = 32
MLA_V = 64
MLA_Q_RANK = 256
MLA_KV_RANK = 128
MLA_OUT = MLA_HEADS * MLA_V
MLA_SCALE = (MLA_NOPE + MLA_ROPE) ** -0.5

HY_WIDTH = 256
HY_EMB = 33
HY_BANDS = (HY_EMB - 1) // 2
HY_HIDDEN = 64
HY_TARGET = 1e-2
HY_FAST = 0.3
HY_SLOW = 1.5
HY_SHIFT = 0.05

DF_HEADS = 4
DF_DIM = 32
DF_V = 2 * DF_DIM
DF_OUT = DF_HEADS * DF_V
DF_SCALE = DF_DIM ** -0.5

N_BRANCH = 3
D_FF = 4 * D_MODEL

LANES = 128
BF16_ROWS = 16
KEY_CHUNK = 256
TQ = 256
MAPS_PER_TILE = 8
LOG2E = 1.4426950408889634
VMEM_LIMIT = 56 * 1024 * 1024

C_Q, C_KV, C_KR, C_HY, C_DQ, C_DK, C_DV, C_END = 0, 256, 384, 512, 1280, 1536, 1792, 2048


def _cparams(*sem):
    return pltpu.CompilerParams(dimension_semantics=sem, vmem_limit_bytes=VMEM_LIMIT)


def _dot(a, b):
    return jnp.dot(a, b, preferred_element_type=f32)


def _dot_nt(a, b):
    return lax.dot_general(a, b, (((1,), (1,)), ((), ())), preferred_element_type=f32)


def _rms(x):
    return x * lax.rsqrt(jnp.mean(x * x, axis=-1, keepdims=True) + EPS)


def _shr(n, k):
    return n >> k if isinstance(n, int) else lax.shift_right_logical(n, k)


def _aligned(x, m):
    return x if isinstance(x, int) else pl.multiple_of(x, m)


def _const_spec(shape):
    nd = len(shape)
    return pl.BlockSpec(shape, lambda *_: (0,) * nd)


def _mod_kernel(c_ref, w_ref, b_ref, o_ref):
    c = c_ref[...]
    s = c * jax.nn.sigmoid(c)
    o_ref[0] = _dot(s.astype(bf16), w_ref[0]) + b_ref[0]


def _modulation(cc, w_mod, b_mod):
    depth, d, n = w_mod.shape
    r = cc.shape[0]
    tn = 2048
    return pl.pallas_call(
        _mod_kernel,
        grid=(depth, n // tn),
        in_specs=[pl.BlockSpec((r, d), lambda l, j: (0, 0)),
                  pl.BlockSpec((1, d, tn), lambda l, j: (l, 0, j)),
                  pl.BlockSpec((1, 1, tn), lambda l, j: (l, 0, j))],
        out_specs=pl.BlockSpec((1, r, tn), lambda l, j: (l, 0, j)),
        out_shape=jax.ShapeDtypeStruct((depth, r, n), f32),
        compiler_params=_cparams("parallel", "parallel"),
        name="modulation",
    )(cc, w_mod.astype(bf16), b_mod.reshape(depth, 1, n))


def _in_proj_kernel(*refs, rope, n_alias):
    (x_ref, mod_ref, g_ref, w_ref, qg_ref, kvg_ref, wq_ref, wqs_ref, wke_ref, wv_ref, p_ref) = refs[:11]
    refs = refs[11:]
    if rope:
        cm_ref, sm_ref, kt_ref, cd_ref, sd_ref = refs[:5]
        refs = refs[5:]
    qm_ref, km_ref, vm_ref, qd_ref, kd_ref, vd_ref, hy_ref = refs[n_alias:]

    md = mod_ref[0]
    subs = [slice(TQ * t, TQ * (t + 1)) for t in range(x_ref.shape[1] // TQ)]
    projected = []
    for rows in subs:
        h = _rms(x_ref[0, rows, :]) * g_ref[...] * (1.0 + md[1:2]) + md[0:1]
        projected.append(_dot(h.astype(bf16), w_ref[...]))

    for t, (rows, p) in enumerate(zip(subs, projected)):
        qn = (_rms(p[:, C_Q:C_KV]) * qg_ref[...]).astype(bf16)
        kvn = (_rms(p[:, C_KV:C_KR]) * kvg_ref[...]).astype(bf16)
        kr = p[:, C_KR:C_HY]
        q = _dot(qn, wq_ref[...])
        dq = p[:, C_DQ:C_DK]
        dk = p[:, C_DK:C_DV]
        if rope:
            kr = kr * kt_ref[rows, :]
            cm = jnp.concatenate([cm_ref[rows, :]] * MLA_HEADS, axis=-1)
            sm = jnp.concatenate([sm_ref[rows, :]] * MLA_HEADS, axis=-1)
            q = q * cm + _dot(qn, wqs_ref[...]) * sm
            cd = jnp.concatenate([cd_ref[rows, :]] * 2, axis=-1)
            sd = jnp.concatenate([sd_ref[rows, :]] * 2, axis=-1)
            dq = dq * cd + _dot(dq.astype(bf16), p_ref[...]) * sd
            dk = dk * cd + _dot(dk.astype(bf16), p_ref[...]) * sd
        k = _dot(jnp.concatenate([kvn, kr.astype(bf16)], axis=-1), wke_ref[...]).astype(bf16)
        dk = dk.astype(bf16)
        qm_ref[0, t] = (q * (MLA_SCALE * LOG2E)).T.astype(bf16)
        for h in range(km_ref.shape[1]):
            km_ref[0, h, rows, :] = k[:, LANES * h:LANES * (h + 1)]
        vm_ref[0, :, rows] = _dot(kvn, wv_ref[...]).T.astype(bf16)
        qd_ref[0, t] = (dq * (DF_SCALE * LOG2E)).T.astype(bf16)
        for h in range(kd_ref.shape[1]):
            kd_ref[0, h, rows, :] = dk[:, LANES * h:LANES * (h + 1)]
        vd_ref[0, :, rows] = p[:, C_DV:C_END].T.astype(bf16)
        hy_ref[0, rows, :] = p[:, C_HY:C_DQ].astype(bf16)


def _in_proj(x, mod, per_batch_mod, g, lw, tables, n_keys, key_block, kv_bufs=None):
    b, lx, d = x.shape
    tm = min(512, lx)
    rope = tables is not None
    kb = key_block * (lx // tm)
    tok = lambda n: pl.BlockSpec((1, tm, n), lambda i, j: (j, i, 0))
    mod_map = (lambda i, j: (j, 0, 0)) if per_batch_mod else (lambda i, j: (0, 0, 0))
    weights = [lw['w_in1'], lw['q_norm_g'], lw['kv_norm_g'], lw['wq'], lw['wqs'],
               lw['wke_rope' if rope else 'wke'], lw['wv'], lw['p']]
    in_specs = ([tok(d), pl.BlockSpec((1, 6, d), mod_map), _const_spec(g.shape)]
                + [_const_spec(w.shape) for w in weights])
    args = [x, mod, g] + weights
    if rope:
        in_specs += [pl.BlockSpec((tm, LANES), lambda i, j: (i, 0))] * len(tables)
        args += list(tables)
    aliases = {}
    if kv_bufs is not None:
        aliases = {len(args) + n: o for n, o in enumerate((1, 2, 4, 5))}
        in_specs += [pl.BlockSpec(memory_space=pl.ANY)] * len(kv_bufs)
        args += list(kv_bufs)
    f_q, f_d = MLA_HEADS * LANES, 2 * DF_HEADS * DF_DIM
    qtile = lambda n: pl.BlockSpec((1, tm // TQ, n, TQ), lambda i, j: (j, i, 0, 0))
    keys = lambda n: pl.BlockSpec((1, n // LANES, tm, LANES), lambda i, j: (j, 0, kb + i, 0))
    vals = lambda n: pl.BlockSpec((1, n, tm), lambda i, j: (j, 0, kb + i))
    sds = lambda *shape: jax.ShapeDtypeStruct(shape, bf16)
    out_specs = [qtile(f_q), keys(f_q), vals(MLA_OUT), qtile(f_d), keys(f_d), vals(DF_OUT), tok(3 * HY_WIDTH)]
    out_shape = [sds(b, lx // TQ, f_q, TQ), sds(b, f_q // LANES, n_keys, LANES), sds(b, MLA_OUT, n_keys),
                 sds(b, lx // TQ, f_d, TQ), sds(b, f_d // LANES, n_keys, LANES), sds(b, DF_OUT, n_keys),
                 sds(b, lx, 3 * HY_WIDTH)]
    return pl.pallas_call(
        functools.partial(_in_proj_kernel, rope=rope, n_alias=len(aliases)),
        grid=(lx // tm, b),
        in_specs=in_specs,
        out_specs=out_specs,
        out_shape=out_shape,
        input_output_aliases=aliases,
        compiler_params=_cparams("parallel", "parallel"),
        name="in_proj",
    )(*args)


def _softmax_maps(n_maps, qt_of, k_of, vt_of, s_scr, o_scr):
    _, lk, tq = s_scr.shape
    dv = o_scr.shape[0] // n_maps
    kc = KEY_CHUNK if lk % KEY_CHUNK == 0 else KEY_CHUNK // 2
    assert lk % kc == 0
    ones = jnp.ones((BF16_ROWS, kc), bf16)

    def scores(n, slot):
        qt = qt_of(n)
        pm = None
        for off in range(0, lk, kc):
            s = _dot(k_of(n, off, kc), qt)
            s_scr[slot, off:off + kc, :] = s
            cm = jnp.max(s.reshape(kc // 8, 8, tq), axis=0)
            pm = cm if pm is None else jnp.maximum(pm, cm)
        return jnp.max(pm, axis=0, keepdims=True)

    def values(n, slot, m):
        acc = None
        for off in range(0, lk, kc):
            p = jnp.exp2(s_scr[slot, off:off + kc, :] - m).astype(bf16)
            part = _dot(jnp.concatenate([vt_of(n, off, kc), ones], axis=0), p)
            acc = part if acc is None else acc + part
        o_scr[pl.ds(_aligned(n * dv, dv), dv), :] = acc[:dv] / acc[dv:dv + 1]

    def group(base, m, count):
        for e in range(count):
            m_next = scores(base + e + 1, (e + 1) & 1)
            values(base + e, e & 1, m)
            m = m_next
        return m

    per_iter = 4 if n_maps >= 16 else 2
    trips = (n_maps - 1) // per_iter
    m = lax.fori_loop(0, trips, lambda i, m: group(i * per_iter, m, per_iter), scores(0, 0))
    m = group(trips * per_iter, m, n_maps - 1 - trips * per_iter)
    values(n_maps - 1, (n_maps - 1) & 1, m)


def _mla_attn_kernel(qt_ref, k_ref, vt_ref, o_ref, s_scr, o_scr):
    n_tiles = qt_ref.shape[1]
    tile_of = lambda n: _shr(n, 3)
    head_of = lambda n: n & (MAPS_PER_TILE - 1)
    _softmax_maps(
        n_tiles * MLA_HEADS,
        lambda n: qt_ref[0, tile_of(n), pl.ds(_aligned(head_of(n) * LANES, LANES), LANES), :],
        lambda n, off, kc: k_ref[0, head_of(n), off:off + kc, :],
        lambda n, off, kc: vt_ref[0, pl.ds(_aligned(head_of(n) * MLA_V, MLA_V), MLA_V), off:off + kc],
        s_scr, o_scr)
    for t in range(n_tiles):
        for j in range(MLA_OUT // LANES):
            r = t * MLA_OUT + LANES * j
            o_ref[0, TQ * t:TQ * (t + 1), LANES * j:LANES * (j + 1)] = o_scr[r:r + LANES, :].T.astype(bf16)


def _diff_attn_kernel(lq1, lk1, lq2, lk2, g_ref, qt_ref, k_ref, vt_ref, o_ref, s_scr, o_scr, *, lam_init):
    n_tiles = qt_ref.shape[1]
    row = lax.broadcasted_iota(jnp.int32, (LANES, TQ), 0)
    per_lanes = LANES // DF_DIM
    tile_of = lambda n: _shr(n, 3)
    map_of = lambda n: n & (MAPS_PER_TILE - 1)
    lanes_of = lambda n: _shr(map_of(n), 2)

    def qt_of(n):
        lo = DF_DIM * (n & (per_lanes - 1))
        qt = qt_ref[0, tile_of(n), pl.ds(_aligned(lanes_of(n) * LANES, LANES), LANES), :]
        return jnp.where(row < lo, 0.0, jnp.where(row < lo + DF_DIM, qt, 0.0)).astype(bf16)

    def vt_of(n, off, kc):
        h = _shr(map_of(n), 1)
        return vt_ref[0, pl.ds(_aligned(h * DF_V, DF_V), DF_V), off:off + kc]

    _softmax_maps(n_tiles * 2 * DF_HEADS, qt_of, lambda n, off, kc: k_ref[0, lanes_of(n), off:off + kc, :], vt_of,
                  s_scr, o_scr)

    lam = (jnp.exp(jnp.sum(lq1[...] * lk1[...], axis=-1, keepdims=True))
           - jnp.exp(jnp.sum(lq2[...] * lk2[...], axis=-1, keepdims=True)) + lam_init)
    g = jnp.concatenate([g_ref[...]] * (TQ // LANES), axis=-1)
    for t in range(n_tiles):
        for j in range(DF_OUT // LANES):
            pair = []
            for h in (2 * j, 2 * j + 1):
                r = (t * MAPS_PER_TILE + 2 * h) * DF_V
                o = o_scr[r:r + DF_V, :] - lam * o_scr[r + DF_V:r + 2 * DF_V, :]
                ms = jnp.mean(o * o, axis=0, keepdims=True)
                pair.append(o * lax.rsqrt(ms + EPS) * g * (1.0 - lam_init))
            o_ref[0, TQ * t:TQ * (t + 1), LANES * j:LANES * (j + 1)] = (
                jnp.concatenate(pair, axis=0).T.astype(bf16))


def _attn_call(kernel, name, small, qt, k_all, vt_all, n_keys, key_block, dv):
    b, n_qt, f, _ = qt.shape
    f_out = vt_all.shape[1]
    tiles = next(t for t in (4, 2, 1) if n_qt % t == 0)
    scratch = [pltpu.VMEM((2, n_keys, TQ), f32), pltpu.VMEM((tiles * MAPS_PER_TILE * dv, TQ), f32)]
    return pl.pallas_call(
        kernel,
        grid=(b, n_qt // tiles),
        in_specs=([_const_spec(a.shape) for a in small]
                  + [pl.BlockSpec((1, tiles, f, TQ), lambda i, j: (i, j, 0, 0)),
                     pl.BlockSpec((1, f // LANES, n_keys, LANES), lambda i, j: (i, 0, key_block, 0)),
                     pl.BlockSpec((1, f_out, n_keys), lambda i, j: (i, 0, key_block))]),
        out_specs=pl.BlockSpec((1, tiles * TQ, f_out), lambda i, j: (i, j, 0)),
        out_shape=jax.ShapeDtypeStruct((b, n_qt * TQ, f_out), bf16),
        scratch_shapes=scratch,
        compiler_params=_cparams("parallel", "parallel"),
        name=name,
    )(*small, qt, k_all, vt_all)


def _mla_attn(q, k_all, vt_all, n_keys, key_block):
    return _attn_call(_mla_attn_kernel, "mla_attn", [], q, k_all, vt_all, n_keys, key_block, MLA_V)


def _diff_attn(lams, g, q, k_all, vt_all, n_keys, key_block, lam_init):
    return _attn_call(functools.partial(_diff_attn_kernel, lam_init=lam_init), "diff_attn",
                      list(lams) + [g], q, k_all, vt_all, n_keys, key_block, DF_V)


def _hy_filter_kernel(z_ref, w1_ref, b1_ref, fr_ref, w2_ref, b2_ref, w3_ref, b3_ref, win_ref,
                      fc_ref, fs_ref, a_ref, a2_ref, bq_ref, h_scr):
    kt = pl.program_id(0)
    hp = lax.Precision.HIGHEST

    @pl.when(kt == 0)
    def _():
        fr = fr_ref[...]
        a = jnp.sin(fr * (jnp.dot(z_ref[...], w1_ref[...], precision=hp, preferred_element_type=f32)
                          + b1_ref[...]))
        a = jnp.sin(fr * (jnp.dot(a, w2_ref[...], precision=hp, preferred_element_type=f32) + b2_ref[...]))
        h = jnp.dot(a, w3_ref[...], precision=hp, preferred_element_type=f32) + b3_ref[...]
        win = win_ref[...]
        row = lax.broadcasted_iota(jnp.int32, win.shape, 0)
        h_scr[:, :HY_WIDTH] = (h[:, :HY_WIDTH] * win).astype(bf16)
        h_scr[:, HY_WIDTH:] = jnp.where(row == 0, 0.0, h[:, HY_WIDTH:] * win).astype(bf16)

    hh = h_scr[...]
    c = _dot(fc_ref[...], hh)
    s = _dot(fs_ref[...], hh)
    tk = c.shape[0]
    row = lax.broadcasted_iota(jnp.int32, (tk, HY_WIDTH), 0) + kt * tk
    first = row == 0
    a = c[:, :HY_WIDTH] + c[:, HY_WIDTH:]
    second = jnp.where(first, s[:, :HY_WIDTH] + s[:, HY_WIDTH:], s[:, :HY_WIDTH] - s[:, HY_WIDTH:])
    a_ref[...] = a
    a2_ref[...] = jnp.where(first, second, a)
    bq_ref[...] = jnp.where(first, 0.0, second)


def _hy_filter(z, lw, win, fmat):
    l = z.shape[0]
    tk = min(512, l)
    nk = l // tk
    small = [lw['hy_w1'], lw['hy_b1'], lw['hy_freq'], lw['hy_w2'], lw['hy_b2'], lw['hy_w3'], lw['hy_b3']]
    out = jax.ShapeDtypeStruct((l, HY_WIDTH), f32)
    return pl.pallas_call(
        _hy_filter_kernel,
        grid=(nk,),
        in_specs=([_const_spec(z.shape)] + [_const_spec(a.shape) for a in small] + [_const_spec(win.shape)]
                  + [pl.BlockSpec((tk, l), lambda k: (k, 0)), pl.BlockSpec((tk, l), lambda k: (k + nk, 0))]),
        out_specs=[pl.BlockSpec((tk, HY_WIDTH), lambda k: (k, 0))] * 3,
        out_shape=[out] * 3,
        scratch_shapes=[pltpu.VMEM((l, 2 * HY_WIDTH), bf16)],
        compiler_params=_cparams("arbitrary"),
        name="hy_filter",
    )(z, *small, win, fmat, fmat)


def _hy_conv_kernel(p_ref, cw_ref, cb_ref, skip_ref, fc_ref, fs_ref, gc_ref, gs_ref,
                    a_ref, a2_ref, bq_ref, o_ref, u_scr, ub_scr, x0_scr, acc_scr, *, bb):
    kt = pl.program_id(1)
    l = p_ref.shape[1]

    @pl.when(kt == 0)
    def _():
        row = lax.broadcasted_iota(jnp.int32, (l, 3 * HY_WIDTH), 0)
        cw = cw_ref[...]
        for i in range(bb):
            p = p_ref[i].astype(f32)
            prev = jnp.where(row == 0, 0.0, pltpu.roll(p, 1, axis=0))
            nxt = jnp.where(row == l - 1, 0.0, pltpu.roll(p, l - 1, axis=0))
            uc = cw[0:1] * prev + cw[1:2] * p + cw[2:3] * nxt + cb_ref[...]
            x0_scr[i] = uc[:, :HY_WIDTH]
            u = uc[:, 2 * HY_WIDTH:] * uc[:, HY_WIDTH:2 * HY_WIDTH]
            u_scr[i] = u
            ub_scr[i] = u.astype(bf16)
            acc_scr[i] = jnp.zeros((l, HY_WIDTH), f32)

    a, a2, bq = a_ref[...], a2_ref[...], bq_ref[...]
    spectra = [(_dot(fc_ref[...], ub_scr[i]), _dot(fs_ref[...], ub_scr[i])) for i in range(bb)]
    for i, (ur, ui) in enumerate(spectra):
        zr = (ur * a - ui * bq).astype(bf16)
        zi = (ur * bq + ui * a2).astype(bf16)
        acc_scr[i] += _dot(gc_ref[...], zr) + _dot(gs_ref[...], zi)

    @pl.when(kt == pl.num_programs(1) - 1)
    def _():
        for i in range(bb):
            o_ref[i] = (x0_scr[i] * (acc_scr[i] + skip_ref[...] * u_scr[i])).astype(bf16)


def _hy_conv(p_hy, lw, spec, fmat, gmat):
    b, l, _ = p_hy.shape
    bb = 2
    tk = min(512, l)
    nk = l // tk
    a, a2, bq = spec
    return pl.pallas_call(
        functools.partial(_hy_conv_kernel, bb=bb),
        grid=(b // bb, nk),
        in_specs=[pl.BlockSpec((bb, l, 3 * HY_WIDTH), lambda i, k: (i, 0, 0)),
                  _const_spec(lw['hy_conv_w'].shape), _const_spec(lw['hy_conv_b'].shape),
                  _const_spec(lw['hy_skip'].shape),
                  pl.BlockSpec((tk, l), lambda i, k: (k, 0)), pl.BlockSpec((tk, l), lambda i, k: (k + nk, 0)),
                  pl.BlockSpec((l, tk), lambda i, k: (0, k)), pl.BlockSpec((l, tk), lambda i, k: (0, k + nk))]
                 + [pl.BlockSpec((tk, HY_WIDTH), lambda i, k: (k, 0))] * 3,
        out_specs=pl.BlockSpec((bb, l, HY_WIDTH), lambda i, k: (i, 0, 0)),
        out_shape=jax.ShapeDtypeStruct((b, l, HY_WIDTH), bf16),
        scratch_shapes=[pltpu.VMEM((bb, l, HY_WIDTH), f32), pltpu.VMEM((bb, l, HY_WIDTH), bf16),
                        pltpu.VMEM((bb, l, HY_WIDTH), f32), pltpu.VMEM((bb, l, HY_WIDTH), f32)],
        compiler_params=_cparams("parallel", "arbitrary"),
        name="hy_conv",
    )(p_hy, lw['hy_conv_w'], lw['hy_conv_b'], lw['hy_skip'], fmat, fmat, gmat, gmat, a, a2, bq)


def _merge_kernel(x_ref, mod_ref, g_ref, ya_ref, yb_ref, yc_ref, wg_ref, wa_ref, wb_ref, wc_ref, wo_ref,
                  o_ref):
    md = mod_ref[0]
    d = D_MODEL
    subs = [slice(TQ * t, TQ * (t + 1)) for t in range(x_ref.shape[1] // TQ)]
    staged = []
    for rows in subs:
        h = (_rms(x_ref[0, rows, :]) * g_ref[...] * (1.0 + md[1:2]) + md[0:1]).astype(bf16)
        gates = [_dot(h, wg_ref[:, d * i:d * (i + 1)]) for i in range(N_BRANCH)]
        branches = [_dot(ya_ref[0, rows, :], wa_ref[...]), _dot(yb_ref[0, rows, :], wb_ref[...]),
                    _dot(yc_ref[0, rows, :], wc_ref[...])]
        staged.append((gates, branches))
    for rows, (gates, branches) in zip(subs, staged):
        m = functools.reduce(jnp.add, [jax.nn.sigmoid(gt) * br for gt, br in zip(gates, branches)])
        o_ref[0, rows, :] = x_ref[0, rows, :] + md[2:3] * _dot(m.astype(bf16), wo_ref[...])


def _merge(x, mod, per_batch_mod, g, ya, yb, yc, lw):
    b, lx, d = x.shape
    tm = min(512, lx)
    tok = lambda n: pl.BlockSpec((1, tm, n), lambda i, j: (i, j, 0))
    mod_map = (lambda i, j: (i, 0, 0)) if per_batch_mod else (lambda i, j: (0, 0, 0))
    weights = [lw['w_gate'], lw['w_br_a'], lw['w_br_b'], lw['w_br_c'], lw['w_out']]
    return pl.pallas_call(
        _merge_kernel,
        grid=(b, lx // tm),
        in_specs=[tok(d), pl.BlockSpec((1, 6, d), mod_map), _const_spec(g.shape),
                  tok(MLA_OUT), tok(HY_WIDTH), tok(DF_OUT)] + [_const_spec(w.shape) for w in weights],
        out_specs=tok(d),
        out_shape=jax.ShapeDtypeStruct(x.shape, f32),
        compiler_params=_cparams("parallel", "parallel"),
        name="merge",
    )(x, mod, g, ya, yb, yc, *weights)


def _mlp_kernel(*refs, final):
    x_ref, mod_ref, g_ref, w1_ref, w2_ref = refs[:5]
    o_ref = refs[-1]
    md = mod_ref[0]
    subs = [slice(TQ * t, TQ * (t + 1)) for t in range(x_ref.shape[1] // TQ)]
    hidden = []
    for rows in subs:
        h = (_rms(x_ref[0, rows, :]) * g_ref[...] * (1.0 + md[4:5]) + md[3:4]).astype(bf16)
        hidden.append(_dot(h, w1_ref[...]))
    for rows, a in zip(subs, hidden):
        a = jnp.maximum(a, 0.0)
        y = x_ref[0, rows, :] + md[5:6] * _dot((a * a).astype(bf16), w2_ref[...])
        if final:
            y = _rms(y) * refs[5][...]
        o_ref[0, rows, :] = y


def _mlp(x, mod, per_batch_mod, g, w1, w2, final_g=None):
    b, lx, d = x.shape
    tm = min(512, lx)
    tok = pl.BlockSpec((1, tm, d), lambda i, j: (i, j, 0))
    mod_map = (lambda i, j: (i, 0, 0)) if per_batch_mod else (lambda i, j: (0, 0, 0))
    final = final_g is not None
    resident = lambda w: pl.BlockSpec(w.shape, lambda i, j: (0, 0), pipeline_mode=pl.Buffered(1))
    in_specs = [tok, pl.BlockSpec((1, 6, d), mod_map), _const_spec(g.shape), resident(w1), resident(w2)]
    args = [x, mod, g, w1, w2]
    if final:
        in_specs.append(_const_spec(final_g.shape))
        args.append(final_g)
    return pl.pallas_call(
        functools.partial(_mlp_kernel, final=final),
        grid=(b, lx // tm),
        in_specs=in_specs,
        out_specs=tok,
        out_shape=jax.ShapeDtypeStruct(x.shape, f32),
        compiler_params=_cparams("parallel", "parallel"),
        name="mlp",
    )(*args)


def _rope_tables(n_tokens):
    rows = n_tokens // GRID_W
    row = jnp.repeat(jnp.arange(rows), GRID_W).astype(f32)
    col = jnp.tile(jnp.arange(GRID_W), rows).astype(f32)
    nf = MLA_ROPE // 4
    inv = ROPE_BASE ** (-jnp.arange(nf, dtype=f32) / nf)
    ang = jnp.concatenate([row[:, None] * inv, col[:, None] * inv], axis=-1)
    cos, sin = jnp.cos(ang), jnp.sin(ang)
    one = jnp.ones((n_tokens, MLA_NOPE), f32)
    zero = jnp.zeros((n_tokens, MLA_NOPE), f32)
    pad = jnp.zeros((n_tokens, LANES - MLA_NOPE - MLA_ROPE), f32)
    cm = jnp.concatenate([one, cos, cos, pad], axis=-1)
    sm = jnp.concatenate([zero, -sin, sin, pad], axis=-1)
    kt = jnp.concatenate([cos, cos, -sin, sin, jnp.zeros((n_tokens, LANES - 2 * MLA_ROPE), f32)], axis=-1)
    cd = jnp.tile(jnp.concatenate([cos, cos], axis=-1), (1, LANES // DF_DIM))
    sd = jnp.tile(jnp.concatenate([-sin, sin], axis=-1), (1, LANES // DF_DIM))
    return cm, sm, kt, cd, sd


def _hy_positional(l):
    t = jnp.linspace(0.0, 1.0, l, dtype=f32)[:, None]
    w = (2.0 * math.pi / l) * jnp.arange(l, dtype=f32)[:, None]
    bands = jnp.linspace(1e-4, HY_BANDS - 1, HY_BANDS, dtype=f32)[None]
    z = jnp.concatenate([t, jnp.cos(bands * w), -jnp.sin(bands * w)], axis=-1)
    z = jnp.pad(z, ((0, 0), (0, LANES - HY_EMB)))
    deltas = jnp.linspace(math.log(HY_TARGET) / HY_SLOW, math.log(HY_TARGET) / HY_FAST, HY_WIDTH, dtype=f32)
    win = jnp.exp(-t * jnp.abs(deltas)) + HY_SHIFT
    return z, win


def _dft_tables(l):
    n2 = 2 * l
    n = jnp.arange(l, dtype=jnp.int32)[None, :]
    a = jnp.arange(l // GRID_W, dtype=jnp.int32)[:, None]
    b = jnp.arange(GRID_W, dtype=jnp.int32)[:, None]
    ang_a = ((GRID_W * a * n) % n2).astype(f32) * (2.0 * math.pi / n2)
    ang_b = ((b * n) % n2).astype(f32) * (2.0 * math.pi / n2)
    ca, sa, cb, sb = jnp.cos(ang_a), jnp.sin(ang_a), jnp.cos(ang_b), jnp.sin(ang_b)
    nyq = jnp.where(n[0] % 2 == 0, 1.0, -1.0)
    c = (ca[:, None, :] * cb[None] - sa[:, None, :] * sb[None]).reshape(l, l)
    s = -(sa[:, None, :] * cb[None] + ca[:, None, :] * sb[None]).reshape(l, l)
    fmat = jnp.concatenate([c, s.at[0].set(nyq)], axis=0)
    wgt = jnp.full((l,), 2.0 / n2, f32).at[0].set(1.0 / n2)
    cat, sat, cbt, sbt = ca.T, sa.T, cb.T, sb.T
    ct = (cat[:, :, None] * cbt[:, None, :] - sat[:, :, None] * sbt[:, None, :]).reshape(l, l)
    st = -(sat[:, :, None] * cbt[:, None, :] + cat[:, :, None] * sbt[:, None, :]).reshape(l, l)
    gmat = jnp.concatenate([ct * wgt[None], st.at[:, 0].set(nyq) * wgt[None]], axis=1)
    return fmat.astype(bf16), gmat.astype(bf16)


def _prep_weights(w_in, mla_q_norm_g, mla_w_uq, mla_kv_norm_g, mla_w_ukv, hy_conv_w, hy_conv_b, hy_w1, hy_b1,
                  hy_freq, hy_w2, hy_b2, hy_w3, hy_b3, hy_skip, df_subln_g, w_br_a, w_br_b, w_br_c, w_out,
                  w_fc1, w_fc2):
    depth = w_in.shape[0]
    o = [0]
    for n in (MLA_Q_RANK, MLA_KV_RANK, MLA_ROPE, 3 * HY_WIDTH, 2 * DF_HEADS * DF_DIM, 2 * DF_HEADS * DF_DIM,
              DF_OUT, N_BRANCH * D_MODEL):
        o.append(o[-1] + n)
    half = MLA_ROPE // 2
    w_kr = w_in[..., o[2]:o[3]]
    kr_pad = jnp.zeros((depth, D_MODEL, LANES - 2 * MLA_ROPE), f32)
    w_in1 = jnp.concatenate([w_in[..., o[0]:o[3]], w_kr[..., half:], w_kr[..., :half], kr_pad,
                             w_in[..., o[3]:o[7]]], axis=-1).astype(bf16)
    w_gate = w_in[..., o[7]:o[8]].astype(bf16)

    uq = mla_w_uq.reshape(depth, MLA_Q_RANK, MLA_HEADS, MLA_NOPE + MLA_ROPE)
    nope, rope = uq[..., :MLA_NOPE], uq[..., MLA_NOPE:]
    zn = jnp.zeros_like(nope)
    zp = jnp.zeros(uq.shape[:3] + (LANES - MLA_NOPE - MLA_ROPE,), f32)
    wq = jnp.concatenate([nope, rope, zp], axis=-1).reshape(depth, MLA_Q_RANK, MLA_HEADS * LANES)
    wqs = jnp.concatenate([zn, rope[..., half:], rope[..., :half], zp], axis=-1)
    wqs = wqs.reshape(depth, MLA_Q_RANK, MLA_HEADS * LANES)
    ukv = mla_w_ukv.reshape(depth, MLA_KV_RANK, MLA_HEADS, MLA_NOPE + MLA_V)
    wk = jnp.concatenate([ukv[..., :MLA_NOPE], jnp.zeros_like(ukv[..., MLA_NOPE:])], axis=-1)
    wk = wk.reshape(depth, MLA_KV_RANK, MLA_HEADS * LANES)
    wv = ukv[..., MLA_NOPE:].reshape(depth, MLA_KV_RANK, MLA_OUT)

    i = jnp.arange(LANES)[:, None]
    js = jnp.arange(MLA_HEADS * LANES)[None, :] % LANES - MLA_NOPE
    e = ((i < MLA_ROPE) & (js == i)).astype(f32)
    e_rope = ((i < 2 * MLA_ROPE) & (js == i % MLA_ROPE)).astype(f32)
    wke = jnp.concatenate([wk, jnp.broadcast_to(e, (depth,) + e.shape)], axis=1).astype(bf16)
    wke_rope = jnp.concatenate([wk, jnp.broadcast_to(e_rope, (depth,) + e.shape)], axis=1).astype(bf16)
    a = jnp.arange(2 * DF_HEADS * DF_DIM)
    dh = DF_DIM // 2
    swap = (a // DF_DIM) * DF_DIM + (a % DF_DIM + dh) % DF_DIM
    p = (a[:, None] == swap[None, :]).astype(bf16)

    pad_rows = lambda w, r: jnp.pad(w, ((0, 0), (0, r - w.shape[1]), (0, 0)))
    pad_cols = lambda w, c: jnp.pad(w, ((0, 0),) * (w.ndim - 1) + ((0, c - w.shape[-1]),))
    layers = []
    for l in range(depth):
        layers.append({
            'w_in1': w_in1[l], 'w_gate': w_gate[l],
            'q_norm_g': mla_q_norm_g[l][None], 'kv_norm_g': mla_kv_norm_g[l][None],
            'wq': wq[l].astype(bf16), 'wqs': wqs[l].astype(bf16), 'wke': wke[l], 'wke_rope': wke_rope[l],
            'wv': wv[l].astype(bf16), 'p': p,
            'hy_conv_w': hy_conv_w[l], 'hy_conv_b': hy_conv_b[l][None], 'hy_skip': hy_skip[l][None],
            'hy_w1': pad_cols(pad_rows(hy_w1, LANES), LANES)[l],
            'hy_b1': pad_cols(hy_b1, LANES)[l][None], 'hy_freq': pad_cols(hy_freq, LANES)[l][None],
            'hy_w2': pad_cols(pad_rows(hy_w2, LANES), LANES)[l],
            'hy_b2': pad_cols(hy_b2, LANES)[l][None],
            'hy_w3': pad_rows(hy_w3, LANES)[l], 'hy_b3': hy_b3[l][None],
            'subln_g': jnp.broadcast_to(df_subln_g[l][:, None], (DF_V, LANES)),
            'w_br_a': w_br_a[l].astype(bf16), 'w_br_b': w_br_b[l].astype(bf16),
            'w_br_c': w_br_c[l].astype(bf16), 'w_out': w_out[l].astype(bf16),
            'w_fc1': w_fc1[l].astype(bf16), 'w_fc2': w_fc2[l].astype(bf16),
        })
    return layers


def _token_mixers(lat, ctx, n_lat, n_ctx, lw, lams, lam_init, hy_lat, hy_ctx, with_ctx):
    qm, _, _, qd, _, _, hy = lat
    cqm, km, vm, cqd, kd, vd, chy = ctx
    n_all = n_lat + n_ctx
    ya = _mla_attn(qm, km, vm, n_all, 0)
    yc = _diff_attn(lams, lw['subln_g'], qd, kd, vd, n_all, 0, lam_init)
    yb = _hy_conv(hy, lw, *hy_lat)
    if not with_ctx:
        return (ya, yb, yc), None
    ya_c = _mla_attn(cqm, km, vm, n_ctx, n_lat // n_ctx)
    yc_c = _diff_attn(lams, lw['subln_g'], cqd, kd, vd, n_ctx, n_lat // n_ctx, lam_init)
    yb_c = _hy_conv(chy, lw, *hy_ctx)
    return (ya, yb, yc), (ya_c, yb_c, yc_c)


def kernel(x, c, ctx, c_ctx, norm_mix_g, norm_ffn_g, w_mod, b_mod, w_in, mla_q_norm_g, mla_w_uq, mla_kv_norm_g, mla_w_ukv, hy_conv_w, hy_conv_b, hy_w1, hy_b1, hy_freq, hy_w2, hy_b2, hy_w3, hy_b3, hy_skip, df_lq1, df_lk1, df_lq2, df_lk2, df_subln_g, w_br_a, w_br_b, w_br_c, w_out, w_fc1, w_fc2, final_norm_g):
    b, n_lat, d = x.shape
    n_ctx = ctx.shape[1]
    assert n_lat % n_ctx == 0 and n_ctx % TQ == 0
    depth = w_in.shape[0]
    layers = _prep_weights(w_in, mla_q_norm_g, mla_w_uq, mla_kv_norm_g, mla_w_ukv, hy_conv_w, hy_conv_b,
                           hy_w1, hy_b1, hy_freq, hy_w2, hy_b2, hy_w3, hy_b3, hy_skip, df_subln_g,
                           w_br_a, w_br_b, w_br_c, w_out, w_fc1, w_fc2)
    tables = _rope_tables(n_lat)
    z_lat, win_lat = _hy_positional(n_lat)
    z_ctx, win_ctx = _hy_positional(n_ctx)
    f_lat, g_lat = _dft_tables(n_lat)
    f_ctx, g_ctx = _dft_tables(n_ctx)

    rows = -(-(b + 1) // 8) * 8
    cc = jnp.concatenate([c, c_ctx[None], jnp.zeros((rows - b - 1, d), f32)], axis=0)
    mod = _modulation(cc, w_mod, b_mod).reshape(depth, rows, 6, d)

    xc = ctx
    for l in range(depth):
        lw = layers[l]
        with_ctx = l < depth - 1
        lam_init = 0.8 - 0.6 * math.exp(-0.3 * l)
        mod_lat, mod_ctx = mod[l, :b], mod[l, b:b + 1]
        g_mix, g_ffn = norm_mix_g[l][None], norm_ffn_g[l][None]
        lams = [df_lq1[l][None], df_lk1[l][None], df_lq2[l][None], df_lk2[l][None]]

        hy_lat = (_hy_filter(z_lat, lw, win_lat, f_lat), f_lat, g_lat)
        hy_ctx = (_hy_filter(z_ctx, lw, win_ctx, f_ctx), f_ctx, g_ctx) if with_ctx else None

        n_all = n_lat + n_ctx
        parts = _in_proj(x, mod_lat, True, g_mix, lw, tables, n_all, 0)
        kv_bufs = [parts[i] for i in (1, 2, 4, 5)]
        parts_c = _in_proj(xc, mod_ctx, False, g_mix, lw, None, n_all, n_lat // n_ctx, kv_bufs)
        ys, ys_c = _token_mixers(parts, parts_c, n_lat, n_ctx, lw, lams, lam_init, hy_lat, hy_ctx, with_ctx)

        x = _merge(x, mod_lat, True, g_mix, *ys, lw)
        x = _mlp(x, mod_lat, True, g_ffn, lw['w_fc1'], lw['w_fc2'],
                 final_norm_g[None] if l == depth - 1 else None)
        if with_ctx:
            xc = _merge(xc, mod_ctx, False, g_mix, *ys_c, lw)
            xc = _mlp(xc, mod_ctx, False, g_ffn, lw['w_fc1'], lw['w_fc2'])
    return x
```

```python
import functools
import math

import jax
import jax.numpy as jnp
from jax import lax
from jax.experimental import pallas as pl
from jax.experimental.pallas import tpu as pltpu

f32 = jnp.float32
bf16 = jnp.bfloat16

D_MODEL = 1024
GRID_W = 64
EPS = 1e-6
ROPE_BASE = 10000.0

MLA_HEADS = 8
MLA_NOPE = 64
MLA_ROPE = 32
MLA_V = 64
MLA_Q_RANK = 256
MLA_KV_RANK = 128
MLA_OUT = MLA_HEADS * MLA_V
MLA_SCALE = (MLA_NOPE + MLA_ROPE) ** -0.5

HY_WIDTH = 256
HY_EMB = 33
HY_BANDS = (HY_EMB - 1) // 2
HY_HIDDEN = 64
HY_TARGET = 1e-2
HY_FAST = 0.3
HY_SLOW = 1.5
HY_SHIFT = 0.05

DF_HEADS = 4
DF_DIM = 32
DF_V = 2 * DF_DIM
DF_OUT = DF_HEADS * DF_V
DF_SCALE = DF_DIM ** -0.5

N_BRANCH = 3
D_FF = 4 * D_MODEL

LANES = 128
BF16_ROWS = 16
KEY_CHUNK = 256
TQ = 256
MAPS_PER_TILE = 8
LOG2E = 1.4426950408889634
VMEM_LIMIT = 56 * 1024 * 1024

C_Q, C_KV, C_KR, C_HY, C_DQ, C_DK, C_DV, C_END = 0, 256, 384, 512, 1280, 1536, 1792, 2048


def _cparams(*sem):
    return pltpu.CompilerParams(dimension_semantics=sem, vmem_limit_bytes=VMEM_LIMIT)


def _dot(a, b):
    return jnp.dot(a, b, preferred_element_type=f32)


def _dot_nt(a, b):
    return lax.dot_general(a, b, (((1,), (1,)), ((), ())), preferred_element_type=f32)


def _rms(x):
    return x * lax.rsqrt(jnp.mean(x * x, axis=-1, keepdims=True) + EPS)


def _shr(n, k):
    return n >> k if isinstance(n, int) else lax.shift_right_logical(n, k)


def _aligned(x, m):
    return x if isinstance(x, int) else pl.multiple_of(x, m)


def _const_spec(shape):
    nd = len(shape)
    return pl.BlockSpec(shape, lambda *_: (0,) * nd)


def _mod_kernel(c_ref, w_ref, b_ref, o_ref):
    c = c_ref[...]
    s = c * jax.nn.sigmoid(c)
    o_ref[0] = _dot(s.astype(bf16), w_ref[0]) + b_ref[0]


def _modulation(cc, w_mod, b_mod):
    depth, d, n = w_mod.shape
    r = cc.shape[0]
    tn = 2048
    return pl.pallas_call(
        _mod_kernel,
        grid=(depth, n // tn),
        in_specs=[pl.BlockSpec((r, d), lambda l, j: (0, 0)),
                  pl.BlockSpec((1, d, tn), lambda l, j: (l, 0, j)),
                  pl.BlockSpec((1, 1, tn), lambda l, j: (l, 0, j))],
        out_specs=pl.BlockSpec((1, r, tn), lambda l, j: (l, 0, j)),
        out_shape=jax.ShapeDtypeStruct((depth, r, n), f32),
        compiler_params=_cparams("parallel", "parallel"),
        name="modulation",
    )(cc, w_mod.astype(bf16), b_mod.reshape(depth, 1, n))


def _in_proj_kernel(*refs, rope, n_alias):
    (x_ref, mod_ref, g_ref, w_ref, qg_ref, kvg_ref, wq_ref, wqs_ref, wke_ref, wv_ref, p_ref) = refs[:11]
    refs = refs[11:]
    if rope:
        cm_ref, sm_ref, kt_ref, cd_ref, sd_ref = refs[:5]
        refs = refs[5:]
    qm_ref, km_ref, vm_ref, qd_ref, kd_ref, vd_ref, hy_ref = refs[n_alias:]

    md = mod_ref[0]
    subs = [slice(TQ * t, TQ * (t + 1)) for t in range(x_ref.shape[1] // TQ)]
    projected = []
    for rows in subs:
        h = _rms(x_ref[0, rows, :]) * g_ref[...] * (1.0 + md[1:2]) + md[0:1]
        projected.append(_dot(h.astype(bf16), w_ref[...]))

    for t, (rows, p) in enumerate(zip(subs, projected)):
        qn = (_rms(p[:, C_Q:C_KV]) * qg_ref[...]).astype(bf16)
        kvn = (_rms(p[:, C_KV:C_KR]) * kvg_ref[...]).astype(bf16)
        kr = p[:, C_KR:C_HY]
        q = _dot(qn, wq_ref[...])
        dq = p[:, C_DQ:C_DK]
        dk = p[:, C_DK:C_DV]
        if rope:
            kr = kr * kt_ref[rows, :]
            cm = jnp.concatenate([cm_ref[rows, :]] * MLA_HEADS, axis=-1)
            sm = jnp.concatenate([sm_ref[rows, :]] * MLA_HEADS, axis=-1)
            q = q * cm + _dot(qn, wqs_ref[...]) * sm
            cd = jnp.concatenate([cd_ref[rows, :]] * 2, axis=-1)
            sd = jnp.concatenate([sd_ref[rows, :]] * 2, axis=-1)
            dq = dq * cd + _dot(dq.astype(bf16), p_ref[...]) * sd
            dk = dk * cd + _dot(dk.astype(bf16), p_ref[...]) * sd
        k = _dot(jnp.concatenate([kvn, kr.astype(bf16)], axis=-1), wke_ref[...]).astype(bf16)
        dk = dk.astype(bf16)
        qm_ref[0, t] = (q * (MLA_SCALE * LOG2E)).T.astype(bf16)
        for h in range(km_ref.shape[1]):
            km_ref[0, h, rows, :] = k[:, LANES * h:LANES * (h + 1)]
        vm_ref[0, :, rows] = _dot(kvn, wv_ref[...]).T.astype(bf16)
        qd_ref[0, t] = (dq * (DF_SCALE * LOG2E)).T.astype(bf16)
        for h in range(kd_ref.shape[1]):
            kd_ref[0, h, rows, :] = dk[:, LANES * h:LANES * (h + 1)]
        vd_ref[0, :, rows] = p[:, C_DV:C_END].T.astype(bf16)
        hy_ref[0, rows, :] = p[:, C_HY:C_DQ].astype(bf16)


def _in_proj(x, mod, per_batch_mod, g, lw, tables, n_keys, key_block, kv_bufs=None):
    b, lx, d = x.shape
    tm = min(512, lx)
    rope = tables is not None
    kb = key_block * (lx // tm)
    tok = lambda n: pl.BlockSpec((1, tm, n), lambda i, j: (j, i, 0))
    mod_map = (lambda i, j: (j, 0, 0)) if per_batch_mod else (lambda i, j: (0, 0, 0))
    weights = [lw['w_in1'], lw['q_norm_g'], lw['kv_norm_g'], lw['wq'], lw['wqs'],
               lw['wke_rope' if rope else 'wke'], lw['wv'], lw['p']]
    in_specs = ([tok(d), pl.BlockSpec((1, 6, d), mod_map), _const_spec(g.shape)]
                + [_const_spec(w.shape) for w in weights])
    args = [x, mod, g] + weights
    if rope:
        in_specs += [pl.BlockSpec((tm, LANES), lambda i, j: (i, 0))] * len(tables)
        args += list(tables)
    aliases = {}
    if kv_bufs is not None:
        aliases = {len(args) + n: o for n, o in enumerate((1, 2, 4, 5))}
        in_specs += [pl.BlockSpec(memory_space=pl.ANY)] * len(kv_bufs)
        args += list(kv_bufs)
    f_q, f_d = MLA_HEADS * LANES, 2 * DF_HEADS * DF_DIM
    qtile = lambda n: pl.BlockSpec((1, tm // TQ, n, TQ), lambda i, j: (j, i, 0, 0))
    keys = lambda n: pl.BlockSpec((1, n // LANES, tm, LANES), lambda i, j: (j, 0, kb + i, 0))
    vals = lambda n: pl.BlockSpec((1, n, tm), lambda i, j: (j, 0, kb + i))
    sds = lambda *shape: jax.ShapeDtypeStruct(shape, bf16)
    out_specs = [qtile(f_q), keys(f_q), vals(MLA_OUT), qtile(f_d), keys(f_d), vals(DF_OUT), tok(3 * HY_WIDTH)]
    out_shape = [sds(b, lx // TQ, f_q, TQ), sds(b, f_q // LANES, n_keys, LANES), sds(b, MLA_OUT, n_keys),
                 sds(b, lx // TQ, f_d, TQ), sds(b, f_d // LANES, n_keys, LANES), sds(b, DF_OUT, n_keys),
                 sds(b, lx, 3 * HY_WIDTH)]
    return pl.pallas_call(
        functools.partial(_in_proj_kernel, rope=rope, n_alias=len(aliases)),
        grid=(lx // tm, b),
        in_specs=in_specs,
        out_specs=out_specs,
        out_shape=out_shape,
        input_output_aliases=aliases,
        compiler_params=_cparams("parallel", "parallel"),
        name="in_proj",
    )(*args)


def _softmax_maps(n_maps, qt_of, k_of, vt_of, s_scr, o_scr):
    _, lk, tq = s_scr.shape
    dv = o_scr.shape[0] // n_maps
    kc = KEY_CHUNK if lk % KEY_CHUNK == 0 else KEY_CHUNK // 2
    assert lk % kc == 0
    ones = jnp.ones((BF16_ROWS, kc), bf16)

    def scores(n, slot):
        qt = qt_of(n)
        pm = None
        for off in range(0, lk, kc):
            s = _dot(k_of(n, off, kc), qt)
            s_scr[slot, off:off + kc, :] = s
            cm = jnp.max(s.reshape(kc // 8, 8, tq), axis=0)
            pm = cm if pm is None else jnp.maximum(pm, cm)
        return jnp.max(pm, axis=0, keepdims=True)

    def values(n, slot, m):
        acc = None
        for off in range(0, lk, kc):
            p = jnp.exp2(s_scr[slot, off:off + kc, :] - m).astype(bf16)
            part = _dot(jnp.concatenate([vt_of(n, off, kc), ones], axis=0), p)
            acc = part if acc is None else acc + part
        o_scr[pl.ds(_aligned(n * dv, dv), dv), :] = acc[:dv] / acc[dv:dv + 1]

    if lk == kc:
        qks = [_dot(k_of(n, 0, kc), qt_of(n)) for n in range(n_maps)]
        col_max = lambda s: jnp.max(jnp.max(s.reshape(kc // 8, 8, tq), axis=0), axis=0, keepdims=True)
        pts = [jnp.exp2(s - col_max(s)).astype(bf16) for s in qks]
        for n, p in enumerate(pts):
            acc = _dot(jnp.concatenate([vt_of(n, 0, kc), ones], axis=0), p)
            o_scr[n * dv:(n + 1) * dv, :] = acc[:dv] / acc[dv:dv + 1]
        return

    def group(base, m, count):
        for e in range(count):
            m_next = scores(base + e + 1, (e + 1) & 1)
            values(base + e, e & 1, m)
            m = m_next
        return m

    per_iter = 8 if n_maps >= 64 else (4 if n_maps >= 16 else 2)
    trips = (n_maps - 1) // per_iter
    m = lax.fori_loop(0, trips, lambda i, m: group(i * per_iter, m, per_iter), scores(0, 0))
    m = group(trips * per_iter, m, n_maps - 1 - trips * per_iter)
    values(n_maps - 1, (n_maps - 1) & 1, m)


def _mla_attn_kernel(qt_ref, k_ref, vt_ref, o_ref, s_scr, o_scr):
    n_tiles = qt_ref.shape[1]
    tile_of = lambda n: _shr(n, 3)
    head_of = lambda n: n & (MAPS_PER_TILE - 1)
    _softmax_maps(
        n_tiles * MLA_HEADS,
        lambda n: qt_ref[0, tile_of(n), pl.ds(_aligned(head_of(n) * LANES, LANES), LANES), :],
        lambda n, off, kc: k_ref[0, head_of(n), off:off + kc, :],
        lambda n, off, kc: vt_ref[0, pl.ds(_aligned(head_of(n) * MLA_V, MLA_V), MLA_V), off:off + kc],
        s_scr, o_scr)
    for t in range(n_tiles):
        for j in range(MLA_OUT // LANES):
            r = t * MLA_OUT + LANES * j
            o_ref[0, TQ * t:TQ * (t + 1), LANES * j:LANES * (j + 1)] = o_scr[r:r + LANES, :].T.astype(bf16)


def _diff_attn_kernel(lq1, lk1, lq2, lk2, g_ref, qt_ref, k_ref, vt_ref, o_ref, s_scr, o_scr, *, lam_init):
    n_tiles = qt_ref.shape[1]
    row = lax.broadcasted_iota(jnp.int32, (LANES, TQ), 0)
    per_lanes = LANES // DF_DIM
    tile_of = lambda n: _shr(n, 3)
    map_of = lambda n: n & (MAPS_PER_TILE - 1)
    lanes_of = lambda n: _shr(map_of(n), 2)

    def qt_of(n):
        lo = DF_DIM * (n & (per_lanes - 1))
        qt = qt_ref[0, tile_of(n), pl.ds(_aligned(lanes_of(n) * LANES, LANES), LANES), :]
        return jnp.where(row < lo, 0.0, jnp.where(row < lo + DF_DIM, qt, 0.0)).astype(bf16)

    def vt_of(n, off, kc):
        h = _shr(map_of(n), 1)
        return vt_ref[0, pl.ds(_aligned(h * DF_V, DF_V), DF_V), off:off + kc]

    _softmax_maps(n_tiles * 2 * DF_HEADS, qt_of, lambda n, off, kc: k_ref[0, lanes_of(n), off:off + kc, :], vt_of,
                  s_scr, o_scr)

    lam = (jnp.exp(jnp.sum(lq1[...] * lk1[...], axis=-1, keepdims=True))
           - jnp.exp(jnp.sum(lq2[...] * lk2[...], axis=-1, keepdims=True)) + lam_init)
    g = jnp.concatenate([g_ref[...]] * (TQ // LANES), axis=-1)
    for t in range(n_tiles):
        for j in range(DF_OUT // LANES):
            pair = []
            for h in (2 * j, 2 * j + 1):
                r = (t * MAPS_PER_TILE + 2 * h) * DF_V
                o = o_scr[r:r + DF_V, :] - lam * o_scr[r + DF_V:r + 2 * DF_V, :]
                ms = jnp.mean(o * o, axis=0, keepdims=True)
                pair.append(o * lax.rsqrt(ms + EPS) * g * (1.0 - lam_init))
            o_ref[0, TQ * t:TQ * (t + 1), LANES * j:LANES * (j + 1)] = (
                jnp.concatenate(pair, axis=0).T.astype(bf16))


def _attn_call(kernel, name, small, qt, k_all, vt_all, n_keys, key_block, dv):
    b, n_qt, f, _ = qt.shape
    f_out = vt_all.shape[1]
    tiles = next(t for t in (8, 4, 2, 1) if n_qt % t == 0)
    scratch = [pltpu.VMEM((2, n_keys, TQ), f32), pltpu.VMEM((tiles * MAPS_PER_TILE * dv, TQ), f32)]
    return pl.pallas_call(
        kernel,
        grid=(b, n_qt // tiles),
        in_specs=([_const_spec(a.shape) for a in small]
                  + [pl.BlockSpec((1, tiles, f, TQ), lambda i, j: (i, j, 0, 0)),
                     pl.BlockSpec((1, f // LANES, n_keys, LANES), lambda i, j: (i, 0, key_block, 0)),
                     pl.BlockSpec((1, f_out, n_keys), lambda i, j: (i, 0, key_block))]),
        out_specs=pl.BlockSpec((1, tiles * TQ, f_out), lambda i, j: (i, j, 0)),
        out_shape=jax.ShapeDtypeStruct((b, n_qt * TQ, f_out), bf16),
        scratch_shapes=scratch,
        compiler_params=_cparams("parallel", "parallel"),
        name=name,
    )(*small, qt, k_all, vt_all)


def _mla_attn(q, k_all, vt_all, n_keys, key_block):
    return _attn_call(_mla_attn_kernel, "mla_attn", [], q, k_all, vt_all, n_keys, key_block, MLA_V)


def _diff_attn(lams, g, q, k_all, vt_all, n_keys, key_block, lam_init):
    return _attn_call(functools.partial(_diff_attn_kernel, lam_init=lam_init), "diff_attn",
                      list(lams) + [g], q, k_all, vt_all, n_keys, key_block, DF_V)


def _hy_filter_kernel(z_ref, w1_ref, b1_ref, fr_ref, w2_ref, b2_ref, w3_ref, b3_ref, win_ref,
                      fc_ref, fs_ref, a_ref, a2_ref, bq_ref, h_scr):
    kt = pl.program_id(1)
    hp = lax.Precision.HIGHEST

    @pl.when(kt == 0)
    def _():
        fr = fr_ref[0]
        a = jnp.sin(fr * (jnp.dot(z_ref[...], w1_ref[0], precision=hp, preferred_element_type=f32) + b1_ref[0]))
        a = jnp.sin(fr * (jnp.dot(a, w2_ref[0], precision=hp, preferred_element_type=f32) + b2_ref[0]))
        h = jnp.dot(a, w3_ref[0], precision=hp, preferred_element_type=f32) + b3_ref[0]
        win = win_ref[...]
        row = lax.broadcasted_iota(jnp.int32, win.shape, 0)
        h_scr[:, :HY_WIDTH] = (h[:, :HY_WIDTH] * win).astype(bf16)
        h_scr[:, HY_WIDTH:] = jnp.where(row == 0, 0.0, h[:, HY_WIDTH:] * win).astype(bf16)

    hh = h_scr[...]
    c = _dot(fc_ref[...], hh)
    s = _dot(fs_ref[...], hh)
    tk = c.shape[0]
    row = lax.broadcasted_iota(jnp.int32, (tk, HY_WIDTH), 0) + kt * tk
    first = row == 0
    a = c[:, :HY_WIDTH] + c[:, HY_WIDTH:]
    second = jnp.where(first, s[:, :HY_WIDTH] + s[:, HY_WIDTH:], s[:, :HY_WIDTH] - s[:, HY_WIDTH:])
    a_ref[0] = a
    a2_ref[0] = jnp.where(first, second, a)
    bq_ref[0] = jnp.where(first, 0.0, second)


def _hy_filter(z, hw, win, fmat):
    l = z.shape[0]
    depth = hw[0].shape[0]
    tk = min(512, l)
    nk = l // tk
    per_layer = lambda a: pl.BlockSpec((1,) + a.shape[1:], lambda d, k: (d,) + (0,) * (a.ndim - 1))
    out = jax.ShapeDtypeStruct((depth, l, HY_WIDTH), f32)
    return pl.pallas_call(
        _hy_filter_kernel,
        grid=(depth, nk),
        in_specs=([_const_spec(z.shape)] + [per_layer(a) for a in hw] + [_const_spec(win.shape)]
                  + [pl.BlockSpec((tk, l), lambda d, k: (k, 0)), pl.BlockSpec((tk, l), lambda d, k: (k + nk, 0))]),
        out_specs=[pl.BlockSpec((1, tk, HY_WIDTH), lambda d, k: (d, k, 0))] * 3,
        out_shape=[out] * 3,
        scratch_shapes=[pltpu.VMEM((l, 2 * HY_WIDTH), bf16)],
        compiler_params=_cparams("parallel", "arbitrary"),
        name="hy_filter",
    )(z, *hw, win, fmat, fmat)


def _hy_conv_kernel(p_ref, cw_ref, cb_ref, skip_ref, fc_ref, fs_ref, gc_ref, gs_ref,
                    a_ref, a2_ref, bq_ref, o_ref, u_scr, ub_scr, x0_scr, acc_scr, *, bb):
    kt = pl.program_id(1)
    l = p_ref.shape[1]

    @pl.when(kt == 0)
    def _():
        row = lax.broadcasted_iota(jnp.int32, (l, 3 * HY_WIDTH), 0)
        cw = cw_ref[...]
        for i in range(bb):
            p = p_ref[i].astype(f32)
            prev = jnp.where(row == 0, 0.0, pltpu.roll(p, 1, axis=0))
            nxt = jnp.where(row == l - 1, 0.0, pltpu.roll(p, l - 1, axis=0))
            uc = cw[0:1] * prev + cw[1:2] * p + cw[2:3] * nxt + cb_ref[...]
            x0_scr[i] = uc[:, :HY_WIDTH]
            u = uc[:, 2 * HY_WIDTH:] * uc[:, HY_WIDTH:2 * HY_WIDTH]
            u_scr[i] = u
            ub_scr[i] = u.astype(bf16)
            acc_scr[i] = jnp.zeros((l, HY_WIDTH), f32)

    a, a2, bq = a_ref[...], a2_ref[...], bq_ref[...]
    spectra = [(_dot(fc_ref[...], ub_scr[i]), _dot(fs_ref[...], ub_scr[i])) for i in range(bb)]
    for i, (ur, ui) in enumerate(spectra):
        zr = (ur * a - ui * bq).astype(bf16)
        zi = (ur * bq + ui * a2).astype(bf16)
        acc_scr[i] += _dot(gc_ref[...], zr) + _dot(gs_ref[...], zi)

    @pl.when(kt == pl.num_programs(1) - 1)
    def _():
        for i in range(bb):
            o_ref[i] = (x0_scr[i] * (acc_scr[i] + skip_ref[...] * u_scr[i])).astype(bf16)


def _hy_conv(p_hy, lw, spec, fmat, gmat):
    b, l, _ = p_hy.shape
    bb = 2
    tk = min(512, l)
    nk = l // tk
    a, a2, bq = spec
    return pl.pallas_call(
        functools.partial(_hy_conv_kernel, bb=bb),
        grid=(b // bb, nk),
        in_specs=[pl.BlockSpec((bb, l, 3 * HY_WIDTH), lambda i, k: (i, 0, 0)),
                  _const_spec(lw['hy_conv_w'].shape), _const_spec(lw['hy_conv_b'].shape),
                  _const_spec(lw['hy_skip'].shape),
                  pl.BlockSpec((tk, l), lambda i, k: (k, 0)), pl.BlockSpec((tk, l), lambda i, k: (k + nk, 0)),
                  pl.BlockSpec((l, tk), lambda i, k: (0, k)), pl.BlockSpec((l, tk), lambda i, k: (0, k + nk))]
                 + [pl.BlockSpec((tk, HY_WIDTH), lambda i, k: (k, 0))] * 3,
        out_specs=pl.BlockSpec((bb, l, HY_WIDTH), lambda i, k: (i, 0, 0)),
        out_shape=jax.ShapeDtypeStruct((b, l, HY_WIDTH), bf16),
        scratch_shapes=[pltpu.VMEM((bb, l, HY_WIDTH), f32), pltpu.VMEM((bb, l, HY_WIDTH), bf16),
                        pltpu.VMEM((bb, l, HY_WIDTH), f32), pltpu.VMEM((bb, l, HY_WIDTH), f32)],
        compiler_params=_cparams("parallel", "arbitrary"),
        name="hy_conv",
    )(p_hy, lw['hy_conv_w'], lw['hy_conv_b'], lw['hy_skip'], fmat, fmat, gmat, gmat, a, a2, bq)


def _merge_kernel(x_ref, mod_ref, g_ref, ya_ref, yb_ref, yc_ref, wg_ref, wa_ref, wb_ref, wc_ref, wo_ref,
                  o_ref):
    md = mod_ref[0]
    d = D_MODEL
    subs = [slice(TQ * t, TQ * (t + 1)) for t in range(x_ref.shape[1] // TQ)]
    staged = []
    for rows in subs:
        h = (_rms(x_ref[0, rows, :]) * g_ref[...] * (1.0 + md[1:2]) + md[0:1]).astype(bf16)
        gates = [_dot(h, wg_ref[:, d * i:d * (i + 1)]) for i in range(N_BRANCH)]
        branches = [_dot(ya_ref[0, rows, :], wa_ref[...]), _dot(yb_ref[0, rows, :], wb_ref[...]),
                    _dot(yc_ref[0, rows, :], wc_ref[...])]
        staged.append((gates, branches))
    for rows, (gates, branches) in zip(subs, staged):
        m = functools.reduce(jnp.add, [jax.nn.sigmoid(gt) * br for gt, br in zip(gates, branches)])
        o_ref[0, rows, :] = x_ref[0, rows, :] + md[2:3] * _dot(m.astype(bf16), wo_ref[...])


def _merge(x, mod, per_batch_mod, g, ya, yb, yc, lw):
    b, lx, d = x.shape
    tm = min(512, lx)
    tok = lambda n: pl.BlockSpec((1, tm, n), lambda i, j: (i, j, 0))
    mod_map = (lambda i, j: (i, 0, 0)) if per_batch_mod else (lambda i, j: (0, 0, 0))
    weights = [lw['w_gate'], lw['w_br_a'], lw['w_br_b'], lw['w_br_c'], lw['w_out']]
    return pl.pallas_call(
        _merge_kernel,
        grid=(b, lx // tm),
        in_specs=[tok(d), pl.BlockSpec((1, 6, d), mod_map), _const_spec(g.shape),
                  tok(MLA_OUT), tok(HY_WIDTH), tok(DF_OUT)] + [_const_spec(w.shape) for w in weights],
        out_specs=tok(d),
        out_shape=jax.ShapeDtypeStruct(x.shape, f32),
        compiler_params=_cparams("parallel", "parallel"),
        name="merge",
    )(x, mod, g, ya, yb, yc, *weights)


def _mlp_kernel(*refs, final):
    x_ref, mod_ref, g_ref, w1_ref, w2_ref = refs[:5]
    o_ref = refs[-1]
    md = mod_ref[0]
    subs = [slice(TQ * t, TQ * (t + 1)) for t in range(x_ref.shape[1] // TQ)]
    hidden = []
    for rows in subs:
        h = (_rms(x_ref[0, rows, :]) * g_ref[...] * (1.0 + md[4:5]) + md[3:4]).astype(bf16)
        hidden.append(_dot(h, w1_ref[...]))
    for rows, a in zip(subs, hidden):
        a = jnp.maximum(a, 0.0)
        y = x_ref[0, rows, :] + md[5:6] * _dot((a * a).astype(bf16), w2_ref[...])
        if final:
            y = _rms(y) * refs[5][...]
        o_ref[0, rows, :] = y


def _mlp(x, mod, per_batch_mod, g, w1, w2, final_g=None):
    b, lx, d = x.shape
    tm = min(512, lx)
    tok = pl.BlockSpec((1, tm, d), lambda i, j: (i, j, 0))
    mod_map = (lambda i, j: (i, 0, 0)) if per_batch_mod else (lambda i, j: (0, 0, 0))
    final = final_g is not None
    resident = lambda w: pl.BlockSpec(w.shape, lambda i, j: (0, 0), pipeline_mode=pl.Buffered(1))
    in_specs = [tok, pl.BlockSpec((1, 6, d), mod_map), _const_spec(g.shape), resident(w1), resident(w2)]
    args = [x, mod, g, w1, w2]
    if final:
        in_specs.append(_const_spec(final_g.shape))
        args.append(final_g)
    return pl.pallas_call(
        functools.partial(_mlp_kernel, final=final),
        grid=(b, lx // tm),
        in_specs=in_specs,
        out_specs=tok,
        out_shape=jax.ShapeDtypeStruct(x.shape, f32),
        compiler_params=_cparams("parallel", "parallel"),
        name="mlp",
    )(*args)


def _rope_tables(n_tokens):
    rows = n_tokens // GRID_W
    row = jnp.repeat(jnp.arange(rows), GRID_W).astype(f32)
    col = jnp.tile(jnp.arange(GRID_W), rows).astype(f32)
    nf = MLA_ROPE // 4
    inv = ROPE_BASE ** (-jnp.arange(nf, dtype=f32) / nf)
    ang = jnp.concatenate([row[:, None] * inv, col[:, None] * inv], axis=-1)
    cos, sin = jnp.cos(ang), jnp.sin(ang)
    one = jnp.ones((n_tokens, MLA_NOPE), f32)
    zero = jnp.zeros((n_tokens, MLA_NOPE), f32)
    pad = jnp.zeros((n_tokens, LANES - MLA_NOPE - MLA_ROPE), f32)
    cm = jnp.concatenate([one, cos, cos, pad], axis=-1)
    sm = jnp.concatenate([zero, -sin, sin, pad], axis=-1)
    kt = jnp.concatenate([cos, cos, -sin, sin, jnp.zeros((n_tokens, LANES - 2 * MLA_ROPE), f32)], axis=-1)
    cd = jnp.tile(jnp.concatenate([cos, cos], axis=-1), (1, LANES // DF_DIM))
    sd = jnp.tile(jnp.concatenate([-sin, sin], axis=-1), (1, LANES // DF_DIM))
    return cm, sm, kt, cd, sd


def _hy_positional(l):
    t = jnp.linspace(0.0, 1.0, l, dtype=f32)[:, None]
    w = (2.0 * math.pi / l) * jnp.arange(l, dtype=f32)[:, None]
    bands = jnp.linspace(1e-4, HY_BANDS - 1, HY_BANDS, dtype=f32)[None]
    z = jnp.concatenate([t, jnp.cos(bands * w), -jnp.sin(bands * w)], axis=-1)
    z = jnp.pad(z, ((0, 0), (0, LANES - HY_EMB)))
    deltas = jnp.linspace(math.log(HY_TARGET) / HY_SLOW, math.log(HY_TARGET) / HY_FAST, HY_WIDTH, dtype=f32)
    win = jnp.exp(-t * jnp.abs(deltas)) + HY_SHIFT
    return z, win


def _dft_tables(l):
    n2 = 2 * l
    n = jnp.arange(l, dtype=jnp.int32)[None, :]
    a = jnp.arange(l // GRID_W, dtype=jnp.int32)[:, None]
    b = jnp.arange(GRID_W, dtype=jnp.int32)[:, None]
    ang_a = ((GRID_W * a * n) % n2).astype(f32) * (2.0 * math.pi / n2)
    ang_b = ((b * n) % n2).astype(f32) * (2.0 * math.pi / n2)
    ca, sa, cb, sb = jnp.cos(ang_a), jnp.sin(ang_a), jnp.cos(ang_b), jnp.sin(ang_b)
    nyq = jnp.where(n[0] % 2 == 0, 1.0, -1.0)
    c = (ca[:, None, :] * cb[None] - sa[:, None, :] * sb[None]).reshape(l, l)
    s = -(sa[:, None, :] * cb[None] + ca[:, None, :] * sb[None]).reshape(l, l)
    fmat = jnp.concatenate([c, s.at[0].set(nyq)], axis=0)
    wgt = jnp.full((l,), 2.0 / n2, f32).at[0].set(1.0 / n2)
    cat, sat, cbt, sbt = ca.T, sa.T, cb.T, sb.T
    ct = (cat[:, :, None] * cbt[:, None, :] - sat[:, :, None] * sbt[:, None, :]).reshape(l, l)
    st = -(sat[:, :, None] * cbt[:, None, :] + cat[:, :, None] * sbt[:, None, :]).reshape(l, l)
    gmat = jnp.concatenate([ct * wgt[None], st.at[:, 0].set(nyq) * wgt[None]], axis=1)
    return fmat.astype(bf16), gmat.astype(bf16)


def _prep_weights(w_in, mla_q_norm_g, mla_w_uq, mla_kv_norm_g, mla_w_ukv, hy_conv_w, hy_conv_b, hy_w1, hy_b1,
                  hy_freq, hy_w2, hy_b2, hy_w3, hy_b3, hy_skip, df_subln_g, w_br_a, w_br_b, w_br_c, w_out,
                  w_fc1, w_fc2):
    depth = w_in.shape[0]
    o = [0]
    for n in (MLA_Q_RANK, MLA_KV_RANK, MLA_ROPE, 3 * HY_WIDTH, 2 * DF_HEADS * DF_DIM, 2 * DF_HEADS * DF_DIM,
              DF_OUT, N_BRANCH * D_MODEL):
        o.append(o[-1] + n)
    half = MLA_ROPE // 2
    w_kr = w_in[..., o[2]:o[3]]
    kr_pad = jnp.zeros((depth, D_MODEL, LANES - 2 * MLA_ROPE), f32)
    w_in1 = jnp.concatenate([w_in[..., o[0]:o[3]], w_kr[..., half:], w_kr[..., :half], kr_pad,
                             w_in[..., o[3]:o[7]]], axis=-1).astype(bf16)
    w_gate = w_in[..., o[7]:o[8]].astype(bf16)

    uq = mla_w_uq.reshape(depth, MLA_Q_RANK, MLA_HEADS, MLA_NOPE + MLA_ROPE)
    nope, rope = uq[..., :MLA_NOPE], uq[..., MLA_NOPE:]
    zn = jnp.zeros_like(nope)
    zp = jnp.zeros(uq.shape[:3] + (LANES - MLA_NOPE - MLA_ROPE,), f32)
    wq = jnp.concatenate([nope, rope, zp], axis=-1).reshape(depth, MLA_Q_RANK, MLA_HEADS * LANES)
    wqs = jnp.concatenate([zn, rope[..., half:], rope[..., :half], zp], axis=-1)
    wqs = wqs.reshape(depth, MLA_Q_RANK, MLA_HEADS * LANES)
    ukv = mla_w_ukv.reshape(depth, MLA_KV_RANK, MLA_HEADS, MLA_NOPE + MLA_V)
    wk = jnp.concatenate([ukv[..., :MLA_NOPE], jnp.zeros_like(ukv[..., MLA_NOPE:])], axis=-1)
    wk = wk.reshape(depth, MLA_KV_RANK, MLA_HEADS * LANES)
    wv = ukv[..., MLA_NOPE:].reshape(depth, MLA_KV_RANK, MLA_OUT)

    i = jnp.arange(LANES)[:, None]
    js = jnp.arange(MLA_HEADS * LANES)[None, :] % LANES - MLA_NOPE
    e = ((i < MLA_ROPE) & (js == i)).astype(f32)
    e_rope = ((i < 2 * MLA_ROPE) & (js == i % MLA_ROPE)).astype(f32)
    wke = jnp.concatenate([wk, jnp.broadcast_to(e, (depth,) + e.shape)], axis=1).astype(bf16)
    wke_rope = jnp.concatenate([wk, jnp.broadcast_to(e_rope, (depth,) + e.shape)], axis=1).astype(bf16)
    a = jnp.arange(2 * DF_HEADS * DF_DIM)
    dh = DF_DIM // 2
    swap = (a // DF_DIM) * DF_DIM + (a % DF_DIM + dh) % DF_DIM
    p = (a[:, None] == swap[None, :]).astype(bf16)

    pad_rows = lambda w, r: jnp.pad(w, ((0, 0), (0, r - w.shape[1]), (0, 0)))
    pad_cols = lambda w, c: jnp.pad(w, ((0, 0),) * (w.ndim - 1) + ((0, c - w.shape[-1]),))
    layers = []
    for l in range(depth):
        layers.append({
            'w_in1': w_in1[l], 'w_gate': w_gate[l],
            'q_norm_g': mla_q_norm_g[l][None], 'kv_norm_g': mla_kv_norm_g[l][None],
            'wq': wq[l].astype(bf16), 'wqs': wqs[l].astype(bf16), 'wke': wke[l], 'wke_rope': wke_rope[l],
            'wv': wv[l].astype(bf16), 'p': p,
            'hy_conv_w': hy_conv_w[l], 'hy_conv_b': hy_conv_b[l][None], 'hy_skip': hy_skip[l][None],
            'subln_g': jnp.broadcast_to(df_subln_g[l][:, None], (DF_V, LANES)),
            'w_br_a': w_br_a[l].astype(bf16), 'w_br_b': w_br_b[l].astype(bf16),
            'w_br_c': w_br_c[l].astype(bf16), 'w_out': w_out[l].astype(bf16),
            'w_fc1': w_fc1[l].astype(bf16), 'w_fc2': w_fc2[l].astype(bf16),
        })
    hy_filter_w = [pad_cols(pad_rows(hy_w1, LANES), LANES), pad_cols(hy_b1, LANES)[:, None],
                   pad_cols(hy_freq, LANES)[:, None], pad_cols(pad_rows(hy_w2, LANES), LANES),
                   pad_cols(hy_b2, LANES)[:, None], pad_rows(hy_w3, LANES), hy_b3[:, None]]
    return layers, hy_filter_w


def _token_mixers(lat, ctx, n_lat, n_ctx, lw, lams, lam_init, hy_lat, hy_ctx, with_ctx):
    qm, _, _, qd, _, _, hy = lat
    cqm, km, vm, cqd, kd, vd, chy = ctx
    n_all = n_lat + n_ctx
    ya = _mla_attn(qm, km, vm, n_all, 0)
    yc = _diff_attn(lams, lw['subln_g'], qd, kd, vd, n_all, 0, lam_init)
    yb = _hy_conv(hy, lw, *hy_lat)
    if not with_ctx:
        return (ya, yb, yc), None
    ya_c = _mla_attn(cqm, km, vm, n_ctx, n_lat // n_ctx)
    yc_c = _diff_attn(lams, lw['subln_g'], cqd, kd, vd, n_ctx, n_lat // n_ctx, lam_init)
    yb_c = _hy_conv(chy, lw, *hy_ctx)
    return (ya, yb, yc), (ya_c, yb_c, yc_c)


def kernel(x, c, ctx, c_ctx, norm_mix_g, norm_ffn_g, w_mod, b_mod, w_in, mla_q_norm_g, mla_w_uq, mla_kv_norm_g, mla_w_ukv, hy_conv_w, hy_conv_b, hy_w1, hy_b1, hy_freq, hy_w2, hy_b2, hy_w3, hy_b3, hy_skip, df_lq1, df_lk1, df_lq2, df_lk2, df_subln_g, w_br_a, w_br_b, w_br_c, w_out, w_fc1, w_fc2, final_norm_g):
    b, n_lat, d = x.shape
    n_ctx = ctx.shape[1]
    assert n_lat % n_ctx == 0 and n_ctx % TQ == 0
    depth = w_in.shape[0]
    layers, hy_filter_w = _prep_weights(w_in, mla_q_norm_g, mla_w_uq, mla_kv_norm_g, mla_w_ukv, hy_conv_w, hy_conv_b,
                           hy_w1, hy_b1, hy_freq, hy_w2, hy_b2, hy_w3, hy_b3, hy_skip, df_subln_g,
                           w_br_a, w_br_b, w_br_c, w_out, w_fc1, w_fc2)
    tables = _rope_tables(n_lat)
    z_lat, win_lat = _hy_positional(n_lat)
    z_ctx, win_ctx = _hy_positional(n_ctx)
    f_lat, g_lat = _dft_tables(n_lat)
    f_ctx, g_ctx = _dft_tables(n_ctx)
    spec_lat = _hy_filter(z_lat, hy_filter_w, win_lat, f_lat)
    spec_ctx = _hy_filter(z_ctx, hy_filter_w, win_ctx, f_ctx)

    rows = -(-(b + 1) // 8) * 8
    cc = jnp.concatenate([c, c_ctx[None], jnp.zeros((rows - b - 1, d), f32)], axis=0)
    mod = _modulation(cc, w_mod, b_mod).reshape(depth, rows, 6, d)

    xc = ctx
    for l in range(depth):
        lw = layers[l]
        with_ctx = l < depth - 1
        lam_init = 0.8 - 0.6 * math.exp(-0.3 * l)
        mod_lat, mod_ctx = mod[l, :b], mod[l, b:b + 1]
        g_mix, g_ffn = norm_mix_g[l][None], norm_ffn_g[l][None]
        lams = [df_lq1[l][None], df_lk1[l][None], df_lq2[l][None], df_lk2[l][None]]

        hy_lat = ([a[l] for a in spec_lat], f_lat, g_lat)
        hy_ctx = ([a[l] for a in spec_ctx], f_ctx, g_ctx) if with_ctx else None

        n_all = n_lat + n_ctx
        parts = _in_proj(x, mod_lat, True, g_mix, lw, tables, n_all, 0)
        kv_bufs = [parts[i] for i in (1, 2, 4, 5)]
        parts_c = _in_proj(xc, mod_ctx, False, g_mix, lw, None, n_all, n_lat // n_ctx, kv_bufs)
        ys, ys_c = _token_mixers(parts, parts_c, n_lat, n_ctx, lw, lams, lam_init, hy_lat, hy_ctx, with_ctx)

        x = _merge(x, mod_lat, True, g_mix, *ys, lw)
        x = _mlp(x, mod_lat, True, g_ffn, lw['w_fc1'], lw['w_fc2'],
                 final_norm_g[None] if l == depth - 1 else None)
        if with_ctx:
            xc = _merge(xc, mod_ctx, False, g_mix, *ys_c, lw)
            xc = _mlp(xc, mod_ctx, False, g_ffn, lw['w_fc1'], lw['w_fc2'])
    return x
```

```python
import functools
import math

import jax
import jax.numpy as jnp
from jax import lax
from jax.experimental import pallas as pl
from jax.experimental.pallas import tpu as pltpu

f32 = jnp.float32
bf16 = jnp.bfloat16

D_MODEL = 1024
GRID_W = 64
EPS = 1e-6
ROPE_BASE = 10000.0

MLA_HEADS = 8
MLA_NOPE = 64
MLA_ROPE = 32
MLA_V = 64
MLA_Q_RANK = 256
MLA_KV_RANK = 128
MLA_OUT = MLA_HEADS * MLA_V
MLA_SCALE = (MLA_NOPE + MLA_ROPE) ** -0.5

HY_WIDTH = 256
HY_EMB = 33
HY_BANDS = (HY_EMB - 1) // 2
HY_TARGET = 1e-2
HY_FAST = 0.3
HY_SLOW = 1.5
HY_SHIFT = 0.05

DF_HEADS = 4
DF_DIM = 32
DF_V = 2 * DF_DIM
DF_OUT = DF_HEADS * DF_V
DF_SCALE = DF_DIM ** -0.5

N_BRANCH = 3

LANES = 128
BF16_ROWS = 16
MXU_TILE = 256
KEY_CHUNK = 1152
TQ = MXU_TILE
MAPS_PER_TILE = 8
LOG2E = 1.4426950408889634
VMEM_LIMIT = 56 * 1024 * 1024

C_Q, C_KV, C_KR, C_HY, C_DQ, C_DK, C_DV, C_END = 0, 256, 384, 512, 1280, 1536, 1792, 2048


def _cparams(*sem):
    return pltpu.CompilerParams(dimension_semantics=sem, vmem_limit_bytes=VMEM_LIMIT)


def _dot(a, b):
    return jnp.dot(a, b, preferred_element_type=f32)


def _rms(x):
    return x * lax.rsqrt(jnp.mean(x * x, axis=-1, keepdims=True) + EPS)


def _shr(n, k):
    return n >> k if isinstance(n, int) else lax.shift_right_logical(n, k)


def _aligned(x, m):
    return x if isinstance(x, int) else pl.multiple_of(x, m)


def _const_spec(shape):
    nd = len(shape)
    return pl.BlockSpec(shape, lambda *_: (0,) * nd)


def _mod_kernel(c_ref, w_ref, b_ref, o_ref):
    c = c_ref[...]
    s = c * jax.nn.sigmoid(c)
    o_ref[0] = _dot(s.astype(bf16), w_ref[0]) + b_ref[0]


def _modulation(cc, w_mod, b_mod):
    depth, d, n = w_mod.shape
    r = cc.shape[0]
    tn = 2048
    return pl.pallas_call(
        _mod_kernel,
        grid=(depth, n // tn),
        in_specs=[pl.BlockSpec((r, d), lambda l, j: (0, 0)),
                  pl.BlockSpec((1, d, tn), lambda l, j: (l, 0, j)),
                  pl.BlockSpec((1, 1, tn), lambda l, j: (l, 0, j))],
        out_specs=pl.BlockSpec((1, r, tn), lambda l, j: (l, 0, j)),
        out_shape=jax.ShapeDtypeStruct((depth, r, n), f32),
        compiler_params=_cparams("parallel", "parallel"),
        name="modulation",
    )(cc, w_mod.astype(bf16), b_mod.reshape(depth, 1, n))


def _in_proj_kernel(*refs, rope, n_alias):
    (x_ref, mod_ref, g_ref, w_ref, qg_ref, kvg_ref, wq_ref, wqs_ref, wke_ref, wv_ref, p_ref) = refs[:11]
    refs = refs[11:]
    if rope:
        cm_ref, sm_ref, kt_ref, cd_ref, sd_ref = refs[:5]
        refs = refs[5:]
    qm_ref, km_ref, vm_ref, qd_ref, kd_ref, vd_ref, hy_ref = refs[n_alias:]

    md = mod_ref[0]
    subs = [slice(TQ * t, TQ * (t + 1)) for t in range(x_ref.shape[1] // TQ)]
    projected = []
    for rows in subs:
        h = _rms(x_ref[0, rows, :]) * g_ref[...] * (1.0 + md[1:2]) + md[0:1]
        projected.append(_dot(h.astype(bf16), w_ref[...]))

    for t, (rows, p) in enumerate(zip(subs, projected)):
        qn = (_rms(p[:, C_Q:C_KV]) * qg_ref[...]).astype(bf16)
        kvn = (_rms(p[:, C_KV:C_KR]) * kvg_ref[...]).astype(bf16)
        kr = p[:, C_KR:C_HY]
        q = _dot(qn, wq_ref[...])
        dq = p[:, C_DQ:C_DK]
        dk = p[:, C_DK:C_DV]
        if rope:
            kr = kr * kt_ref[rows, :]
            cm = jnp.concatenate([cm_ref[rows, :]] * MLA_HEADS, axis=-1)
            sm = jnp.concatenate([sm_ref[rows, :]] * MLA_HEADS, axis=-1)
            q = q * cm + _dot(qn, wqs_ref[...]) * sm
            cd = jnp.concatenate([cd_ref[rows, :]] * 2, axis=-1)
            sd = jnp.concatenate([sd_ref[rows, :]] * 2, axis=-1)
            dq = dq * cd + _dot(dq.astype(bf16), p_ref[...]) * sd
            dk = dk * cd + _dot(dk.astype(bf16), p_ref[...]) * sd
        k = _dot(jnp.concatenate([kvn, kr.astype(bf16)], axis=-1), wke_ref[...]).astype(bf16)
        dk = dk.astype(bf16)
        qm_ref[0, t] = (q * (MLA_SCALE * LOG2E)).T.astype(bf16)
        for h in range(km_ref.shape[1]):
            km_ref[0, h, rows, :] = k[:, LANES * h:LANES * (h + 1)]
        vm_ref[0, :, rows] = _dot(kvn, wv_ref[...]).T.astype(bf16)
        qd_ref[0, t] = (dq * (DF_SCALE * LOG2E)).T.astype(bf16)
        for h in range(kd_ref.shape[1]):
            kd_ref[0, h, rows, :] = dk[:, LANES * h:LANES * (h + 1)]
        vd_ref[0, :, rows] = p[:, C_DV:C_END].T.astype(bf16)
        hy_ref[0, rows, :] = p[:, C_HY:C_DQ].astype(bf16)


def _in_proj(x, mod, per_batch_mod, g, lw, tables, n_keys, key_block, kv_bufs=None):
    b, lx, d = x.shape
    tm = min(512, lx)
    rope = tables is not None
    kb = key_block * (lx // tm)
    tok = lambda n: pl.BlockSpec((1, tm, n), lambda i, j: (j, i, 0))
    mod_map = (lambda i, j: (j, 0, 0)) if per_batch_mod else (lambda i, j: (0, 0, 0))
    weights = [lw['w_in1'], lw['q_norm_g'], lw['kv_norm_g'], lw['wq'], lw['wqs'],
               lw['wke_rope' if rope else 'wke'], lw['wv'], lw['p']]
    in_specs = ([tok(d), pl.BlockSpec((1, 6, d), mod_map), _const_spec(g.shape)]
                + [_const_spec(w.shape) for w in weights])
    args = [x, mod, g] + weights
    if rope:
        in_specs += [pl.BlockSpec((tm, LANES), lambda i, j: (i, 0))] * len(tables)
        args += list(tables)
    aliases = {}
    if kv_bufs is not None:
        aliases = {len(args) + n: o for n, o in enumerate((1, 2, 4, 5))}
        in_specs += [pl.BlockSpec(memory_space=pl.ANY)] * len(kv_bufs)
        args += list(kv_bufs)
    f_q, f_d = MLA_HEADS * LANES, 2 * DF_HEADS * DF_DIM
    qtile = lambda n: pl.BlockSpec((1, tm // TQ, n, TQ), lambda i, j: (j, i, 0, 0))
    keys = lambda n: pl.BlockSpec((1, n // LANES, tm, LANES), lambda i, j: (j, 0, kb + i, 0))
    vals = lambda n: pl.BlockSpec((1, n, tm), lambda i, j: (j, 0, kb + i))
    sds = lambda *shape: jax.ShapeDtypeStruct(shape, bf16)
    out_specs = [qtile(f_q), keys(f_q), vals(MLA_OUT), qtile(f_d), keys(f_d), vals(DF_OUT), tok(3 * HY_WIDTH)]
    out_shape = [sds(b, lx // TQ, f_q, TQ), sds(b, f_q // LANES, n_keys, LANES), sds(b, MLA_OUT, n_keys),
                 sds(b, lx // TQ, f_d, TQ), sds(b, f_d // LANES, n_keys, LANES), sds(b, DF_OUT, n_keys),
                 sds(b, lx, 3 * HY_WIDTH)]
    return pl.pallas_call(
        functools.partial(_in_proj_kernel, rope=rope, n_alias=len(aliases)),
        grid=(lx // tm, b),
        in_specs=in_specs,
        out_specs=out_specs,
        out_shape=out_shape,
        input_output_aliases=aliases,
        compiler_params=_cparams("parallel", "parallel"),
        name="in_proj",
    )(*args)


def _softmax_maps(n_maps, qt_of, k_of, vt_of, s_scr, o_scr):
    _, lk, tq = s_scr.shape
    dv = o_scr.shape[0] // n_maps
    kc = KEY_CHUNK if lk % KEY_CHUNK == 0 else MXU_TILE
    assert lk % kc == 0
    ones = jnp.ones((BF16_ROWS, kc), bf16)

    def scores(n, slot):
        qt = qt_of(n)
        pm = None
        for off in range(0, lk, kc):
            s = _dot(k_of(n, off, kc), qt)
            s_scr[slot, off:off + kc, :] = s
            cm = jnp.max(s.reshape(kc // 8, 8, tq), axis=0)
            pm = cm if pm is None else jnp.maximum(pm, cm)
        return jnp.max(pm, axis=0, keepdims=True)

    def values(n, slot, m):
        acc = None
        for off in range(0, lk, kc):
            p = jnp.exp2(s_scr[slot, off:off + kc, :] - m).astype(bf16)
            part = _dot(jnp.concatenate([vt_of(n, off, kc), ones], axis=0), p)
            acc = part if acc is None else acc + part
        o_scr[pl.ds(_aligned(n * dv, dv), dv), :] = acc[:dv] / acc[dv:dv + 1]

    if lk == kc and n_maps <= MAPS_PER_TILE:
        qks = [_dot(k_of(n, 0, kc), qt_of(n)) for n in range(n_maps)]
        col_max = lambda s: jnp.max(jnp.max(s.reshape(kc // 8, 8, tq), axis=0), axis=0, keepdims=True)
        pts = [jnp.exp2(s - col_max(s)).astype(bf16) for s in qks]
        for n, p in enumerate(pts):
            acc = _dot(jnp.concatenate([vt_of(n, 0, kc), ones], axis=0), p)
            o_scr[n * dv:(n + 1) * dv, :] = acc[:dv] / acc[dv:dv + 1]
        return

    def group(base, m, count):
        for e in range(count):
            m_next = scores(base + e + 1, (e + 1) & 1)
            values(base + e, e & 1, m)
            m = m_next
        return m

    per_iter = 8 if n_maps >= 64 else (4 if n_maps >= 16 else 2)
    trips = (n_maps - 1) // per_iter
    m = lax.fori_loop(0, trips, lambda i, m: group(i * per_iter, m, per_iter), scores(0, 0))
    m = group(trips * per_iter, m, n_maps - 1 - trips * per_iter)
    values(n_maps - 1, (n_maps - 1) & 1, m)


def _mla_attn_kernel(qt_ref, k_ref, vt_ref, o_ref, s_scr, o_scr):
    n_tiles = qt_ref.shape[1]
    tile_of = lambda n: _shr(n, 3)
    head_of = lambda n: n & (MAPS_PER_TILE - 1)
    _softmax_maps(
        n_tiles * MLA_HEADS,
        lambda n: qt_ref[0, tile_of(n), pl.ds(_aligned(head_of(n) * LANES, LANES), LANES), :],
        lambda n, off, kc: k_ref[0, head_of(n), off:off + kc, :],
        lambda n, off, kc: vt_ref[0, pl.ds(_aligned(head_of(n) * MLA_V, MLA_V), MLA_V), off:off + kc],
        s_scr, o_scr)
    for t in range(n_tiles):
        for j in range(MLA_OUT // LANES):
            r = t * MLA_OUT + LANES * j
            o_ref[0, TQ * t:TQ * (t + 1), LANES * j:LANES * (j + 1)] = o_scr[r:r + LANES, :].T.astype(bf16)


def _diff_attn_kernel(lq1, lk1, lq2, lk2, g_ref, qt_ref, k_ref, vt_ref, o_ref, s_scr, o_scr, *, lam_init):
    n_tiles = qt_ref.shape[1]
    row = lax.broadcasted_iota(jnp.int32, (LANES, TQ), 0)
    per_lanes = LANES // DF_DIM
    tile_of = lambda n: _shr(n, 3)
    map_of = lambda n: n & (MAPS_PER_TILE - 1)
    lanes_of = lambda n: _shr(map_of(n), 2)

    def qt_of(n):
        lo = DF_DIM * (n & (per_lanes - 1))
        qt = qt_ref[0, tile_of(n), pl.ds(_aligned(lanes_of(n) * LANES, LANES), LANES), :]
        return jnp.where(row < lo, 0.0, jnp.where(row < lo + DF_DIM, qt, 0.0)).astype(bf16)

    def vt_of(n, off, kc):
        h = _shr(map_of(n), 1)
        return vt_ref[0, pl.ds(_aligned(h * DF_V, DF_V), DF_V), off:off + kc]

    _softmax_maps(n_tiles * 2 * DF_HEADS, qt_of, lambda n, off, kc: k_ref[0, lanes_of(n), off:off + kc, :], vt_of,
                  s_scr, o_scr)

    lam = (jnp.exp(jnp.sum(lq1[...] * lk1[...], axis=-1, keepdims=True))
           - jnp.exp(jnp.sum(lq2[...] * lk2[...], axis=-1, keepdims=True)) + lam_init)
    g = jnp.concatenate([g_ref[...]] * (TQ // LANES), axis=-1)
    for t in range(n_tiles):
        for j in range(DF_OUT // LANES):
            pair = []
            for h in (2 * j, 2 * j + 1):
                r = (t * MAPS_PER_TILE + 2 * h) * DF_V
                o = o_scr[r:r + DF_V, :] - lam * o_scr[r + DF_V:r + 2 * DF_V, :]
                ms = jnp.mean(o * o, axis=0, keepdims=True)
                pair.append(o * lax.rsqrt(ms + EPS) * g * (1.0 - lam_init))
            o_ref[0, TQ * t:TQ * (t + 1), LANES * j:LANES * (j + 1)] = (
                jnp.concatenate(pair, axis=0).T.astype(bf16))


def _attn_call(kernel, name, small, qt, k_all, vt_all, n_keys, key_block, dv):
    b, n_qt, f, _ = qt.shape
    f_out = vt_all.shape[1]
    tiles = next(t for t in (8, 4, 2, 1) if n_qt % t == 0)
    scratch = [pltpu.VMEM((2, n_keys, TQ), f32), pltpu.VMEM((tiles * MAPS_PER_TILE * dv, TQ), f32)]
    return pl.pallas_call(
        kernel,
        grid=(b, n_qt // tiles),
        in_specs=([_const_spec(a.shape) for a in small]
                  + [pl.BlockSpec((1, tiles, f, TQ), lambda i, j: (i, j, 0, 0)),
                     pl.BlockSpec((1, f // LANES, n_keys, LANES), lambda i, j: (i, 0, key_block, 0)),
                     pl.BlockSpec((1, f_out, n_keys), lambda i, j: (i, 0, key_block))]),
        out_specs=pl.BlockSpec((1, tiles * TQ, f_out), lambda i, j: (i, j, 0)),
        out_shape=jax.ShapeDtypeStruct((b, n_qt * TQ, f_out), bf16),
        scratch_shapes=scratch,
        compiler_params=_cparams("parallel", "parallel"),
        name=name,
    )(*small, qt, k_all, vt_all)


def _mla_attn(q, k_all, vt_all, n_keys, key_block):
    return _attn_call(_mla_attn_kernel, "mla_attn", [], q, k_all, vt_all, n_keys, key_block, MLA_V)


def _diff_attn(lams, g, q, k_all, vt_all, n_keys, key_block, lam_init):
    return _attn_call(functools.partial(_diff_attn_kernel, lam_init=lam_init), "diff_attn",
                      list(lams) + [g], q, k_all, vt_all, n_keys, key_block, DF_V)


def _hy_filter_kernel(z_ref, w1_ref, b1_ref, fr_ref, w2_ref, b2_ref, w3_ref, b3_ref, win_ref,
                      fc_ref, fs_ref, a_ref, a2_ref, bq_ref, h_scr):
    kt = pl.program_id(1)
    hp = lax.Precision.HIGHEST

    @pl.when(kt == 0)
    def _():
        fr = fr_ref[0]
        a = jnp.sin(fr * (jnp.dot(z_ref[...], w1_ref[0], precision=hp, preferred_element_type=f32) + b1_ref[0]))
        a = jnp.sin(fr * (jnp.dot(a, w2_ref[0], precision=hp, preferred_element_type=f32) + b2_ref[0]))
        h = jnp.dot(a, w3_ref[0], precision=hp, preferred_element_type=f32) + b3_ref[0]
        win = win_ref[...]
        row = lax.broadcasted_iota(jnp.int32, win.shape, 0)
        h_scr[:, :HY_WIDTH] = (h[:, :HY_WIDTH] * win).astype(bf16)
        h_scr[:, HY_WIDTH:] = jnp.where(row == 0, 0.0, h[:, HY_WIDTH:] * win).astype(bf16)

    hh = h_scr[...]
    c = _dot(fc_ref[...], hh)
    s = _dot(fs_ref[...], hh)
    tk = c.shape[0]
    row = lax.broadcasted_iota(jnp.int32, (tk, HY_WIDTH), 0) + kt * tk
    first = row == 0
    a = c[:, :HY_WIDTH] + c[:, HY_WIDTH:]
    second = jnp.where(first, s[:, :HY_WIDTH] + s[:, HY_WIDTH:], s[:, :HY_WIDTH] - s[:, HY_WIDTH:])
    a_ref[0] = a
    a2_ref[0] = jnp.where(first, second, a)
    bq_ref[0] = jnp.where(first, 0.0, second)


def _hy_filter(z, hw, win, fmat):
    l = z.shape[0]
    depth = hw[0].shape[0]
    tk = min(512, l)
    nk = l // tk
    per_layer = lambda a: pl.BlockSpec((1,) + a.shape[1:], lambda d, k: (d,) + (0,) * (a.ndim - 1))
    out = jax.ShapeDtypeStruct((depth, l, HY_WIDTH), f32)
    return pl.pallas_call(
        _hy_filter_kernel,
        grid=(depth, nk),
        in_specs=([_const_spec(z.shape)] + [per_layer(a) for a in hw] + [_const_spec(win.shape)]
                  + [pl.BlockSpec((tk, l), lambda d, k: (k, 0)), pl.BlockSpec((tk, l), lambda d, k: (k + nk, 0))]),
        out_specs=[pl.BlockSpec((1, tk, HY_WIDTH), lambda d, k: (d, k, 0))] * 3,
        out_shape=[out] * 3,
        scratch_shapes=[pltpu.VMEM((l, 2 * HY_WIDTH), bf16)],
        compiler_params=_cparams("parallel", "arbitrary"),
        name="hy_filter",
    )(z, *hw, win, fmat, fmat)


def _hy_conv_kernel(p_ref, cw_ref, cb_ref, skip_ref, fc_ref, fs_ref, gc_ref, gs_ref,
                    a_ref, a2_ref, bq_ref, o_ref, u_scr, ub_scr, x0_scr, acc_scr, *, bb):
    kt = pl.program_id(1)
    l = p_ref.shape[1]

    @pl.when(kt == 0)
    def _():
        row = lax.broadcasted_iota(jnp.int32, (l, 3 * HY_WIDTH), 0)
        cw = cw_ref[...]
        for i in range(bb):
            p = p_ref[i].astype(f32)
            prev = jnp.where(row == 0, 0.0, pltpu.roll(p, 1, axis=0))
            nxt = jnp.where(row == l - 1, 0.0, pltpu.roll(p, l - 1, axis=0))
            uc = cw[0:1] * prev + cw[1:2] * p + cw[2:3] * nxt + cb_ref[...]
            x0_scr[i] = uc[:, :HY_WIDTH]
            u = uc[:, 2 * HY_WIDTH:] * uc[:, HY_WIDTH:2 * HY_WIDTH]
            u_scr[i] = u
            ub_scr[i] = u.astype(bf16)
            acc_scr[i] = jnp.zeros((l, HY_WIDTH), f32)

    a, a2, bq = a_ref[...], a2_ref[...], bq_ref[...]
    spectra = [(_dot(fc_ref[...], ub_scr[i]), _dot(fs_ref[...], ub_scr[i])) for i in range(bb)]
    for i, (ur, ui) in enumerate(spectra):
        zr = (ur * a - ui * bq).astype(bf16)
        zi = (ur * bq + ui * a2).astype(bf16)
        acc_scr[i] += _dot(gc_ref[...], zr) + _dot(gs_ref[...], zi)

    @pl.when(kt == pl.num_programs(1) - 1)
    def _():
        for i in range(bb):
            o_ref[i] = (x0_scr[i] * (acc_scr[i] + skip_ref[...] * u_scr[i])).astype(bf16)


def _hy_conv(p_hy, lw, spec, fmat, gmat):
    b, l, _ = p_hy.shape
    bb = 2
    tk = min(512, l)
    nk = l // tk
    a, a2, bq = spec
    return pl.pallas_call(
        functools.partial(_hy_conv_kernel, bb=bb),
        grid=(b // bb, nk),
        in_specs=[pl.BlockSpec((bb, l, 3 * HY_WIDTH), lambda i, k: (i, 0, 0)),
                  _const_spec(lw['hy_conv_w'].shape), _const_spec(lw['hy_conv_b'].shape),
                  _const_spec(lw['hy_skip'].shape),
                  pl.BlockSpec((tk, l), lambda i, k: (k, 0)), pl.BlockSpec((tk, l), lambda i, k: (k + nk, 0)),
                  pl.BlockSpec((l, tk), lambda i, k: (0, k)), pl.BlockSpec((l, tk), lambda i, k: (0, k + nk))]
                 + [pl.BlockSpec((tk, HY_WIDTH), lambda i, k: (k, 0))] * 3,
        out_specs=pl.BlockSpec((bb, l, HY_WIDTH), lambda i, k: (i, 0, 0)),
        out_shape=jax.ShapeDtypeStruct((b, l, HY_WIDTH), bf16),
        scratch_shapes=[pltpu.VMEM((bb, l, HY_WIDTH), f32), pltpu.VMEM((bb, l, HY_WIDTH), bf16),
                        pltpu.VMEM((bb, l, HY_WIDTH), f32), pltpu.VMEM((bb, l, HY_WIDTH), f32)],
        compiler_params=_cparams("parallel", "arbitrary"),
        name="hy_conv",
    )(p_hy, lw['hy_conv_w'], lw['hy_conv_b'], lw['hy_skip'], fmat, fmat, gmat, gmat, a, a2, bq)


def _merge_kernel(x_ref, mod_ref, g_ref, ya_ref, yb_ref, yc_ref, wg_ref, wa_ref, wb_ref, wc_ref, wo_ref,
                  o_ref):
    md = mod_ref[0]
    d = D_MODEL
    subs = [slice(TQ * t, TQ * (t + 1)) for t in range(x_ref.shape[1] // TQ)]
    staged = []
    for rows in subs:
        h = (_rms(x_ref[0, rows, :]) * g_ref[...] * (1.0 + md[1:2]) + md[0:1]).astype(bf16)
        gates = [_dot(h, wg_ref[:, d * i:d * (i + 1)]) for i in range(N_BRANCH)]
        branches = [_dot(ya_ref[0, rows, :], wa_ref[...]), _dot(yb_ref[0, rows, :], wb_ref[...]),
                    _dot(yc_ref[0, rows, :], wc_ref[...])]
        staged.append((gates, branches))
    for rows, (gates, branches) in zip(subs, staged):
        m = functools.reduce(jnp.add, [jax.nn.sigmoid(gt) * br for gt, br in zip(gates, branches)])
        o_ref[0, rows, :] = x_ref[0, rows, :] + md[2:3] * _dot(m.astype(bf16), wo_ref[...])


def _merge(x, mod, per_batch_mod, g, ya, yb, yc, lw):
    b, lx, d = x.shape
    tm = min(512, lx)
    tok = lambda n: pl.BlockSpec((1, tm, n), lambda i, j: (i, j, 0))
    mod_map = (lambda i, j: (i, 0, 0)) if per_batch_mod else (lambda i, j: (0, 0, 0))
    weights = [lw['w_gate'], lw['w_br_a'], lw['w_br_b'], lw['w_br_c'], lw['w_out']]
    return pl.pallas_call(
        _merge_kernel,
        grid=(b, lx // tm),
        in_specs=[tok(d), pl.BlockSpec((1, 6, d), mod_map), _const_spec(g.shape),
                  tok(MLA_OUT), tok(HY_WIDTH), tok(DF_OUT)] + [_const_spec(w.shape) for w in weights],
        out_specs=tok(d),
        out_shape=jax.ShapeDtypeStruct(x.shape, f32),
        compiler_params=_cparams("parallel", "parallel"),
        name="merge",
    )(x, mod, g, ya, yb, yc, *weights)


def _mlp_kernel(*refs, final):
    x_ref, mod_ref, g_ref, w1_ref, w2_ref = refs[:5]
    o_ref = refs[-1]
    md = mod_ref[0]
    subs = [slice(TQ * t, TQ * (t + 1)) for t in range(x_ref.shape[1] // TQ)]
    hidden = []
    for rows in subs:
        h = (_rms(x_ref[0, rows, :]) * g_ref[...] * (1.0 + md[4:5]) + md[3:4]).astype(bf16)
        hidden.append(_dot(h, w1_ref[...]))
    for rows, a in zip(subs, hidden):
        a = jnp.maximum(a, 0.0)
        y = x_ref[0, rows, :] + md[5:6] * _dot((a * a).astype(bf16), w2_ref[...])
        if final:
            y = _rms(y) * refs[5][...]
        o_ref[0, rows, :] = y


def _mlp(x, mod, per_batch_mod, g, w1, w2, final_g=None):
    b, lx, d = x.shape
    tm = min(512, lx)
    tok = pl.BlockSpec((1, tm, d), lambda i, j: (i, j, 0))
    mod_map = (lambda i, j: (i, 0, 0)) if per_batch_mod else (lambda i, j: (0, 0, 0))
    final = final_g is not None
    resident = lambda w: pl.BlockSpec(w.shape, lambda i, j: (0, 0), pipeline_mode=pl.Buffered(1))
    in_specs = [tok, pl.BlockSpec((1, 6, d), mod_map), _const_spec(g.shape), resident(w1), resident(w2)]
    args = [x, mod, g, w1, w2]
    if final:
        in_specs.append(_const_spec(final_g.shape))
        args.append(final_g)
    return pl.pallas_call(
        functools.partial(_mlp_kernel, final=final),
        grid=(b, lx // tm),
        in_specs=in_specs,
        out_specs=tok,
        out_shape=jax.ShapeDtypeStruct(x.shape, f32),
        compiler_params=_cparams("parallel", "parallel"),
        name="mlp",
    )(*args)


def _rope_tables(n_tokens):
    rows = n_tokens // GRID_W
    row = jnp.repeat(jnp.arange(rows), GRID_W).astype(f32)
    col = jnp.tile(jnp.arange(GRID_W), rows).astype(f32)
    nf = MLA_ROPE // 4
    inv = ROPE_BASE ** (-jnp.arange(nf, dtype=f32) / nf)
    ang = jnp.concatenate([row[:, None] * inv, col[:, None] * inv], axis=-1)
    cos, sin = jnp.cos(ang), jnp.sin(ang)
    one = jnp.ones((n_tokens, MLA_NOPE), f32)
    zero = jnp.zeros((n_tokens, MLA_NOPE), f32)
    pad = jnp.zeros((n_tokens, LANES - MLA_NOPE - MLA_ROPE), f32)
    cm = jnp.concatenate([one, cos, cos, pad], axis=-1)
    sm = jnp.concatenate([zero, -sin, sin, pad], axis=-1)
    kt = jnp.concatenate([cos, cos, -sin, sin, jnp.zeros((n_tokens, LANES - 2 * MLA_ROPE), f32)], axis=-1)
    cd = jnp.tile(jnp.concatenate([cos, cos], axis=-1), (1, LANES // DF_DIM))
    sd = jnp.tile(jnp.concatenate([-sin, sin], axis=-1), (1, LANES // DF_DIM))
    return cm, sm, kt, cd, sd


def _hy_positional(l):
    t = jnp.linspace(0.0, 1.0, l, dtype=f32)[:, None]
    w = (2.0 * math.pi / l) * jnp.arange(l, dtype=f32)[:, None]
    bands = jnp.linspace(1e-4, HY_BANDS - 1, HY_BANDS, dtype=f32)[None]
    z = jnp.concatenate([t, jnp.cos(bands * w), -jnp.sin(bands * w)], axis=-1)
    z = jnp.pad(z, ((0, 0), (0, LANES - HY_EMB)))
    deltas = jnp.linspace(math.log(HY_TARGET) / HY_SLOW, math.log(HY_TARGET) / HY_FAST, HY_WIDTH, dtype=f32)
    win = jnp.exp(-t * jnp.abs(deltas)) + HY_SHIFT
    return z, win


def _dft_tables(l):
    n2 = 2 * l
    n = jnp.arange(l, dtype=jnp.int32)[None, :]
    a = jnp.arange(l // GRID_W, dtype=jnp.int32)[:, None]
    b = jnp.arange(GRID_W, dtype=jnp.int32)[:, None]
    ang_a = ((GRID_W * a * n) % n2).astype(f32) * (2.0 * math.pi / n2)
    ang_b = ((b * n) % n2).astype(f32) * (2.0 * math.pi / n2)
    ca, sa, cb, sb = jnp.cos(ang_a), jnp.sin(ang_a), jnp.cos(ang_b), jnp.sin(ang_b)
    nyq = jnp.where(n[0] % 2 == 0, 1.0, -1.0)
    c = (ca[:, None, :] * cb[None] - sa[:, None, :] * sb[None]).reshape(l, l)
    s = -(sa[:, None, :] * cb[None] + ca[:, None, :] * sb[None]).reshape(l, l)
    fmat = jnp.concatenate([c, s.at[0].set(nyq)], axis=0)
    wgt = jnp.full((l,), 2.0 / n2, f32).at[0].set(1.0 / n2)
    cat, sat, cbt, sbt = ca.T, sa.T, cb.T, sb.T
    ct = (cat[:, :, None] * cbt[:, None, :] - sat[:, :, None] * sbt[:, None, :]).reshape(l, l)
    st = -(sat[:, :, None] * cbt[:, None, :] + cat[:, :, None] * sbt[:, None, :]).reshape(l, l)
    gmat = jnp.concatenate([ct * wgt[None], st.at[:, 0].set(nyq) * wgt[None]], axis=1)
    return fmat.astype(bf16), gmat.astype(bf16)


def _prep_weights(w_in, mla_q_norm_g, mla_w_uq, mla_kv_norm_g, mla_w_ukv, hy_conv_w, hy_conv_b, hy_w1, hy_b1,
                  hy_freq, hy_w2, hy_b2, hy_w3, hy_b3, hy_skip, df_subln_g, w_br_a, w_br_b, w_br_c, w_out,
                  w_fc1, w_fc2):
    depth = w_in.shape[0]
    o = [0]
    for n in (MLA_Q_RANK, MLA_KV_RANK, MLA_ROPE, 3 * HY_WIDTH, 2 * DF_HEADS * DF_DIM, 2 * DF_HEADS * DF_DIM,
              DF_OUT, N_BRANCH * D_MODEL):
        o.append(o[-1] + n)
    half = MLA_ROPE // 2
    w_kr = w_in[..., o[2]:o[3]]
    kr_pad = jnp.zeros((depth, D_MODEL, LANES - 2 * MLA_ROPE), f32)
    w_in1 = jnp.concatenate([w_in[..., o[0]:o[3]], w_kr[..., half:], w_kr[..., :half], kr_pad,
                             w_in[..., o[3]:o[7]]], axis=-1).astype(bf16)
    w_gate = w_in[..., o[7]:o[8]].astype(bf16)

    uq = mla_w_uq.reshape(depth, MLA_Q_RANK, MLA_HEADS, MLA_NOPE + MLA_ROPE)
    nope, rope = uq[..., :MLA_NOPE], uq[..., MLA_NOPE:]
    zn = jnp.zeros_like(nope)
    zp = jnp.zeros(uq.shape[:3] + (LANES - MLA_NOPE - MLA_ROPE,), f32)
    wq = jnp.concatenate([nope, rope, zp], axis=-1).reshape(depth, MLA_Q_RANK, MLA_HEADS * LANES)
    wqs = jnp.concatenate([zn, rope[..., half:], rope[..., :half], zp], axis=-1)
    wqs = wqs.reshape(depth, MLA_Q_RANK, MLA_HEADS * LANES)
    ukv = mla_w_ukv.reshape(depth, MLA_KV_RANK, MLA_HEADS, MLA_NOPE + MLA_V)
    wk = jnp.concatenate([ukv[..., :MLA_NOPE], jnp.zeros_like(ukv[..., MLA_NOPE:])], axis=-1)
    wk = wk.reshape(depth, MLA_KV_RANK, MLA_HEADS * LANES)
    wv = ukv[..., MLA_NOPE:].reshape(depth, MLA_KV_RANK, MLA_OUT)

    i = jnp.arange(LANES)[:, None]
    js = jnp.arange(MLA_HEADS * LANES)[None, :] % LANES - MLA_NOPE
    e = ((i < MLA_ROPE) & (js == i)).astype(f32)
    e_rope = ((i < 2 * MLA_ROPE) & (js == i % MLA_ROPE)).astype(f32)
    wke = jnp.concatenate([wk, jnp.broadcast_to(e, (depth,) + e.shape)], axis=1).astype(bf16)
    wke_rope = jnp.concatenate([wk, jnp.broadcast_to(e_rope, (depth,) + e.shape)], axis=1).astype(bf16)
    a = jnp.arange(2 * DF_HEADS * DF_DIM)
    dh = DF_DIM // 2
    swap = (a // DF_DIM) * DF_DIM + (a % DF_DIM + dh) % DF_DIM
    p = (a[:, None] == swap[None, :]).astype(bf16)

    pad_rows = lambda w, r: jnp.pad(w, ((0, 0), (0, r - w.shape[1]), (0, 0)))
    pad_cols = lambda w, c: jnp.pad(w, ((0, 0),) * (w.ndim - 1) + ((0, c - w.shape[-1]),))
    layers = []
    for l in range(depth):
        layers.append({
            'w_in1': w_in1[l], 'w_gate': w_gate[l],
            'q_norm_g': mla_q_norm_g[l][None], 'kv_norm_g': mla_kv_norm_g[l][None],
            'wq': wq[l].astype(bf16), 'wqs': wqs[l].astype(bf16), 'wke': wke[l], 'wke_rope': wke_rope[l],
            'wv': wv[l].astype(bf16), 'p': p,
            'hy_conv_w': hy_conv_w[l], 'hy_conv_b': hy_conv_b[l][None], 'hy_skip': hy_skip[l][None],
            'subln_g': jnp.broadcast_to(df_subln_g[l][:, None], (DF_V, LANES)),
            'w_br_a': w_br_a[l].astype(bf16), 'w_br_b': w_br_b[l].astype(bf16),
            'w_br_c': w_br_c[l].astype(bf16), 'w_out': w_out[l].astype(bf16),
            'w_fc1': w_fc1[l].astype(bf16), 'w_fc2': w_fc2[l].astype(bf16),
        })
    hy_filter_w = [pad_cols(pad_rows(hy_w1, LANES), LANES), pad_cols(hy_b1, LANES)[:, None],
                   pad_cols(hy_freq, LANES)[:, None], pad_cols(pad_rows(hy_w2, LANES), LANES),
                   pad_cols(hy_b2, LANES)[:, None], pad_rows(hy_w3, LANES), hy_b3[:, None]]
    return layers, hy_filter_w


def _token_mixers(lat, ctx, n_lat, n_ctx, lw, lams, lam_init, hy_lat, hy_ctx, with_ctx):
    qm, _, _, qd, _, _, hy = lat
    cqm, km, vm, cqd, kd, vd, chy = ctx
    n_all = n_lat + n_ctx
    ya = _mla_attn(qm, km, vm, n_all, 0)
    yc = _diff_attn(lams, lw['subln_g'], qd, kd, vd, n_all, 0, lam_init)
    yb = _hy_conv(hy, lw, *hy_lat)
    if not with_ctx:
        return (ya, yb, yc), None
    ya_c = _mla_attn(cqm, km, vm, n_ctx, n_lat // n_ctx)
    yc_c = _diff_attn(lams, lw['subln_g'], cqd, kd, vd, n_ctx, n_lat // n_ctx, lam_init)
    yb_c = _hy_conv(chy, lw, *hy_ctx)
    return (ya, yb, yc), (ya_c, yb_c, yc_c)


def kernel(x, c, ctx, c_ctx, norm_mix_g, norm_ffn_g, w_mod, b_mod, w_in, mla_q_norm_g, mla_w_uq, mla_kv_norm_g, mla_w_ukv, hy_conv_w, hy_conv_b, hy_w1, hy_b1, hy_freq, hy_w2, hy_b2, hy_w3, hy_b3, hy_skip, df_lq1, df_lk1, df_lq2, df_lk2, df_subln_g, w_br_a, w_br_b, w_br_c, w_out, w_fc1, w_fc2, final_norm_g):
    b, n_lat, d = x.shape
    n_ctx = ctx.shape[1]
    assert n_lat % n_ctx == 0 and n_ctx % TQ == 0
    depth = w_in.shape[0]
    layers, hy_filter_w = _prep_weights(w_in, mla_q_norm_g, mla_w_uq, mla_kv_norm_g, mla_w_ukv, hy_conv_w, hy_conv_b,
                           hy_w1, hy_b1, hy_freq, hy_w2, hy_b2, hy_w3, hy_b3, hy_skip, df_subln_g,
                           w_br_a, w_br_b, w_br_c, w_out, w_fc1, w_fc2)
    tables = _rope_tables(n_lat)
    z_lat, win_lat = _hy_positional(n_lat)
    z_ctx, win_ctx = _hy_positional(n_ctx)
    f_lat, g_lat = _dft_tables(n_lat)
    f_ctx, g_ctx = _dft_tables(n_ctx)
    spec_lat = _hy_filter(z_lat, hy_filter_w, win_lat, f_lat)
    spec_ctx = _hy_filter(z_ctx, hy_filter_w, win_ctx, f_ctx)

    rows = -(-(b + 1) // 8) * 8
    cc = jnp.concatenate([c, c_ctx[None], jnp.zeros((rows - b - 1, d), f32)], axis=0)
    mod = _modulation(cc, w_mod, b_mod).reshape(depth, rows, 6, d)

    xc = ctx
    for l in range(depth):
        lw = layers[l]
        with_ctx = l < depth - 1
        lam_init = 0.8 - 0.6 * math.exp(-0.3 * l)
        mod_lat, mod_ctx = mod[l, :b], mod[l, b:b + 1]
        g_mix, g_ffn = norm_mix_g[l][None], norm_ffn_g[l][None]
        lams = [df_lq1[l][None], df_lk1[l][None], df_lq2[l][None], df_lk2[l][None]]

        hy_lat = ([a[l] for a in spec_lat], f_lat, g_lat)
        hy_ctx = ([a[l] for a in spec_ctx], f_ctx, g_ctx) if with_ctx else None

        n_all = n_lat + n_ctx
        parts = _in_proj(x, mod_lat, True, g_mix, lw, tables, n_all, 0)
        kv_bufs = [parts[i] for i in (1, 2, 4, 5)]
        parts_c = _in_proj(xc, mod_ctx, False, g_mix, lw, None, n_all, n_lat // n_ctx, kv_bufs)
        ys, ys_c = _token_mixers(parts, parts_c, n_lat, n_ctx, lw, lams, lam_init, hy_lat, hy_ctx, with_ctx)

        x = _merge(x, mod_lat, True, g_mix, *ys, lw)
        x = _mlp(x, mod_lat, True, g_ffn, lw['w_fc1'], lw['w_fc2'],
                 final_norm_g[None] if l == depth - 1 else None)
        if with_ctx:
            xc = _merge(xc, mod_ctx, False, g_mix, *ys_c, lw)
            xc = _mlp(xc, mod_ctx, False, g_ffn, lw['w_fc1'], lw['w_fc2'])
    return x
```

```python
import functools
import math

import jax
import jax.numpy as jnp
from jax import lax
from jax.experimental import pallas as pl
from jax.experimental.pallas import tpu as pltpu

f32 = jnp.float32
bf16 = jnp.bfloat16

D_MODEL = 1024
GRID_W = 64
EPS = 1e-6
ROPE_BASE = 10000.0

MLA_HEADS = 8
MLA_NOPE = 64
MLA_ROPE = 32
MLA_V = 64
MLA_Q_RANK = 256
MLA_KV_RANK = 128
MLA_OUT = MLA_HEADS * MLA_V
MLA_SCALE = (MLA_NOPE + MLA_ROPE) ** -0.5

HY_WIDTH = 256
HY_EMB = 33
HY_BANDS = (HY_EMB - 1) // 2
HY_TARGET = 1e-2
HY_FAST = 0.3
HY_SLOW = 1.5
HY_SHIFT = 0.05

DF_HEADS = 4
DF_DIM = 32
DF_V = 2 * DF_DIM
DF_OUT = DF_HEADS * DF_V
DF_SCALE = DF_DIM ** -0.5

N_BRANCH = 3

LANES = 128
BF16_ROWS = 16
MXU_TILE = 256
KEY_CHUNK = 1152
TQ = MXU_TILE
MAPS_PER_TILE = 8
LOG2E = 1.4426950408889634
VMEM_LIMIT = 56 * 1024 * 1024

C_Q, C_KV, C_KR, C_HY, C_DQ, C_DK, C_DV, C_END = 0, 256, 384, 512, 1280, 1536, 1792, 2048


def _cparams(*sem):
    return pltpu.CompilerParams(dimension_semantics=sem, vmem_limit_bytes=VMEM_LIMIT)


def _dot(a, b):
    return jnp.dot(a, b, preferred_element_type=f32)


def _rms(x):
    return x * lax.rsqrt(jnp.mean(x * x, axis=-1, keepdims=True) + EPS)


def _shr(n, k):
    return n >> k if isinstance(n, int) else lax.shift_right_logical(n, k)


def _aligned(x, m):
    return x if isinstance(x, int) else pl.multiple_of(x, m)


def _const_spec(shape):
    nd = len(shape)
    return pl.BlockSpec(shape, lambda *_: (0,) * nd)


def _mod_kernel(c_ref, w_ref, b_ref, o_ref):
    c = c_ref[...]
    s = c * jax.nn.sigmoid(c)
    o_ref[0] = _dot(s.astype(bf16), w_ref[0].astype(bf16)) + b_ref[0]


def _modulation(cc, w_mod, b_mod):
    depth, d, n = w_mod.shape
    r = cc.shape[0]
    tn = 2048
    return pl.pallas_call(
        _mod_kernel,
        grid=(depth, n // tn),
        in_specs=[pl.BlockSpec((r, d), lambda l, j: (0, 0)),
                  pl.BlockSpec((1, d, tn), lambda l, j: (l, 0, j)),
                  pl.BlockSpec((1, 1, tn), lambda l, j: (l, 0, j))],
        out_specs=pl.BlockSpec((1, r, tn), lambda l, j: (l, 0, j)),
        out_shape=jax.ShapeDtypeStruct((depth, r, n), f32),
        compiler_params=_cparams("parallel", "parallel"),
        name="modulation",
    )(cc, w_mod, b_mod.reshape(depth, 1, n))


def _in_proj_kernel(*refs, rope, n_alias):
    (x_ref, mod_ref, g_ref, w_ref, qg_ref, kvg_ref, wq_ref, wqs_ref, wke_ref, wv_ref, p_ref) = refs[:11]
    refs = refs[11:]
    if rope:
        cm_ref, sm_ref, kt_ref, cd_ref, sd_ref = refs[:5]
        refs = refs[5:]
    qm_ref, km_ref, vm_ref, qd_ref, kd_ref, vd_ref, hy_ref = refs[n_alias:]

    md = mod_ref[0]
    subs = [slice(TQ * t, TQ * (t + 1)) for t in range(x_ref.shape[1] // TQ)]
    projected = []
    for rows in subs:
        h = _rms(x_ref[0, rows, :]) * g_ref[...] * (1.0 + md[1:2]) + md[0:1]
        projected.append(_dot(h.astype(bf16), w_ref[...]))

    for t, (rows, p) in enumerate(zip(subs, projected)):
        qn = (_rms(p[:, C_Q:C_KV]) * qg_ref[...]).astype(bf16)
        kvn = (_rms(p[:, C_KV:C_KR]) * kvg_ref[...]).astype(bf16)
        kr = p[:, C_KR:C_HY]
        q = _dot(qn, wq_ref[...])
        dq = p[:, C_DQ:C_DK]
        dk = p[:, C_DK:C_DV]
        if rope:
            kr = kr * kt_ref[rows, :]
            cm = jnp.concatenate([cm_ref[rows, :]] * MLA_HEADS, axis=-1)
            sm = jnp.concatenate([sm_ref[rows, :]] * MLA_HEADS, axis=-1)
            q = q * cm + _dot(qn, wqs_ref[...]) * sm
            cd = jnp.concatenate([cd_ref[rows, :]] * 2, axis=-1)
            sd = jnp.concatenate([sd_ref[rows, :]] * 2, axis=-1)
            dq = dq * cd + _dot(dq.astype(bf16), p_ref[...]) * sd
            dk = dk * cd + _dot(dk.astype(bf16), p_ref[...]) * sd
        k = _dot(jnp.concatenate([kvn, kr.astype(bf16)], axis=-1), wke_ref[...]).astype(bf16)
        dk = dk.astype(bf16)
        qm_ref[0, t] = (q * (MLA_SCALE * LOG2E)).T.astype(bf16)
        for h in range(km_ref.shape[1]):
            km_ref[0, h, rows, :] = k[:, LANES * h:LANES * (h + 1)]
        vm_ref[0, :, rows] = _dot(kvn, wv_ref[...]).T.astype(bf16)
        qd_ref[0, t] = (dq * (DF_SCALE * LOG2E)).T.astype(bf16)
        for h in range(kd_ref.shape[1]):
            kd_ref[0, h, rows, :] = dk[:, LANES * h:LANES * (h + 1)]
        vd_ref[0, :, rows] = p[:, C_DV:C_END].T.astype(bf16)
        hy_ref[0, rows, :] = p[:, C_HY:C_DQ].astype(bf16)


def _in_proj(x, mod, per_batch_mod, g, lw, tables, n_keys, key_block, kv_bufs=None):
    b, lx, d = x.shape
    tm = min(512, lx)
    rope = tables is not None
    kb = key_block * (lx // tm)
    tok = lambda n: pl.BlockSpec((1, tm, n), lambda i, j: (j, i, 0))
    mod_map = (lambda i, j: (j, 0, 0)) if per_batch_mod else (lambda i, j: (0, 0, 0))
    weights = [lw['w_in1'], lw['q_norm_g'], lw['kv_norm_g'], lw['wq'], lw['wqs'],
               lw['wke_rope' if rope else 'wke'], lw['wv'], lw['p']]
    in_specs = ([tok(d), pl.BlockSpec((1, 6, d), mod_map), _const_spec(g.shape)]
                + [_const_spec(w.shape) for w in weights])
    args = [x, mod, g] + weights
    if rope:
        in_specs += [pl.BlockSpec((tm, LANES), lambda i, j: (i, 0))] * len(tables)
        args += list(tables)
    aliases = {}
    if kv_bufs is not None:
        aliases = {len(args) + n: o for n, o in enumerate((1, 2, 4, 5))}
        in_specs += [pl.BlockSpec(memory_space=pl.ANY)] * len(kv_bufs)
        args += list(kv_bufs)
    f_q, f_d = MLA_HEADS * LANES, 2 * DF_HEADS * DF_DIM
    qtile = lambda n: pl.BlockSpec((1, tm // TQ, n, TQ), lambda i, j: (j, i, 0, 0))
    keys = lambda n: pl.BlockSpec((1, n // LANES, tm, LANES), lambda i, j: (j, 0, kb + i, 0))
    vals = lambda n: pl.BlockSpec((1, n, tm), lambda i, j: (j, 0, kb + i))
    sds = lambda *shape: jax.ShapeDtypeStruct(shape, bf16)
    out_specs = [qtile(f_q), keys(f_q), vals(MLA_OUT), qtile(f_d), keys(f_d), vals(DF_OUT), tok(3 * HY_WIDTH)]
    out_shape = [sds(b, lx // TQ, f_q, TQ), sds(b, f_q // LANES, n_keys, LANES), sds(b, MLA_OUT, n_keys),
                 sds(b, lx // TQ, f_d, TQ), sds(b, f_d // LANES, n_keys, LANES), sds(b, DF_OUT, n_keys),
                 sds(b, lx, 3 * HY_WIDTH)]
    return pl.pallas_call(
        functools.partial(_in_proj_kernel, rope=rope, n_alias=len(aliases)),
        grid=(lx // tm, b),
        in_specs=in_specs,
        out_specs=out_specs,
        out_shape=out_shape,
        input_output_aliases=aliases,
        compiler_params=_cparams("parallel", "parallel"),
        name="in_proj",
    )(*args)


def _softmax_maps(n_maps, qt_of, k_of, vt_of, s_scr, o_scr):
    _, lk, tq = s_scr.shape
    dv = o_scr.shape[0] // n_maps
    kc = KEY_CHUNK if lk % KEY_CHUNK == 0 else MXU_TILE
    assert lk % kc == 0
    ones = jnp.ones((BF16_ROWS, kc), bf16)

    def scores(n, slot):
        qt = qt_of(n)
        pm = None
        for off in range(0, lk, kc):
            s = _dot(k_of(n, off, kc), qt)
            s_scr[slot, off:off + kc, :] = s
            cm = jnp.max(s.reshape(kc // 8, 8, tq), axis=0)
            pm = cm if pm is None else jnp.maximum(pm, cm)
        return jnp.max(pm, axis=0, keepdims=True)

    def values(n, slot, m):
        acc = None
        for off in range(0, lk, kc):
            p = jnp.exp2(s_scr[slot, off:off + kc, :] - m).astype(bf16)
            part = _dot(jnp.concatenate([vt_of(n, off, kc), ones], axis=0), p)
            acc = part if acc is None else acc + part
        o_scr[pl.ds(_aligned(n * dv, dv), dv), :] = acc[:dv] / acc[dv:dv + 1]

    if lk == kc and n_maps <= MAPS_PER_TILE:
        qks = [_dot(k_of(n, 0, kc), qt_of(n)) for n in range(n_maps)]
        col_max = lambda s: jnp.max(jnp.max(s.reshape(kc // 8, 8, tq), axis=0), axis=0, keepdims=True)
        pts = [jnp.exp2(s - col_max(s)).astype(bf16) for s in qks]
        for n, p in enumerate(pts):
            acc = _dot(jnp.concatenate([vt_of(n, 0, kc), ones], axis=0), p)
            o_scr[n * dv:(n + 1) * dv, :] = acc[:dv] / acc[dv:dv + 1]
        return

    def group(base, m, count):
        for e in range(count):
            m_next = scores(base + e + 1, (e + 1) & 1)
            values(base + e, e & 1, m)
            m = m_next
        return m

    per_iter = 16 if n_maps >= 64 else (4 if n_maps >= 16 else 2)
    trips = (n_maps - 1) // per_iter
    m = lax.fori_loop(0, trips, lambda i, m: group(i * per_iter, m, per_iter), scores(0, 0))
    m = group(trips * per_iter, m, n_maps - 1 - trips * per_iter)
    values(n_maps - 1, (n_maps - 1) & 1, m)


def _mla_attn_kernel(qt_ref, k_ref, vt_ref, o_ref, s_scr, o_scr):
    n_tiles = qt_ref.shape[1]
    tile_of = lambda n: _shr(n, 3)
    head_of = lambda n: n & (MAPS_PER_TILE - 1)
    _softmax_maps(
        n_tiles * MLA_HEADS,
        lambda n: qt_ref[0, tile_of(n), pl.ds(_aligned(head_of(n) * LANES, LANES), LANES), :],
        lambda n, off, kc: k_ref[0, head_of(n), off:off + kc, :],
        lambda n, off, kc: vt_ref[0, pl.ds(_aligned(head_of(n) * MLA_V, MLA_V), MLA_V), off:off + kc],
        s_scr, o_scr)
    for t in range(n_tiles):
        for j in range(MLA_OUT // LANES):
            r = t * MLA_OUT + LANES * j
            o_ref[0, TQ * t:TQ * (t + 1), LANES * j:LANES * (j + 1)] = o_scr[r:r + LANES, :].T.astype(bf16)


def _diff_attn_kernel(lq1, lk1, lq2, lk2, g_ref, qt_ref, k_ref, vt_ref, o_ref, s_scr, o_scr, *, lam_init):
    n_tiles = qt_ref.shape[1]
    row = lax.broadcasted_iota(jnp.int32, (LANES, TQ), 0)
    per_lanes = LANES // DF_DIM
    tile_of = lambda n: _shr(n, 3)
    map_of = lambda n: n & (MAPS_PER_TILE - 1)
    lanes_of = lambda n: _shr(map_of(n), 2)

    def qt_of(n):
        lo = DF_DIM * (n & (per_lanes - 1))
        qt = qt_ref[0, tile_of(n), pl.ds(_aligned(lanes_of(n) * LANES, LANES), LANES), :]
        return jnp.where(row < lo, 0.0, jnp.where(row < lo + DF_DIM, qt, 0.0)).astype(bf16)

    def vt_of(n, off, kc):
        h = _shr(map_of(n), 1)
        return vt_ref[0, pl.ds(_aligned(h * DF_V, DF_V), DF_V), off:off + kc]

    _softmax_maps(n_tiles * 2 * DF_HEADS, qt_of, lambda n, off, kc: k_ref[0, lanes_of(n), off:off + kc, :], vt_of,
                  s_scr, o_scr)

    lam = (jnp.exp(jnp.sum(lq1[...] * lk1[...], axis=-1, keepdims=True))
           - jnp.exp(jnp.sum(lq2[...] * lk2[...], axis=-1, keepdims=True)) + lam_init)
    g = jnp.concatenate([g_ref[...]] * (TQ // LANES), axis=-1)
    for t in range(n_tiles):
        for j in range(DF_OUT // LANES):
            pair = []
            for h in (2 * j, 2 * j + 1):
                r = (t * MAPS_PER_TILE + 2 * h) * DF_V
                o = o_scr[r:r + DF_V, :] - lam * o_scr[r + DF_V:r + 2 * DF_V, :]
                ms = jnp.mean(o * o, axis=0, keepdims=True)
                pair.append(o * lax.rsqrt(ms + EPS) * g * (1.0 - lam_init))
            o_ref[0, TQ * t:TQ * (t + 1), LANES * j:LANES * (j + 1)] = (
                jnp.concatenate(pair, axis=0).T.astype(bf16))


def _attn_call(kernel, name, small, qt, k_all, vt_all, n_keys, key_block, dv):
    b, n_qt, f, _ = qt.shape
    f_out = vt_all.shape[1]
    tiles = next(t for t in (8, 4, 2, 1) if n_qt % t == 0)
    scratch = [pltpu.VMEM((2, n_keys, TQ), f32), pltpu.VMEM((tiles * MAPS_PER_TILE * dv, TQ), f32)]
    return pl.pallas_call(
        kernel,
        grid=(b, n_qt // tiles),
        in_specs=([_const_spec(a.shape) for a in small]
                  + [pl.BlockSpec((1, tiles, f, TQ), lambda i, j: (i, j, 0, 0)),
                     pl.BlockSpec((1, f // LANES, n_keys, LANES), lambda i, j: (i, 0, key_block, 0)),
                     pl.BlockSpec((1, f_out, n_keys), lambda i, j: (i, 0, key_block))]),
        out_specs=pl.BlockSpec((1, tiles * TQ, f_out), lambda i, j: (i, j, 0)),
        out_shape=jax.ShapeDtypeStruct((b, n_qt * TQ, f_out), bf16),
        scratch_shapes=scratch,
        compiler_params=_cparams("parallel", "parallel"),
        name=name,
    )(*small, qt, k_all, vt_all)


def _mla_attn(q, k_all, vt_all, n_keys, key_block):
    return _attn_call(_mla_attn_kernel, "mla_attn", [], q, k_all, vt_all, n_keys, key_block, MLA_V)


def _diff_attn(lams, g, q, k_all, vt_all, n_keys, key_block, lam_init):
    return _attn_call(functools.partial(_diff_attn_kernel, lam_init=lam_init), "diff_attn",
                      list(lams) + [g], q, k_all, vt_all, n_keys, key_block, DF_V)


def _hy_filter_kernel(z_ref, w1_ref, b1_ref, fr_ref, w2_ref, b2_ref, w3_ref, b3_ref, win_ref,
                      fc_ref, fs_ref, a_ref, a2_ref, bq_ref, h_scr):
    kt = pl.program_id(1)
    hp = lax.Precision.HIGHEST

    @pl.when(kt == 0)
    def _():
        fr = fr_ref[0]
        a = jnp.sin(fr * (jnp.dot(z_ref[...], w1_ref[0], precision=hp, preferred_element_type=f32) + b1_ref[0]))
        a = jnp.sin(fr * (jnp.dot(a, w2_ref[0], precision=hp, preferred_element_type=f32) + b2_ref[0]))
        h = jnp.dot(a, w3_ref[0], precision=hp, preferred_element_type=f32) + b3_ref[0]
        win = win_ref[...]
        row = lax.broadcasted_iota(jnp.int32, win.shape, 0)
        h_scr[:, :HY_WIDTH] = (h[:, :HY_WIDTH] * win).astype(bf16)
        h_scr[:, HY_WIDTH:] = jnp.where(row == 0, 0.0, h[:, HY_WIDTH:] * win).astype(bf16)

    hh = h_scr[...]
    c = _dot(fc_ref[...], hh)
    s = _dot(fs_ref[...], hh)
    tk = c.shape[0]
    row = lax.broadcasted_iota(jnp.int32, (tk, HY_WIDTH), 0) + kt * tk
    first = row == 0
    a = c[:, :HY_WIDTH] + c[:, HY_WIDTH:]
    second = jnp.where(first, s[:, :HY_WIDTH] + s[:, HY_WIDTH:], s[:, :HY_WIDTH] - s[:, HY_WIDTH:])
    a_ref[0] = a
    a2_ref[0] = jnp.where(first, second, a)
    bq_ref[0] = jnp.where(first, 0.0, second)


def _hy_filter(z, hw, win, fmat):
    l = z.shape[0]
    depth = hw[0].shape[0]
    tk = min(512, l)
    nk = l // tk
    per_layer = lambda a: pl.BlockSpec((1,) + a.shape[1:], lambda d, k: (d,) + (0,) * (a.ndim - 1))
    out = jax.ShapeDtypeStruct((depth, l, HY_WIDTH), f32)
    return pl.pallas_call(
        _hy_filter_kernel,
        grid=(depth, nk),
        in_specs=([_const_spec(z.shape)] + [per_layer(a) for a in hw] + [_const_spec(win.shape)]
                  + [pl.BlockSpec((tk, l), lambda d, k: (k, 0)), pl.BlockSpec((tk, l), lambda d, k: (k + nk, 0))]),
        out_specs=[pl.BlockSpec((1, tk, HY_WIDTH), lambda d, k: (d, k, 0))] * 3,
        out_shape=[out] * 3,
        scratch_shapes=[pltpu.VMEM((l, 2 * HY_WIDTH), bf16)],
        compiler_params=_cparams("parallel", "arbitrary"),
        name="hy_filter",
    )(z, *hw, win, fmat, fmat)


def _hy_conv_kernel(p_ref, cw_ref, cb_ref, skip_ref, fc_ref, fs_ref, gc_ref, gs_ref,
                    a_ref, a2_ref, bq_ref, o_ref, u_scr, ub_scr, x0_scr, acc_scr, *, bb):
    kt = pl.program_id(1)
    l = p_ref.shape[1]

    @pl.when(kt == 0)
    def _():
        row = lax.broadcasted_iota(jnp.int32, (l, 3 * HY_WIDTH), 0)
        cw = cw_ref[...]
        for i in range(bb):
            p = p_ref[i].astype(f32)
            prev = jnp.where(row == 0, 0.0, pltpu.roll(p, 1, axis=0))
            nxt = jnp.where(row == l - 1, 0.0, pltpu.roll(p, l - 1, axis=0))
            uc = cw[0:1] * prev + cw[1:2] * p + cw[2:3] * nxt + cb_ref[...]
            x0_scr[i] = uc[:, :HY_WIDTH]
            u = uc[:, 2 * HY_WIDTH:] * uc[:, HY_WIDTH:2 * HY_WIDTH]
            u_scr[i] = u
            ub_scr[i] = u.astype(bf16)
            acc_scr[i] = jnp.zeros((l, HY_WIDTH), f32)

    a, a2, bq = a_ref[...], a2_ref[...], bq_ref[...]
    spectra = [(_dot(fc_ref[...], ub_scr[i]), _dot(fs_ref[...], ub_scr[i])) for i in range(bb)]
    for i, (ur, ui) in enumerate(spectra):
        zr = (ur * a - ui * bq).astype(bf16)
        zi = (ur * bq + ui * a2).astype(bf16)
        acc_scr[i] += _dot(gc_ref[...], zr) + _dot(gs_ref[...], zi)

    @pl.when(kt == pl.num_programs(1) - 1)
    def _():
        for i in range(bb):
            o_ref[i] = (x0_scr[i] * (acc_scr[i] + skip_ref[...] * u_scr[i])).astype(bf16)


def _hy_conv(p_hy, lw, spec, fmat, gmat):
    b, l, _ = p_hy.shape
    bb = 2
    tk = min(512, l)
    nk = l // tk
    a, a2, bq = spec
    return pl.pallas_call(
        functools.partial(_hy_conv_kernel, bb=bb),
        grid=(b // bb, nk),
        in_specs=[pl.BlockSpec((bb, l, 3 * HY_WIDTH), lambda i, k: (i, 0, 0)),
                  _const_spec(lw['hy_conv_w'].shape), _const_spec(lw['hy_conv_b'].shape),
                  _const_spec(lw['hy_skip'].shape),
                  pl.BlockSpec((tk, l), lambda i, k: (k, 0)), pl.BlockSpec((tk, l), lambda i, k: (k + nk, 0)),
                  pl.BlockSpec((l, tk), lambda i, k: (0, k)), pl.BlockSpec((l, tk), lambda i, k: (0, k + nk))]
                 + [pl.BlockSpec((tk, HY_WIDTH), lambda i, k: (k, 0))] * 3,
        out_specs=pl.BlockSpec((bb, l, HY_WIDTH), lambda i, k: (i, 0, 0)),
        out_shape=jax.ShapeDtypeStruct((b, l, HY_WIDTH), bf16),
        scratch_shapes=[pltpu.VMEM((bb, l, HY_WIDTH), f32), pltpu.VMEM((bb, l, HY_WIDTH), bf16),
                        pltpu.VMEM((bb, l, HY_WIDTH), f32), pltpu.VMEM((bb, l, HY_WIDTH), f32)],
        compiler_params=_cparams("parallel", "arbitrary"),
        name="hy_conv",
    )(p_hy, lw['hy_conv_w'], lw['hy_conv_b'], lw['hy_skip'], fmat, fmat, gmat, gmat, a, a2, bq)


def _merge_kernel(x_ref, mod_ref, g_ref, ya_ref, yb_ref, yc_ref, wg_ref, wa_ref, wb_ref, wc_ref, wo_ref,
                  o_ref):
    md = mod_ref[0]
    d = D_MODEL
    subs = [slice(TQ * t, TQ * (t + 1)) for t in range(x_ref.shape[1] // TQ)]
    staged = []
    for rows in subs:
        h = (_rms(x_ref[0, rows, :]) * g_ref[...] * (1.0 + md[1:2]) + md[0:1]).astype(bf16)
        gates = [_dot(h, wg_ref[:, d * i:d * (i + 1)]) for i in range(N_BRANCH)]
        branches = [_dot(ya_ref[0, rows, :], wa_ref[...]), _dot(yb_ref[0, rows, :], wb_ref[...]),
                    _dot(yc_ref[0, rows, :], wc_ref[...])]
        staged.append((gates, branches))
    for rows, (gates, branches) in zip(subs, staged):
        m = functools.reduce(jnp.add, [jax.nn.sigmoid(gt) * br for gt, br in zip(gates, branches)])
        o_ref[0, rows, :] = x_ref[0, rows, :] + md[2:3] * _dot(m.astype(bf16), wo_ref[...])


def _merge(x, mod, per_batch_mod, g, ya, yb, yc, lw):
    b, lx, d = x.shape
    tm = min(512, lx)
    tok = lambda n: pl.BlockSpec((1, tm, n), lambda i, j: (i, j, 0))
    mod_map = (lambda i, j: (i, 0, 0)) if per_batch_mod else (lambda i, j: (0, 0, 0))
    weights = [lw['w_gate'], lw['w_br_a'], lw['w_br_b'], lw['w_br_c'], lw['w_out']]
    return pl.pallas_call(
        _merge_kernel,
        grid=(b, lx // tm),
        in_specs=[tok(d), pl.BlockSpec((1, 6, d), mod_map), _const_spec(g.shape),
                  tok(MLA_OUT), tok(HY_WIDTH), tok(DF_OUT)] + [_const_spec(w.shape) for w in weights],
        out_specs=tok(d),
        out_shape=jax.ShapeDtypeStruct(x.shape, f32),
        compiler_params=_cparams("parallel", "parallel"),
        name="merge",
    )(x, mod, g, ya, yb, yc, *weights)


def _mlp_kernel(*refs, final):
    x_ref, mod_ref, g_ref, w1_ref, w2_ref = refs[:5]
    o_ref = refs[-1]
    md = mod_ref[0]
    subs = [slice(TQ * t, TQ * (t + 1)) for t in range(x_ref.shape[1] // TQ)]
    hidden = []
    for rows in subs:
        h = (_rms(x_ref[0, rows, :]) * g_ref[...] * (1.0 + md[4:5]) + md[3:4]).astype(bf16)
        hidden.append(_dot(h, w1_ref[...]))
    for rows, a in zip(subs, hidden):
        a = jnp.maximum(a, 0.0)
        y = x_ref[0, rows, :] + md[5:6] * _dot((a * a).astype(bf16), w2_ref[...])
        if final:
            y = _rms(y) * refs[5][...]
        o_ref[0, rows, :] = y


def _mlp(x, mod, per_batch_mod, g, w1, w2, final_g=None):
    b, lx, d = x.shape
    tm = min(512, lx)
    tok = pl.BlockSpec((1, tm, d), lambda i, j: (i, j, 0))
    mod_map = (lambda i, j: (i, 0, 0)) if per_batch_mod else (lambda i, j: (0, 0, 0))
    final = final_g is not None
    resident = lambda w: pl.BlockSpec(w.shape, lambda i, j: (0, 0), pipeline_mode=pl.Buffered(1))
    in_specs = [tok, pl.BlockSpec((1, 6, d), mod_map), _const_spec(g.shape), resident(w1), resident(w2)]
    args = [x, mod, g, w1, w2]
    if final:
        in_specs.append(_const_spec(final_g.shape))
        args.append(final_g)
    return pl.pallas_call(
        functools.partial(_mlp_kernel, final=final),
        grid=(b, lx // tm),
        in_specs=in_specs,
        out_specs=tok,
        out_shape=jax.ShapeDtypeStruct(x.shape, f32),
        compiler_params=_cparams("parallel", "parallel"),
        name="mlp",
    )(*args)


def _rope_tables(n_tokens):
    rows = n_tokens // GRID_W
    row = jnp.repeat(jnp.arange(rows), GRID_W).astype(f32)
    col = jnp.tile(jnp.arange(GRID_W), rows).astype(f32)
    nf = MLA_ROPE // 4
    inv = ROPE_BASE ** (-jnp.arange(nf, dtype=f32) / nf)
    ang = jnp.concatenate([row[:, None] * inv, col[:, None] * inv], axis=-1)
    cos, sin = jnp.cos(ang), jnp.sin(ang)
    one = jnp.ones((n_tokens, MLA_NOPE), f32)
    zero = jnp.zeros((n_tokens, MLA_NOPE), f32)
    pad = jnp.zeros((n_tokens, LANES - MLA_NOPE - MLA_ROPE), f32)
    cm = jnp.concatenate([one, cos, cos, pad], axis=-1)
    sm = jnp.concatenate([zero, -sin, sin, pad], axis=-1)
    kt = jnp.concatenate([cos, cos, -sin, sin, jnp.zeros((n_tokens, LANES - 2 * MLA_ROPE), f32)], axis=-1)
    cd = jnp.tile(jnp.concatenate([cos, cos], axis=-1), (1, LANES // DF_DIM))
    sd = jnp.tile(jnp.concatenate([-sin, sin], axis=-1), (1, LANES // DF_DIM))
    return cm, sm, kt, cd, sd


def _hy_positional(l):
    t = jnp.linspace(0.0, 1.0, l, dtype=f32)[:, None]
    w = (2.0 * math.pi / l) * jnp.arange(l, dtype=f32)[:, None]
    bands = jnp.linspace(1e-4, HY_BANDS - 1, HY_BANDS, dtype=f32)[None]
    z = jnp.concatenate([t, jnp.cos(bands * w), -jnp.sin(bands * w)], axis=-1)
    z = jnp.pad(z, ((0, 0), (0, LANES - HY_EMB)))
    deltas = jnp.linspace(math.log(HY_TARGET) / HY_SLOW, math.log(HY_TARGET) / HY_FAST, HY_WIDTH, dtype=f32)
    win = jnp.exp(-t * jnp.abs(deltas)) + HY_SHIFT
    return z, win


def _dft_tables(l):
    n2 = 2 * l
    n = jnp.arange(l, dtype=jnp.int32)[None, :]
    a = jnp.arange(l // GRID_W, dtype=jnp.int32)[:, None]
    b = jnp.arange(GRID_W, dtype=jnp.int32)[:, None]
    ang_a = ((GRID_W * a * n) % n2).astype(f32) * (2.0 * math.pi / n2)
    ang_b = ((b * n) % n2).astype(f32) * (2.0 * math.pi / n2)
    ca, sa, cb, sb = jnp.cos(ang_a), jnp.sin(ang_a), jnp.cos(ang_b), jnp.sin(ang_b)
    nyq = jnp.where(n[0] % 2 == 0, 1.0, -1.0)
    c = (ca[:, None, :] * cb[None] - sa[:, None, :] * sb[None]).reshape(l, l)
    s = -(sa[:, None, :] * cb[None] + ca[:, None, :] * sb[None]).reshape(l, l)
    fmat = jnp.concatenate([c, s.at[0].set(nyq)], axis=0)
    wgt = jnp.full((l,), 2.0 / n2, f32).at[0].set(1.0 / n2)
    cat, sat, cbt, sbt = ca.T, sa.T, cb.T, sb.T
    ct = (cat[:, :, None] * cbt[:, None, :] - sat[:, :, None] * sbt[:, None, :]).reshape(l, l)
    st = -(sat[:, :, None] * cbt[:, None, :] + cat[:, :, None] * sbt[:, None, :]).reshape(l, l)
    gmat = jnp.concatenate([ct * wgt[None], st.at[:, 0].set(nyq) * wgt[None]], axis=1)
    return fmat.astype(bf16), gmat.astype(bf16)


def _prep_weights(w_in, mla_q_norm_g, mla_w_uq, mla_kv_norm_g, mla_w_ukv, hy_conv_w, hy_conv_b, hy_w1, hy_b1,
                  hy_freq, hy_w2, hy_b2, hy_w3, hy_b3, hy_skip, df_subln_g, w_br_a, w_br_b, w_br_c, w_out,
                  w_fc1, w_fc2):
    depth = w_in.shape[0]
    o = [0]
    for n in (MLA_Q_RANK, MLA_KV_RANK, MLA_ROPE, 3 * HY_WIDTH, 2 * DF_HEADS * DF_DIM, 2 * DF_HEADS * DF_DIM,
              DF_OUT, N_BRANCH * D_MODEL):
        o.append(o[-1] + n)
    half = MLA_ROPE // 2
    w_kr = w_in[..., o[2]:o[3]]
    kr_pad = jnp.zeros((depth, D_MODEL, LANES - 2 * MLA_ROPE), f32)
    w_in1 = jnp.concatenate([w_in[..., o[0]:o[3]], w_kr[..., half:], w_kr[..., :half], kr_pad,
                             w_in[..., o[3]:o[7]]], axis=-1).astype(bf16)
    w_gate = w_in[..., o[7]:o[8]].astype(bf16)

    uq = mla_w_uq.reshape(depth, MLA_Q_RANK, MLA_HEADS, MLA_NOPE + MLA_ROPE)
    nope, rope = uq[..., :MLA_NOPE], uq[..., MLA_NOPE:]
    zn = jnp.zeros_like(nope)
    zp = jnp.zeros(uq.shape[:3] + (LANES - MLA_NOPE - MLA_ROPE,), f32)
    wq = jnp.concatenate([nope, rope, zp], axis=-1).reshape(depth, MLA_Q_RANK, MLA_HEADS * LANES)
    wqs = jnp.concatenate([zn, rope[..., half:], rope[..., :half], zp], axis=-1)
    wqs = wqs.reshape(depth, MLA_Q_RANK, MLA_HEADS * LANES)
    ukv = mla_w_ukv.reshape(depth, MLA_KV_RANK, MLA_HEADS, MLA_NOPE + MLA_V)
    wk = jnp.concatenate([ukv[..., :MLA_NOPE], jnp.zeros_like(ukv[..., MLA_NOPE:])], axis=-1)
    wk = wk.reshape(depth, MLA_KV_RANK, MLA_HEADS * LANES)
    wv = ukv[..., MLA_NOPE:].reshape(depth, MLA_KV_RANK, MLA_OUT)

    i = jnp.arange(LANES)[:, None]
    js = jnp.arange(MLA_HEADS * LANES)[None, :] % LANES - MLA_NOPE
    e = ((i < MLA_ROPE) & (js == i)).astype(f32)
    e_rope = ((i < 2 * MLA_ROPE) & (js == i % MLA_ROPE)).astype(f32)
    wke = jnp.concatenate([wk, jnp.broadcast_to(e, (depth,) + e.shape)], axis=1).astype(bf16)
    wke_rope = jnp.concatenate([wk, jnp.broadcast_to(e_rope, (depth,) + e.shape)], axis=1).astype(bf16)
    a = jnp.arange(2 * DF_HEADS * DF_DIM)
    dh = DF_DIM // 2
    swap = (a // DF_DIM) * DF_DIM + (a % DF_DIM + dh) % DF_DIM
    p = (a[:, None] == swap[None, :]).astype(bf16)

    pad_rows = lambda w, r: jnp.pad(w, ((0, 0), (0, r - w.shape[1]), (0, 0)))
    pad_cols = lambda w, c: jnp.pad(w, ((0, 0),) * (w.ndim - 1) + ((0, c - w.shape[-1]),))
    layers = []
    for l in range(depth):
        layers.append({
            'w_in1': w_in1[l], 'w_gate': w_gate[l],
            'q_norm_g': mla_q_norm_g[l][None], 'kv_norm_g': mla_kv_norm_g[l][None],
            'wq': wq[l].astype(bf16), 'wqs': wqs[l].astype(bf16), 'wke': wke[l], 'wke_rope': wke_rope[l],
            'wv': wv[l].astype(bf16), 'p': p,
            'hy_conv_w': hy_conv_w[l], 'hy_conv_b': hy_conv_b[l][None], 'hy_skip': hy_skip[l][None],
            'subln_g': jnp.broadcast_to(df_subln_g[l][:, None], (DF_V, LANES)),
            'w_br_a': w_br_a[l].astype(bf16), 'w_br_b': w_br_b[l].astype(bf16),
            'w_br_c': w_br_c[l].astype(bf16), 'w_out': w_out[l].astype(bf16),
            'w_fc1': w_fc1[l].astype(bf16), 'w_fc2': w_fc2[l].astype(bf16),
        })
    hy_filter_w = [pad_cols(pad_rows(hy_w1, LANES), LANES), pad_cols(hy_b1, LANES)[:, None],
                   pad_cols(hy_freq, LANES)[:, None], pad_cols(pad_rows(hy_w2, LANES), LANES),
                   pad_cols(hy_b2, LANES)[:, None], pad_rows(hy_w3, LANES), hy_b3[:, None]]
    return layers, hy_filter_w


def _token_mixers(lat, ctx, n_lat, n_ctx, lw, lams, lam_init, hy_lat, hy_ctx, with_ctx):
    qm, _, _, qd, _, _, hy = lat
    cqm, km, vm, cqd, kd, vd, chy = ctx
    n_all = n_lat + n_ctx
    ya = _mla_attn(qm, km, vm, n_all, 0)
    yc = _diff_attn(lams, lw['subln_g'], qd, kd, vd, n_all, 0, lam_init)
    yb = _hy_conv(hy, lw, *hy_lat)
    if not with_ctx:
        return (ya, yb, yc), None
    ya_c = _mla_attn(cqm, km, vm, n_ctx, n_lat // n_ctx)
    yc_c = _diff_attn(lams, lw['subln_g'], cqd, kd, vd, n_ctx, n_lat // n_ctx, lam_init)
    yb_c = _hy_conv(chy, lw, *hy_ctx)
    return (ya, yb, yc), (ya_c, yb_c, yc_c)


def kernel(x, c, ctx, c_ctx, norm_mix_g, norm_ffn_g, w_mod, b_mod, w_in, mla_q_norm_g, mla_w_uq, mla_kv_norm_g, mla_w_ukv, hy_conv_w, hy_conv_b, hy_w1, hy_b1, hy_freq, hy_w2, hy_b2, hy_w3, hy_b3, hy_skip, df_lq1, df_lk1, df_lq2, df_lk2, df_subln_g, w_br_a, w_br_b, w_br_c, w_out, w_fc1, w_fc2, final_norm_g):
    b, n_lat, d = x.shape
    n_ctx = ctx.shape[1]
    assert n_lat % n_ctx == 0 and n_ctx % TQ == 0
    depth = w_in.shape[0]
    layers, hy_filter_w = _prep_weights(w_in, mla_q_norm_g, mla_w_uq, mla_kv_norm_g, mla_w_ukv, hy_conv_w, hy_conv_b,
                           hy_w1, hy_b1, hy_freq, hy_w2, hy_b2, hy_w3, hy_b3, hy_skip, df_subln_g,
                           w_br_a, w_br_b, w_br_c, w_out, w_fc1, w_fc2)
    tables = _rope_tables(n_lat)
    z_lat, win_lat = _hy_positional(n_lat)
    z_ctx, win_ctx = _hy_positional(n_ctx)
    f_lat, g_lat = _dft_tables(n_lat)
    f_ctx, g_ctx = _dft_tables(n_ctx)
    spec_lat = _hy_filter(z_lat, hy_filter_w, win_lat, f_lat)
    spec_ctx = _hy_filter(z_ctx, hy_filter_w, win_ctx, f_ctx)

    rows = -(-(b + 1) // 8) * 8
    cc = jnp.concatenate([c, c_ctx[None], jnp.zeros((rows - b - 1, d), f32)], axis=0)
    mod = _modulation(cc, w_mod, b_mod).reshape(depth, rows, 6, d)

    xc = ctx
    for l in range(depth):
        lw = layers[l]
        with_ctx = l < depth - 1
        lam_init = 0.8 - 0.6 * math.exp(-0.3 * l)
        mod_lat, mod_ctx = mod[l, :b], mod[l, b:b + 1]
        g_mix, g_ffn = norm_mix_g[l][None], norm_ffn_g[l][None]
        lams = [df_lq1[l][None], df_lk1[l][None], df_lq2[l][None], df_lk2[l][None]]

        hy_lat = ([a[l] for a in spec_lat], f_lat, g_lat)
        hy_ctx = ([a[l] for a in spec_ctx], f_ctx, g_ctx) if with_ctx else None

        n_all = n_lat + n_ctx
        parts = _in_proj(x, mod_lat, True, g_mix, lw, tables, n_all, 0)
        kv_bufs = [parts[i] for i in (1, 2, 4, 5)]
        parts_c = _in_proj(xc, mod_ctx, False, g_mix, lw, None, n_all, n_lat // n_ctx, kv_bufs)
        ys, ys_c = _token_mixers(parts, parts_c, n_lat, n_ctx, lw, lams, lam_init, hy_lat, hy_ctx, with_ctx)

        x = _merge(x, mod_lat, True, g_mix, *ys, lw)
        x = _mlp(x, mod_lat, True, g_ffn, lw['w_fc1'], lw['w_fc2'],
                 final_norm_g[None] if l == depth - 1 else None)
        if with_ctx:
            xc = _merge(xc, mod_ctx, False, g_mix, *ys_c, lw)
            xc = _mlp(xc, mod_ctx, False, g_ffn, lw['w_fc1'], lw['w_fc2'])
    return x
```

```python
import functools
import math

import jax
import jax.numpy as jnp
from jax import lax
from jax.experimental import pallas as pl
from jax.experimental.pallas import tpu as pltpu

f32 = jnp.float32
bf16 = jnp.bfloat16

D_MODEL = 1024
GRID_W = 64
EPS = 1e-6
ROPE_BASE = 10000.0

MLA_HEADS = 8
MLA_NOPE = 64
MLA_ROPE = 32
MLA_V = 64
MLA_Q_RANK = 256
MLA_KV_RANK = 128
MLA_OUT = MLA_HEADS * MLA_V
MLA_SCALE = (MLA_NOPE + MLA_ROPE) ** -0.5

HY_WIDTH = 256
HY_EMB = 33
HY_BANDS = (HY_EMB - 1) // 2
HY_TARGET = 1e-2
HY_FAST = 0.3
HY_SLOW = 1.5
HY_SHIFT = 0.05

DF_HEADS = 4
DF_DIM = 32
DF_V = 2 * DF_DIM
DF_OUT = DF_HEADS * DF_V
DF_SCALE = DF_DIM ** -0.5

N_BRANCH = 3

LANES = 128
BF16_ROWS = 16
MXU_TILE = 256
KEY_CHUNK = 1152
TQ = MXU_TILE
MAPS_PER_TILE = 8
TOKEN_TILE = 1024
STAGED_SUB_TILES = 2
LOG2E = 1.4426950408889634
VMEM_LIMIT = 56 * 1024 * 1024

C_Q, C_KV, C_KR, C_HY, C_DQ, C_DK, C_DV, C_END = 0, 256, 384, 512, 1280, 1536, 1792, 2048


def _cparams(*sem):
    return pltpu.CompilerParams(dimension_semantics=sem, vmem_limit_bytes=VMEM_LIMIT)


def _dot(a, b):
    return jnp.dot(a, b, preferred_element_type=f32)


def _rms(x):
    return x * lax.rsqrt(jnp.mean(x * x, axis=-1, keepdims=True) + EPS)


def _shr(n, k):
    return n >> k if isinstance(n, int) else lax.shift_right_logical(n, k)


def _aligned(x, m):
    return x if isinstance(x, int) else pl.multiple_of(x, m)


def _const_spec(shape):
    nd = len(shape)
    return pl.BlockSpec(shape, lambda *_: (0,) * nd)


def _resident_spec(shape):
    nd = len(shape)
    return pl.BlockSpec(shape, lambda *_: (0,) * nd, pipeline_mode=pl.Buffered(1))


def _sub_tile_groups(n_rows):
    subs = [slice(TQ * t, TQ * (t + 1)) for t in range(n_rows // TQ)]
    return [subs[i:i + STAGED_SUB_TILES] for i in range(0, len(subs), STAGED_SUB_TILES)]


def _mod_kernel(c_ref, w_ref, b_ref, o_ref):
    c = c_ref[...]
    s = c * jax.nn.sigmoid(c)
    o_ref[0] = _dot(s.astype(bf16), w_ref[0].astype(bf16)) + b_ref[0]


def _modulation(cc, w_mod, b_mod):
    depth, d, n = w_mod.shape
    r = cc.shape[0]
    tn = 2048
    return pl.pallas_call(
        _mod_kernel,
        grid=(depth, n // tn),
        in_specs=[pl.BlockSpec((r, d), lambda l, j: (0, 0)),
                  pl.BlockSpec((1, d, tn), lambda l, j: (l, 0, j)),
                  pl.BlockSpec((1, 1, tn), lambda l, j: (l, 0, j))],
        out_specs=pl.BlockSpec((1, r, tn), lambda l, j: (l, 0, j)),
        out_shape=jax.ShapeDtypeStruct((depth, r, n), f32),
        compiler_params=_cparams("parallel", "parallel"),
        name="modulation",
    )(cc, w_mod, b_mod.reshape(depth, 1, n))


def _in_proj_kernel(*refs, rope, n_alias):
    (x_ref, mod_ref, g_ref, w_ref, qg_ref, kvg_ref, wq_ref, wqs_ref, wke_ref, wv_ref, p_ref) = refs[:11]
    refs = refs[11:]
    if rope:
        cm_ref, sm_ref, kt_ref, cd_ref, sd_ref = refs[:5]
        refs = refs[5:]
    qm_ref, km_ref, vm_ref, qd_ref, kd_ref, vd_ref, hy_ref = refs[n_alias:]

    md = mod_ref[0]

    def project(rows):
        h = _rms(x_ref[0, rows, :]) * g_ref[...] * (1.0 + md[1:2]) + md[0:1]
        return _dot(h.astype(bf16), w_ref[...])

    def finish(rows, p):
        t = rows.start // TQ
        qn = (_rms(p[:, C_Q:C_KV]) * qg_ref[...]).astype(bf16)
        kvn = (_rms(p[:, C_KV:C_KR]) * kvg_ref[...]).astype(bf16)
        kr = p[:, C_KR:C_HY]
        q = _dot(qn, wq_ref[...])
        dq = p[:, C_DQ:C_DK]
        dk = p[:, C_DK:C_DV]
        if rope:
            kr = kr * kt_ref[rows, :]
            cm = jnp.concatenate([cm_ref[rows, :]] * MLA_HEADS, axis=-1)
            sm = jnp.concatenate([sm_ref[rows, :]] * MLA_HEADS, axis=-1)
            q = q * cm + _dot(qn, wqs_ref[...]) * sm
            cd = jnp.concatenate([cd_ref[rows, :]] * 2, axis=-1)
            sd = jnp.concatenate([sd_ref[rows, :]] * 2, axis=-1)
            dq = dq * cd + _dot(dq.astype(bf16), p_ref[...]) * sd
            dk = dk * cd + _dot(dk.astype(bf16), p_ref[...]) * sd
        k = _dot(jnp.concatenate([kvn, kr.astype(bf16)], axis=-1), wke_ref[...]).astype(bf16)
        dk = dk.astype(bf16)
        qm_ref[0, t] = (q * (MLA_SCALE * LOG2E)).T.astype(bf16)
        for h in range(km_ref.shape[1]):
            km_ref[0, h, rows, :] = k[:, LANES * h:LANES * (h + 1)]
        vm_ref[0, :, rows] = _dot(kvn, wv_ref[...]).T.astype(bf16)
        qd_ref[0, t] = (dq * (DF_SCALE * LOG2E)).T.astype(bf16)
        for h in range(kd_ref.shape[1]):
            kd_ref[0, h, rows, :] = dk[:, LANES * h:LANES * (h + 1)]
        vd_ref[0, :, rows] = p[:, C_DV:C_END].T.astype(bf16)
        hy_ref[0, rows, :] = p[:, C_HY:C_DQ].astype(bf16)

    for group in _sub_tile_groups(x_ref.shape[1]):
        projected = [project(rows) for rows in group]
        for rows, p in zip(group, projected):
            finish(rows, p)


def _in_proj(x, mod, per_batch_mod, g, lw, tables, n_keys, key_block, kv_bufs=None):
    b, lx, d = x.shape
    tm = min(TOKEN_TILE, lx)
    rope = tables is not None
    kb = key_block * (lx // tm)
    tok = lambda n: pl.BlockSpec((1, tm, n), lambda i, j: (j, i, 0))
    mod_map = (lambda i, j: (j, 0, 0)) if per_batch_mod else (lambda i, j: (0, 0, 0))
    weights = [lw['w_in1'], lw['q_norm_g'], lw['kv_norm_g'], lw['wq'], lw['wqs'],
               lw['wke_rope' if rope else 'wke'], lw['wv'], lw['p']]
    in_specs = ([tok(d), pl.BlockSpec((1, 6, d), mod_map), _const_spec(g.shape)]
                + [_resident_spec(w.shape) for w in weights])
    args = [x, mod, g] + weights
    if rope:
        in_specs += [pl.BlockSpec((tm, LANES), lambda i, j: (i, 0))] * len(tables)
        args += list(tables)
    aliases = {}
    if kv_bufs is not None:
        aliases = {len(args) + n: o for n, o in enumerate((1, 2, 4, 5))}
        in_specs += [pl.BlockSpec(memory_space=pl.ANY)] * len(kv_bufs)
        args += list(kv_bufs)
    f_q, f_d = MLA_HEADS * LANES, 2 * DF_HEADS * DF_DIM
    qtile = lambda n: pl.BlockSpec((1, tm // TQ, n, TQ), lambda i, j: (j, i, 0, 0))
    keys = lambda n: pl.BlockSpec((1, n // LANES, tm, LANES), lambda i, j: (j, 0, kb + i, 0))
    vals = lambda n: pl.BlockSpec((1, n, tm), lambda i, j: (j, 0, kb + i))
    sds = lambda *shape: jax.ShapeDtypeStruct(shape, bf16)
    out_specs = [qtile(f_q), keys(f_q), vals(MLA_OUT), qtile(f_d), keys(f_d), vals(DF_OUT), tok(3 * HY_WIDTH)]
    out_shape = [sds(b, lx // TQ, f_q, TQ), sds(b, f_q // LANES, n_keys, LANES), sds(b, MLA_OUT, n_keys),
                 sds(b, lx // TQ, f_d, TQ), sds(b, f_d // LANES, n_keys, LANES), sds(b, DF_OUT, n_keys),
                 sds(b, lx, 3 * HY_WIDTH)]
    return pl.pallas_call(
        functools.partial(_in_proj_kernel, rope=rope, n_alias=len(aliases)),
        grid=(lx // tm, b),
        in_specs=in_specs,
        out_specs=out_specs,
        out_shape=out_shape,
        input_output_aliases=aliases,
        compiler_params=_cparams("parallel", "parallel"),
        name="in_proj",
    )(*args)


def _softmax_maps(n_maps, qt_of, k_of, vt_of, s_scr, o_scr):
    _, lk, tq = s_scr.shape
    dv = o_scr.shape[0] // n_maps
    kc = KEY_CHUNK if lk % KEY_CHUNK == 0 else MXU_TILE
    assert lk % kc == 0
    ones = jnp.ones((BF16_ROWS, kc), bf16)

    def scores(n, slot):
        qt = qt_of(n)
        pm = None
        for off in range(0, lk, kc):
            s = _dot(k_of(n, off, kc), qt)
            s_scr[slot, off:off + kc, :] = s
            cm = jnp.max(s.reshape(kc // 8, 8, tq), axis=0)
            pm = cm if pm is None else jnp.maximum(pm, cm)
        return jnp.max(pm, axis=0, keepdims=True)

    def values(n, slot, m):
        acc = None
        for off in range(0, lk, kc):
            p = jnp.exp2(s_scr[slot, off:off + kc, :] - m).astype(bf16)
            part = _dot(jnp.concatenate([vt_of(n, off, kc), ones], axis=0), p)
            acc = part if acc is None else acc + part
        o_scr[pl.ds(_aligned(n * dv, dv), dv), :] = acc[:dv] / acc[dv:dv + 1]

    if lk == kc and n_maps <= MAPS_PER_TILE:
        qks = [_dot(k_of(n, 0, kc), qt_of(n)) for n in range(n_maps)]
        col_max = lambda s: jnp.max(jnp.max(s.reshape(kc // 8, 8, tq), axis=0), axis=0, keepdims=True)
        pts = [jnp.exp2(s - col_max(s)).astype(bf16) for s in qks]
        for n, p in enumerate(pts):
            acc = _dot(jnp.concatenate([vt_of(n, 0, kc), ones], axis=0), p)
            o_scr[n * dv:(n + 1) * dv, :] = acc[:dv] / acc[dv:dv + 1]
        return

    def group(base, m, count):
        for e in range(count):
            m_next = scores(base + e + 1, (e + 1) & 1)
            values(base + e, e & 1, m)
            m = m_next
        return m

    per_iter = 16 if n_maps >= 64 else (4 if n_maps >= 16 else 2)
    trips = (n_maps - 1) // per_iter
    m = lax.fori_loop(0, trips, lambda i, m: group(i * per_iter, m, per_iter), scores(0, 0))
    m = group(trips * per_iter, m, n_maps - 1 - trips * per_iter)
    values(n_maps - 1, (n_maps - 1) & 1, m)


def _mla_attn_kernel(qt_ref, k_ref, vt_ref, o_ref, s_scr, o_scr):
    n_tiles = qt_ref.shape[1]
    tile_of = lambda n: _shr(n, 3)
    head_of = lambda n: n & (MAPS_PER_TILE - 1)
    _softmax_maps(
        n_tiles * MLA_HEADS,
        lambda n: qt_ref[0, tile_of(n), pl.ds(_aligned(head_of(n) * LANES, LANES), LANES), :],
        lambda n, off, kc: k_ref[0, head_of(n), off:off + kc, :],
        lambda n, off, kc: vt_ref[0, pl.ds(_aligned(head_of(n) * MLA_V, MLA_V), MLA_V), off:off + kc],
        s_scr, o_scr)
    for t in range(n_tiles):
        for j in range(MLA_OUT // LANES):
            r = t * MLA_OUT + LANES * j
            o_ref[0, TQ * t:TQ * (t + 1), LANES * j:LANES * (j + 1)] = o_scr[r:r + LANES, :].T.astype(bf16)


def _diff_attn_kernel(lq1, lk1, lq2, lk2, g_ref, qt_ref, k_ref, vt_ref, o_ref, s_scr, o_scr, *, lam_init):
    n_tiles = qt_ref.shape[1]
    row = lax.broadcasted_iota(jnp.int32, (LANES, TQ), 0)
    per_lanes = LANES // DF_DIM
    tile_of = lambda n: _shr(n, 3)
    map_of = lambda n: n & (MAPS_PER_TILE - 1)
    lanes_of = lambda n: _shr(map_of(n), 2)

    def qt_of(n):
        lo = DF_DIM * (n & (per_lanes - 1))
        qt = qt_ref[0, tile_of(n), pl.ds(_aligned(lanes_of(n) * LANES, LANES), LANES), :]
        return jnp.where(row < lo, 0.0, jnp.where(row < lo + DF_DIM, qt, 0.0)).astype(bf16)

    def vt_of(n, off, kc):
        h = _shr(map_of(n), 1)
        return vt_ref[0, pl.ds(_aligned(h * DF_V, DF_V), DF_V), off:off + kc]

    _softmax_maps(n_tiles * 2 * DF_HEADS, qt_of, lambda n, off, kc: k_ref[0, lanes_of(n), off:off + kc, :], vt_of,
                  s_scr, o_scr)

    lam = (jnp.exp(jnp.sum(lq1[...] * lk1[...], axis=-1, keepdims=True))
           - jnp.exp(jnp.sum(lq2[...] * lk2[...], axis=-1, keepdims=True)) + lam_init)
    g = jnp.concatenate([g_ref[...]] * (TQ // LANES), axis=-1)
    for t in range(n_tiles):
        for j in range(DF_OUT // LANES):
            pair = []
            for h in (2 * j, 2 * j + 1):
                r = (t * MAPS_PER_TILE + 2 * h) * DF_V
                o = o_scr[r:r + DF_V, :] - lam * o_scr[r + DF_V:r + 2 * DF_V, :]
                ms = jnp.mean(o * o, axis=0, keepdims=True)
                pair.append(o * lax.rsqrt(ms + EPS) * g * (1.0 - lam_init))
            o_ref[0, TQ * t:TQ * (t + 1), LANES * j:LANES * (j + 1)] = (
                jnp.concatenate(pair, axis=0).T.astype(bf16))


def _attn_call(kernel, name, small, qt, k_all, vt_all, n_keys, key_block, dv):
    b, n_qt, f, _ = qt.shape
    f_out = vt_all.shape[1]
    tiles = next(t for t in (8, 4, 2, 1) if n_qt % t == 0)
    scratch = [pltpu.VMEM((2, n_keys, TQ), f32), pltpu.VMEM((tiles * MAPS_PER_TILE * dv, TQ), f32)]
    return pl.pallas_call(
        kernel,
        grid=(b, n_qt // tiles),
        in_specs=([_const_spec(a.shape) for a in small]
                  + [pl.BlockSpec((1, tiles, f, TQ), lambda i, j: (i, j, 0, 0)),
                     pl.BlockSpec((1, f // LANES, n_keys, LANES), lambda i, j: (i, 0, key_block, 0)),
                     pl.BlockSpec((1, f_out, n_keys), lambda i, j: (i, 0, key_block))]),
        out_specs=pl.BlockSpec((1, tiles * TQ, f_out), lambda i, j: (i, j, 0)),
        out_shape=jax.ShapeDtypeStruct((b, n_qt * TQ, f_out), bf16),
        scratch_shapes=scratch,
        compiler_params=_cparams("parallel", "parallel"),
        name=name,
    )(*small, qt, k_all, vt_all)


def _mla_attn(q, k_all, vt_all, n_keys, key_block):
    return _attn_call(_mla_attn_kernel, "mla_attn", [], q, k_all, vt_all, n_keys, key_block, MLA_V)


def _diff_attn(lams, g, q, k_all, vt_all, n_keys, key_block, lam_init):
    return _attn_call(functools.partial(_diff_attn_kernel, lam_init=lam_init), "diff_attn",
                      list(lams) + [g], q, k_all, vt_all, n_keys, key_block, DF_V)


def _hy_filter_kernel(z_ref, w1_ref, b1_ref, fr_ref, w2_ref, b2_ref, w3_ref, b3_ref, win_ref,
                      fc_ref, fs_ref, a_ref, a2_ref, bq_ref, h_scr):
    kt = pl.program_id(1)
    hp = lax.Precision.HIGHEST

    @pl.when(kt == 0)
    def _():
        fr = fr_ref[0]
        a = jnp.sin(fr * (jnp.dot(z_ref[...], w1_ref[0], precision=hp, preferred_element_type=f32) + b1_ref[0]))
        a = jnp.sin(fr * (jnp.dot(a, w2_ref[0], precision=hp, preferred_element_type=f32) + b2_ref[0]))
        h = jnp.dot(a, w3_ref[0], precision=hp, preferred_element_type=f32) + b3_ref[0]
        win = win_ref[...]
        row = lax.broadcasted_iota(jnp.int32, win.shape, 0)
        h_scr[:, :HY_WIDTH] = (h[:, :HY_WIDTH] * win).astype(bf16)
        h_scr[:, HY_WIDTH:] = jnp.where(row == 0, 0.0, h[:, HY_WIDTH:] * win).astype(bf16)

    hh = h_scr[...]
    c = _dot(fc_ref[...], hh)
    s = _dot(fs_ref[...], hh)
    tk = c.shape[0]
    row = lax.broadcasted_iota(jnp.int32, (tk, HY_WIDTH), 0) + kt * tk
    first = row == 0
    a = c[:, :HY_WIDTH] + c[:, HY_WIDTH:]
    second = jnp.where(first, s[:, :HY_WIDTH] + s[:, HY_WIDTH:], s[:, :HY_WIDTH] - s[:, HY_WIDTH:])
    a_ref[0] = a
    a2_ref[0] = jnp.where(first, second, a)
    bq_ref[0] = jnp.where(first, 0.0, second)


def _hy_filter(z, hw, win, fmat):
    l = z.shape[0]
    depth = hw[0].shape[0]
    tk = min(512, l)
    nk = l // tk
    per_layer = lambda a: pl.BlockSpec((1,) + a.shape[1:], lambda d, k: (d,) + (0,) * (a.ndim - 1))
    out = jax.ShapeDtypeStruct((depth, l, HY_WIDTH), f32)
    return pl.pallas_call(
        _hy_filter_kernel,
        grid=(depth, nk),
        in_specs=([_const_spec(z.shape)] + [per_layer(a) for a in hw] + [_const_spec(win.shape)]
                  + [pl.BlockSpec((tk, l), lambda d, k: (k, 0)), pl.BlockSpec((tk, l), lambda d, k: (k + nk, 0))]),
        out_specs=[pl.BlockSpec((1, tk, HY_WIDTH), lambda d, k: (d, k, 0))] * 3,
        out_shape=[out] * 3,
        scratch_shapes=[pltpu.VMEM((l, 2 * HY_WIDTH), bf16)],
        compiler_params=_cparams("parallel", "arbitrary"),
        name="hy_filter",
    )(z, *hw, win, fmat, fmat)


def _hy_conv_kernel(p_ref, cw_ref, cb_ref, skip_ref, fc_ref, fs_ref, gc_ref, gs_ref,
                    a_ref, a2_ref, bq_ref, o_ref, u_scr, ub_scr, x0_scr, acc_scr, *, bb):
    kt = pl.program_id(1)
    l = p_ref.shape[1]

    @pl.when(kt == 0)
    def _():
        row = lax.broadcasted_iota(jnp.int32, (l, 3 * HY_WIDTH), 0)
        cw = cw_ref[...]
        for i in range(bb):
            p = p_ref[i].astype(f32)
            prev = jnp.where(row == 0, 0.0, pltpu.roll(p, 1, axis=0))
            nxt = jnp.where(row == l - 1, 0.0, pltpu.roll(p, l - 1, axis=0))
            uc = cw[0:1] * prev + cw[1:2] * p + cw[2:3] * nxt + cb_ref[...]
            x0_scr[i] = uc[:, :HY_WIDTH]
            u = uc[:, 2 * HY_WIDTH:] * uc[:, HY_WIDTH:2 * HY_WIDTH]
            u_scr[i] = u
            ub_scr[i] = u.astype(bf16)
            acc_scr[i] = jnp.zeros((l, HY_WIDTH), f32)

    a, a2, bq = a_ref[...], a2_ref[...], bq_ref[...]
    spectra = [(_dot(fc_ref[...], ub_scr[i]), _dot(fs_ref[...], ub_scr[i])) for i in range(bb)]
    for i, (ur, ui) in enumerate(spectra):
        zr = (ur * a - ui * bq).astype(bf16)
        zi = (ur * bq + ui * a2).astype(bf16)
        acc_scr[i] += _dot(gc_ref[...], zr) + _dot(gs_ref[...], zi)

    @pl.when(kt == pl.num_programs(1) - 1)
    def _():
        for i in range(bb):
            o_ref[i] = (x0_scr[i] * (acc_scr[i] + skip_ref[...] * u_scr[i])).astype(bf16)


def _hy_conv(p_hy, lw, spec, fmat, gmat):
    b, l, _ = p_hy.shape
    bb = 2
    tk = min(512, l)
    nk = l // tk
    a, a2, bq = spec
    return pl.pallas_call(
        functools.partial(_hy_conv_kernel, bb=bb),
        grid=(b // bb, nk),
        in_specs=[pl.BlockSpec((bb, l, 3 * HY_WIDTH), lambda i, k: (i, 0, 0)),
                  _const_spec(lw['hy_conv_w'].shape), _const_spec(lw['hy_conv_b'].shape),
                  _const_spec(lw['hy_skip'].shape),
                  pl.BlockSpec((tk, l), lambda i, k: (k, 0)), pl.BlockSpec((tk, l), lambda i, k: (k + nk, 0)),
                  pl.BlockSpec((l, tk), lambda i, k: (0, k)), pl.BlockSpec((l, tk), lambda i, k: (0, k + nk))]
                 + [pl.BlockSpec((tk, HY_WIDTH), lambda i, k: (k, 0))] * 3,
        out_specs=pl.BlockSpec((bb, l, HY_WIDTH), lambda i, k: (i, 0, 0)),
        out_shape=jax.ShapeDtypeStruct((b, l, HY_WIDTH), bf16),
        scratch_shapes=[pltpu.VMEM((bb, l, HY_WIDTH), f32), pltpu.VMEM((bb, l, HY_WIDTH), bf16),
                        pltpu.VMEM((bb, l, HY_WIDTH), f32), pltpu.VMEM((bb, l, HY_WIDTH), f32)],
        compiler_params=_cparams("parallel", "arbitrary"),
        name="hy_conv",
    )(p_hy, lw['hy_conv_w'], lw['hy_conv_b'], lw['hy_skip'], fmat, fmat, gmat, gmat, a, a2, bq)


def _merge_kernel(x_ref, mod_ref, g_ref, ya_ref, yb_ref, yc_ref, wg_ref, wa_ref, wb_ref, wc_ref, wo_ref,
                  o_ref):
    md = mod_ref[0]
    d = D_MODEL
    for group in _sub_tile_groups(x_ref.shape[1]):
        staged = []
        for rows in group:
            h = (_rms(x_ref[0, rows, :]) * g_ref[...] * (1.0 + md[1:2]) + md[0:1]).astype(bf16)
            gates = [_dot(h, wg_ref[:, d * i:d * (i + 1)]) for i in range(N_BRANCH)]
            branches = [_dot(ya_ref[0, rows, :], wa_ref[...]), _dot(yb_ref[0, rows, :], wb_ref[...]),
                        _dot(yc_ref[0, rows, :], wc_ref[...])]
            staged.append((gates, branches))
        for rows, (gates, branches) in zip(group, staged):
            m = functools.reduce(jnp.add, [jax.nn.sigmoid(gt) * br for gt, br in zip(gates, branches)])
            o_ref[0, rows, :] = x_ref[0, rows, :] + md[2:3] * _dot(m.astype(bf16), wo_ref[...])


def _merge(x, mod, per_batch_mod, g, ya, yb, yc, lw):
    b, lx, d = x.shape
    tm = min(TOKEN_TILE, lx)
    tok = lambda n: pl.BlockSpec((1, tm, n), lambda i, j: (i, j, 0))
    mod_map = (lambda i, j: (i, 0, 0)) if per_batch_mod else (lambda i, j: (0, 0, 0))
    weights = [lw['w_gate'], lw['w_br_a'], lw['w_br_b'], lw['w_br_c'], lw['w_out']]
    return pl.pallas_call(
        _merge_kernel,
        grid=(b, lx // tm),
        in_specs=[tok(d), pl.BlockSpec((1, 6, d), mod_map), _const_spec(g.shape),
                  tok(MLA_OUT), tok(HY_WIDTH), tok(DF_OUT)] + [_resident_spec(w.shape) for w in weights],
        out_specs=tok(d),
        out_shape=jax.ShapeDtypeStruct(x.shape, f32),
        compiler_params=_cparams("parallel", "parallel"),
        name="merge",
    )(x, mod, g, ya, yb, yc, *weights)


def _mlp_kernel(*refs, final):
    x_ref, mod_ref, g_ref, w1_ref, w2_ref = refs[:5]
    o_ref = refs[-1]
    md = mod_ref[0]
    for group in _sub_tile_groups(x_ref.shape[1]):
        hidden = []
        for rows in group:
            h = (_rms(x_ref[0, rows, :]) * g_ref[...] * (1.0 + md[4:5]) + md[3:4]).astype(bf16)
            hidden.append(_dot(h, w1_ref[...]))
        for rows, a in zip(group, hidden):
            a = jnp.maximum(a, 0.0)
            y = x_ref[0, rows, :] + md[5:6] * _dot((a * a).astype(bf16), w2_ref[...])
            if final:
                y = _rms(y) * refs[5][...]
            o_ref[0, rows, :] = y


def _mlp(x, mod, per_batch_mod, g, w1, w2, final_g=None):
    b, lx, d = x.shape
    tm = min(TOKEN_TILE, lx)
    tok = pl.BlockSpec((1, tm, d), lambda i, j: (i, j, 0))
    mod_map = (lambda i, j: (i, 0, 0)) if per_batch_mod else (lambda i, j: (0, 0, 0))
    final = final_g is not None
    in_specs = [tok, pl.BlockSpec((1, 6, d), mod_map), _const_spec(g.shape),
                _resident_spec(w1.shape), _resident_spec(w2.shape)]
    args = [x, mod, g, w1, w2]
    if final:
        in_specs.append(_const_spec(final_g.shape))
        args.append(final_g)
    return pl.pallas_call(
        functools.partial(_mlp_kernel, final=final),
        grid=(b, lx // tm),
        in_specs=in_specs,
        out_specs=tok,
        out_shape=jax.ShapeDtypeStruct(x.shape, f32),
        compiler_params=_cparams("parallel", "parallel"),
        name="mlp",
    )(*args)


def _rope_tables(n_tokens):
    rows = n_tokens // GRID_W
    row = jnp.repeat(jnp.arange(rows), GRID_W).astype(f32)
    col = jnp.tile(jnp.arange(GRID_W), rows).astype(f32)
    nf = MLA_ROPE // 4
    inv = ROPE_BASE ** (-jnp.arange(nf, dtype=f32) / nf)
    ang = jnp.concatenate([row[:, None] * inv, col[:, None] * inv], axis=-1)
    cos, sin = jnp.cos(ang), jnp.sin(ang)
    one = jnp.ones((n_tokens, MLA_NOPE), f32)
    zero = jnp.zeros((n_tokens, MLA_NOPE), f32)
    pad = jnp.zeros((n_tokens, LANES - MLA_NOPE - MLA_ROPE), f32)
    cm = jnp.concatenate([one, cos, cos, pad], axis=-1)
    sm = jnp.concatenate([zero, -sin, sin, pad], axis=-1)
    kt = jnp.concatenate([cos, cos, -sin, sin, jnp.zeros((n_tokens, LANES - 2 * MLA_ROPE), f32)], axis=-1)
    cd = jnp.tile(jnp.concatenate([cos, cos], axis=-1), (1, LANES // DF_DIM))
    sd = jnp.tile(jnp.concatenate([-sin, sin], axis=-1), (1, LANES // DF_DIM))
    return cm, sm, kt, cd, sd


def _hy_positional(l):
    t = jnp.linspace(0.0, 1.0, l, dtype=f32)[:, None]
    w = (2.0 * math.pi / l) * jnp.arange(l, dtype=f32)[:, None]
    bands = jnp.linspace(1e-4, HY_BANDS - 1, HY_BANDS, dtype=f32)[None]
    z = jnp.concatenate([t, jnp.cos(bands * w), -jnp.sin(bands * w)], axis=-1)
    z = jnp.pad(z, ((0, 0), (0, LANES - HY_EMB)))
    deltas = jnp.linspace(math.log(HY_TARGET) / HY_SLOW, math.log(HY_TARGET) / HY_FAST, HY_WIDTH, dtype=f32)
    win = jnp.exp(-t * jnp.abs(deltas)) + HY_SHIFT
    return z, win


def _dft_tables(l):
    n2 = 2 * l
    n = jnp.arange(l, dtype=jnp.int32)[None, :]
    a = jnp.arange(l // GRID_W, dtype=jnp.int32)[:, None]
    b = jnp.arange(GRID_W, dtype=jnp.int32)[:, None]
    ang_a = ((GRID_W * a * n) % n2).astype(f32) * (2.0 * math.pi / n2)
    ang_b = ((b * n) % n2).astype(f32) * (2.0 * math.pi / n2)
    ca, sa, cb, sb = jnp.cos(ang_a), jnp.sin(ang_a), jnp.cos(ang_b), jnp.sin(ang_b)
    nyq = jnp.where(n[0] % 2 == 0, 1.0, -1.0)
    c = (ca[:, None, :] * cb[None] - sa[:, None, :] * sb[None]).reshape(l, l)
    s = -(sa[:, None, :] * cb[None] + ca[:, None, :] * sb[None]).reshape(l, l)
    fmat = jnp.concatenate([c, s.at[0].set(nyq)], axis=0)
    wgt = jnp.full((l,), 2.0 / n2, f32).at[0].set(1.0 / n2)
    cat, sat, cbt, sbt = ca.T, sa.T, cb.T, sb.T
    ct = (cat[:, :, None] * cbt[:, None, :] - sat[:, :, None] * sbt[:, None, :]).reshape(l, l)
    st = -(sat[:, :, None] * cbt[:, None, :] + cat[:, :, None] * sbt[:, None, :]).reshape(l, l)
    gmat = jnp.concatenate([ct * wgt[None], st.at[:, 0].set(nyq) * wgt[None]], axis=1)
    return fmat.astype(bf16), gmat.astype(bf16)


def _prep_weights(w_in, mla_q_norm_g, mla_w_uq, mla_kv_norm_g, mla_w_ukv, hy_conv_w, hy_conv_b, hy_w1, hy_b1,
                  hy_freq, hy_w2, hy_b2, hy_w3, hy_b3, hy_skip, df_subln_g, w_br_a, w_br_b, w_br_c, w_out,
                  w_fc1, w_fc2):
    depth = w_in.shape[0]
    o = [0]
    for n in (MLA_Q_RANK, MLA_KV_RANK, MLA_ROPE, 3 * HY_WIDTH, 2 * DF_HEADS * DF_DIM, 2 * DF_HEADS * DF_DIM,
              DF_OUT, N_BRANCH * D_MODEL):
        o.append(o[-1] + n)
    half = MLA_ROPE // 2
    w_kr = w_in[..., o[2]:o[3]]
    kr_pad = jnp.zeros((depth, D_MODEL, LANES - 2 * MLA_ROPE), f32)
    w_in1 = jnp.concatenate([w_in[..., o[0]:o[3]], w_kr[..., half:], w_kr[..., :half], kr_pad,
                             w_in[..., o[3]:o[7]]], axis=-1).astype(bf16)
    w_gate = w_in[..., o[7]:o[8]].astype(bf16)

    uq = mla_w_uq.reshape(depth, MLA_Q_RANK, MLA_HEADS, MLA_NOPE + MLA_ROPE)
    nope, rope = uq[..., :MLA_NOPE], uq[..., MLA_NOPE:]
    zn = jnp.zeros_like(nope)
    zp = jnp.zeros(uq.shape[:3] + (LANES - MLA_NOPE - MLA_ROPE,), f32)
    wq = jnp.concatenate([nope, rope, zp], axis=-1).reshape(depth, MLA_Q_RANK, MLA_HEADS * LANES)
    wqs = jnp.concatenate([zn, rope[..., half:], rope[..., :half], zp], axis=-1)
    wqs = wqs.reshape(depth, MLA_Q_RANK, MLA_HEADS * LANES)
    ukv = mla_w_ukv.reshape(depth, MLA_KV_RANK, MLA_HEADS, MLA_NOPE + MLA_V)
    wk = jnp.concatenate([ukv[..., :MLA_NOPE], jnp.zeros_like(ukv[..., MLA_NOPE:])], axis=-1)
    wk = wk.reshape(depth, MLA_KV_RANK, MLA_HEADS * LANES)
    wv = ukv[..., MLA_NOPE:].reshape(depth, MLA_KV_RANK, MLA_OUT)

    i = jnp.arange(LANES)[:, None]
    js = jnp.arange(MLA_HEADS * LANES)[None, :] % LANES - MLA_NOPE
    e = ((i < MLA_ROPE) & (js == i)).astype(f32)
    e_rope = ((i < 2 * MLA_ROPE) & (js == i % MLA_ROPE)).astype(f32)
    wke = jnp.concatenate([wk, jnp.broadcast_to(e, (depth,) + e.shape)], axis=1).astype(bf16)
    wke_rope = jnp.concatenate([wk, jnp.broadcast_to(e_rope, (depth,) + e.shape)], axis=1).astype(bf16)
    a = jnp.arange(2 * DF_HEADS * DF_DIM)
    dh = DF_DIM // 2
    swap = (a // DF_DIM) * DF_DIM + (a % DF_DIM + dh) % DF_DIM
    p = (a[:, None] == swap[None, :]).astype(bf16)

    pad_rows = lambda w, r: jnp.pad(w, ((0, 0), (0, r - w.shape[1]), (0, 0)))
    pad_cols = lambda w, c: jnp.pad(w, ((0, 0),) * (w.ndim - 1) + ((0, c - w.shape[-1]),))
    layers = []
    for l in range(depth):
        layers.append({
            'w_in1': w_in1[l], 'w_gate': w_gate[l],
            'q_norm_g': mla_q_norm_g[l][None], 'kv_norm_g': mla_kv_norm_g[l][None],
            'wq': wq[l].astype(bf16), 'wqs': wqs[l].astype(bf16), 'wke': wke[l], 'wke_rope': wke_rope[l],
            'wv': wv[l].astype(bf16), 'p': p,
            'hy_conv_w': hy_conv_w[l], 'hy_conv_b': hy_conv_b[l][None], 'hy_skip': hy_skip[l][None],
            'subln_g': jnp.broadcast_to(df_subln_g[l][:, None], (DF_V, LANES)),
            'w_br_a': w_br_a[l].astype(bf16), 'w_br_b': w_br_b[l].astype(bf16),
            'w_br_c': w_br_c[l].astype(bf16), 'w_out': w_out[l].astype(bf16),
            'w_fc1': w_fc1[l].astype(bf16), 'w_fc2': w_fc2[l].astype(bf16),
        })
    hy_filter_w = [pad_cols(pad_rows(hy_w1, LANES), LANES), pad_cols(hy_b1, LANES)[:, None],
                   pad_cols(hy_freq, LANES)[:, None], pad_cols(pad_rows(hy_w2, LANES), LANES),
                   pad_cols(hy_b2, LANES)[:, None], pad_rows(hy_w3, LANES), hy_b3[:, None]]
    return layers, hy_filter_w


def _token_mixers(lat, ctx, n_lat, n_ctx, lw, lams, lam_init, hy_lat, hy_ctx, with_ctx):
    qm, _, _, qd, _, _, hy = lat
    cqm, km, vm, cqd, kd, vd, chy = ctx
    n_all = n_lat + n_ctx
    ya = _mla_attn(qm, km, vm, n_all, 0)
    yc = _diff_attn(lams, lw['subln_g'], qd, kd, vd, n_all, 0, lam_init)
    yb = _hy_conv(hy, lw, *hy_lat)
    if not with_ctx:
        return (ya, yb, yc), None
    ya_c = _mla_attn(cqm, km, vm, n_ctx, n_lat // n_ctx)
    yc_c = _diff_attn(lams, lw['subln_g'], cqd, kd, vd, n_ctx, n_lat // n_ctx, lam_init)
    yb_c = _hy_conv(chy, lw, *hy_ctx)
    return (ya, yb, yc), (ya_c, yb_c, yc_c)


def kernel(x, c, ctx, c_ctx, norm_mix_g, norm_ffn_g, w_mod, b_mod, w_in, mla_q_norm_g, mla_w_uq, mla_kv_norm_g, mla_w_ukv, hy_conv_w, hy_conv_b, hy_w1, hy_b1, hy_freq, hy_w2, hy_b2, hy_w3, hy_b3, hy_skip, df_lq1, df_lk1, df_lq2, df_lk2, df_subln_g, w_br_a, w_br_b, w_br_c, w_out, w_fc1, w_fc2, final_norm_g):
    b, n_lat, d = x.shape
    n_ctx = ctx.shape[1]
    assert n_lat % n_ctx == 0 and n_ctx % TQ == 0
    depth = w_in.shape[0]
    layers, hy_filter_w = _prep_weights(w_in, mla_q_norm_g, mla_w_uq, mla_kv_norm_g, mla_w_ukv, hy_conv_w, hy_conv_b,
                           hy_w1, hy_b1, hy_freq, hy_w2, hy_b2, hy_w3, hy_b3, hy_skip, df_subln_g,
                           w_br_a, w_br_b, w_br_c, w_out, w_fc1, w_fc2)
    tables = _rope_tables(n_lat)
    z_lat, win_lat = _hy_positional(n_lat)
    z_ctx, win_ctx = _hy_positional(n_ctx)
    f_lat, g_lat = _dft_tables(n_lat)
    f_ctx, g_ctx = _dft_tables(n_ctx)
    spec_lat = _hy_filter(z_lat, hy_filter_w, win_lat, f_lat)
    spec_ctx = _hy_filter(z_ctx, hy_filter_w, win_ctx, f_ctx)

    rows = -(-(b + 1) // 8) * 8
    cc = jnp.concatenate([c, c_ctx[None], jnp.zeros((rows - b - 1, d), f32)], axis=0)
    mod = _modulation(cc, w_mod, b_mod).reshape(depth, rows, 6, d)

    xc = ctx
    for l in range(depth):
        lw = layers[l]
        with_ctx = l < depth - 1
        lam_init = 0.8 - 0.6 * math.exp(-0.3 * l)
        mod_lat, mod_ctx = mod[l, :b], mod[l, b:b + 1]
        g_mix, g_ffn = norm_mix_g[l][None], norm_ffn_g[l][None]
        lams = [df_lq1[l][None], df_lk1[l][None], df_lq2[l][None], df_lk2[l][None]]

        hy_lat = ([a[l] for a in spec_lat], f_lat, g_lat)
        hy_ctx = ([a[l] for a in spec_ctx], f_ctx, g_ctx) if with_ctx else None

        n_all = n_lat + n_ctx
        parts = _in_proj(x, mod_lat, True, g_mix, lw, tables, n_all, 0)
        kv_bufs = [parts[i] for i in (1, 2, 4, 5)]
        parts_c = _in_proj(xc, mod_ctx, False, g_mix, lw, None, n_all, n_lat // n_ctx, kv_bufs)
        ys, ys_c = _token_mixers(parts, parts_c, n_lat, n_ctx, lw, lams, lam_init, hy_lat, hy_ctx, with_ctx)

        x = _merge(x, mod_lat, True, g_mix, *ys, lw)
        x = _mlp(x, mod_lat, True, g_ffn, lw['w_fc1'], lw['w_fc2'],
                 final_norm_g[None] if l == depth - 1 else None)
        if with_ctx:
            xc = _merge(xc, mod_ctx, False, g_mix, *ys_c, lw)
            xc = _mlp(xc, mod_ctx, False, g_ffn, lw['w_fc1'], lw['w_fc2'])
    return x
```

```python
import functools
import math

import jax
import jax.numpy as jnp
from jax import lax
from jax.experimental import pallas as pl
from jax.experimental.pallas import tpu as pltpu

f32 = jnp.float32
bf16 = jnp.bfloat16

D_MODEL = 1024
GRID_W = 64
EPS = 1e-6
ROPE_BASE = 10000.0

MLA_HEADS = 8
MLA_NOPE = 64
MLA_ROPE = 32
MLA_V = 64
MLA_Q_RANK = 256
MLA_KV_RANK = 128
MLA_OUT = MLA_HEADS * MLA_V
MLA_SCALE = (MLA_NOPE + MLA_ROPE) ** -0.5

HY_WIDTH = 256
HY_EMB = 33
HY_BANDS = (HY_EMB - 1) // 2
HY_TARGET = 1e-2
HY_FAST = 0.3
HY_SLOW = 1.5
HY_SHIFT = 0.05

DF_HEADS = 4
DF_DIM = 32
DF_V = 2 * DF_DIM
DF_OUT = DF_HEADS * DF_V
DF_SCALE = DF_DIM ** -0.5

N_BRANCH = 3

LANES = 128
BF16_ROWS = 16
MXU_TILE = 256
KEY_CHUNK = 1152
TQ = MXU_TILE
MAPS_PER_TILE = 8
TOKEN_TILE = 1024
FUSED_TOKEN_TILE = 512
STAGED_SUB_TILES = 2
LOG2E = 1.4426950408889634
VMEM_LIMIT = 56 * 1024 * 1024

C_Q, C_KV, C_KR, C_HY, C_DQ, C_DK, C_DV, C_END = 0, 256, 384, 512, 1280, 1536, 1792, 2048


def _cparams(*sem):
    return pltpu.CompilerParams(dimension_semantics=sem, vmem_limit_bytes=VMEM_LIMIT)


def _dot(a, b):
    return jnp.dot(a, b, preferred_element_type=f32)


def _rms(x):
    return x * lax.rsqrt(jnp.mean(x * x, axis=-1, keepdims=True) + EPS)


def _shr(n, k):
    return n >> k if isinstance(n, int) else lax.shift_right_logical(n, k)


def _aligned(x, m):
    return x if isinstance(x, int) else pl.multiple_of(x, m)


def _const_spec(shape):
    nd = len(shape)
    return pl.BlockSpec(shape, lambda *_: (0,) * nd)


def _resident_spec(shape):
    nd = len(shape)
    return pl.BlockSpec(shape, lambda *_: (0,) * nd, pipeline_mode=pl.Buffered(1))


def _sub_tile_groups(n_rows):
    subs = [slice(TQ * t, TQ * (t + 1)) for t in range(n_rows // TQ)]
    return [subs[i:i + STAGED_SUB_TILES] for i in range(0, len(subs), STAGED_SUB_TILES)]


def _mod_kernel(c_ref, w_ref, b_ref, o_ref):
    c = c_ref[...]
    s = c * jax.nn.sigmoid(c)
    o_ref[0] = _dot(s.astype(bf16), w_ref[0].astype(bf16)) + b_ref[0]


def _modulation(cc, w_mod, b_mod):
    depth, d, n = w_mod.shape
    r = cc.shape[0]
    tn = 2048
    return pl.pallas_call(
        _mod_kernel,
        grid=(depth, n // tn),
        in_specs=[pl.BlockSpec((r, d), lambda l, j: (0, 0)),
                  pl.BlockSpec((1, d, tn), lambda l, j: (l, 0, j)),
                  pl.BlockSpec((1, 1, tn), lambda l, j: (l, 0, j))],
        out_specs=pl.BlockSpec((1, r, tn), lambda l, j: (l, 0, j)),
        out_shape=jax.ShapeDtypeStruct((depth, r, n), f32),
        compiler_params=_cparams("parallel", "parallel"),
        name="modulation",
    )(cc, w_mod, b_mod.reshape(depth, 1, n))


def _in_proj_kernel(*refs, rope, n_alias):
    (x_ref, mod_ref, g_ref, w_ref, qg_ref, kvg_ref, wq_ref, wqs_ref, wke_ref, wv_ref, p_ref) = refs[:11]
    refs = refs[11:]
    if rope:
        cm_ref, sm_ref, kt_ref, cd_ref, sd_ref = refs[:5]
        refs = refs[5:]
    qm_ref, km_ref, vm_ref, qd_ref, kd_ref, vd_ref, hy_ref = refs[n_alias:]

    md = mod_ref[0]

    def project(rows):
        h = _rms(x_ref[0, rows, :]) * g_ref[...] * (1.0 + md[1:2]) + md[0:1]
        return _dot(h.astype(bf16), w_ref[...])

    def finish(rows, p):
        t = rows.start // TQ
        qn = (_rms(p[:, C_Q:C_KV]) * qg_ref[...]).astype(bf16)
        kvn = (_rms(p[:, C_KV:C_KR]) * kvg_ref[...]).astype(bf16)
        kr = p[:, C_KR:C_HY]
        q = _dot(qn, wq_ref[...])
        dq = p[:, C_DQ:C_DK]
        dk = p[:, C_DK:C_DV]
        if rope:
            kr = kr * kt_ref[rows, :]
            cm = jnp.concatenate([cm_ref[rows, :]] * MLA_HEADS, axis=-1)
            sm = jnp.concatenate([sm_ref[rows, :]] * MLA_HEADS, axis=-1)
            q = q * cm + _dot(qn, wqs_ref[...]) * sm
            cd = jnp.concatenate([cd_ref[rows, :]] * 2, axis=-1)
            sd = jnp.concatenate([sd_ref[rows, :]] * 2, axis=-1)
            dq = dq * cd + _dot(dq.astype(bf16), p_ref[...]) * sd
            dk = dk * cd + _dot(dk.astype(bf16), p_ref[...]) * sd
        k = _dot(jnp.concatenate([kvn, kr.astype(bf16)], axis=-1), wke_ref[...]).astype(bf16)
        dk = dk.astype(bf16)
        qm_ref[0, t] = (q * (MLA_SCALE * LOG2E)).T.astype(bf16)
        for h in range(km_ref.shape[1]):
            km_ref[0, h, rows, :] = k[:, LANES * h:LANES * (h + 1)]
        vm_ref[0, :, rows] = _dot(kvn, wv_ref[...]).T.astype(bf16)
        qd_ref[0, t] = (dq * (DF_SCALE * LOG2E)).T.astype(bf16)
        for h in range(kd_ref.shape[1]):
            kd_ref[0, h, rows, :] = dk[:, LANES * h:LANES * (h + 1)]
        vd_ref[0, :, rows] = p[:, C_DV:C_END].T.astype(bf16)
        hy_ref[0, rows, :] = p[:, C_HY:C_DQ].astype(bf16)

    for group in _sub_tile_groups(x_ref.shape[1]):
        projected = [project(rows) for rows in group]
        for rows, p in zip(group, projected):
            finish(rows, p)


def _in_proj(x, mod, per_batch_mod, g, lw, tables, n_keys, key_block, kv_bufs=None):
    b, lx, d = x.shape
    tm = min(TOKEN_TILE, lx)
    rope = tables is not None
    kb = key_block * (lx // tm)
    tok = lambda n: pl.BlockSpec((1, tm, n), lambda i, j: (j, i, 0))
    mod_map = (lambda i, j: (j, 0, 0)) if per_batch_mod else (lambda i, j: (0, 0, 0))
    weights = [lw['w_in1'], lw['q_norm_g'], lw['kv_norm_g'], lw['wq'], lw['wqs'],
               lw['wke_rope' if rope else 'wke'], lw['wv'], lw['p']]
    in_specs = ([tok(d), pl.BlockSpec((1, 6, d), mod_map), _const_spec(g.shape)]
                + [_resident_spec(w.shape) for w in weights])
    args = [x, mod, g] + weights
    if rope:
        in_specs += [pl.BlockSpec((tm, LANES), lambda i, j: (i, 0))] * len(tables)
        args += list(tables)
    aliases = {}
    if kv_bufs is not None:
        aliases = {len(args) + n: o for n, o in enumerate((1, 2, 4, 5))}
        in_specs += [pl.BlockSpec(memory_space=pl.ANY)] * len(kv_bufs)
        args += list(kv_bufs)
    f_q, f_d = MLA_HEADS * LANES, 2 * DF_HEADS * DF_DIM
    qtile = lambda n: pl.BlockSpec((1, tm // TQ, n, TQ), lambda i, j: (j, i, 0, 0))
    keys = lambda n: pl.BlockSpec((1, n // LANES, tm, LANES), lambda i, j: (j, 0, kb + i, 0))
    vals = lambda n: pl.BlockSpec((1, n, tm), lambda i, j: (j, 0, kb + i))
    sds = lambda *shape: jax.ShapeDtypeStruct(shape, bf16)
    out_specs = [qtile(f_q), keys(f_q), vals(MLA_OUT), qtile(f_d), keys(f_d), vals(DF_OUT), tok(3 * HY_WIDTH)]
    out_shape = [sds(b, lx // TQ, f_q, TQ), sds(b, f_q // LANES, n_keys, LANES), sds(b, MLA_OUT, n_keys),
                 sds(b, lx // TQ, f_d, TQ), sds(b, f_d // LANES, n_keys, LANES), sds(b, DF_OUT, n_keys),
                 sds(b, lx, 3 * HY_WIDTH)]
    return pl.pallas_call(
        functools.partial(_in_proj_kernel, rope=rope, n_alias=len(aliases)),
        grid=(lx // tm, b),
        in_specs=in_specs,
        out_specs=out_specs,
        out_shape=out_shape,
        input_output_aliases=aliases,
        compiler_params=_cparams("parallel", "parallel"),
        name="in_proj",
    )(*args)


def _softmax_maps(n_maps, qt_of, k_of, vt_of, s_scr, o_scr):
    _, lk, tq = s_scr.shape
    dv = o_scr.shape[0] // n_maps
    kc = KEY_CHUNK if lk % KEY_CHUNK == 0 else MXU_TILE
    assert lk % kc == 0
    ones = jnp.ones((BF16_ROWS, kc), bf16)

    def scores(n, slot):
        qt = qt_of(n)
        pm = None
        for off in range(0, lk, kc):
            s = _dot(k_of(n, off, kc), qt)
            s_scr[slot, off:off + kc, :] = s
            cm = jnp.max(s.reshape(kc // 8, 8, tq), axis=0)
            pm = cm if pm is None else jnp.maximum(pm, cm)
        return jnp.max(pm, axis=0, keepdims=True)

    def values(n, slot, m):
        acc = None
        for off in range(0, lk, kc):
            p = jnp.exp2(s_scr[slot, off:off + kc, :] - m).astype(bf16)
            part = _dot(jnp.concatenate([vt_of(n, off, kc), ones], axis=0), p)
            acc = part if acc is None else acc + part
        o_scr[pl.ds(_aligned(n * dv, dv), dv), :] = acc[:dv] / acc[dv:dv + 1]

    if lk == kc and n_maps <= MAPS_PER_TILE:
        qks = [_dot(k_of(n, 0, kc), qt_of(n)) for n in range(n_maps)]
        col_max = lambda s: jnp.max(jnp.max(s.reshape(kc // 8, 8, tq), axis=0), axis=0, keepdims=True)
        pts = [jnp.exp2(s - col_max(s)).astype(bf16) for s in qks]
        for n, p in enumerate(pts):
            acc = _dot(jnp.concatenate([vt_of(n, 0, kc), ones], axis=0), p)
            o_scr[n * dv:(n + 1) * dv, :] = acc[:dv] / acc[dv:dv + 1]
        return

    def group(base, m, count):
        for e in range(count):
            m_next = scores(base + e + 1, (e + 1) & 1)
            values(base + e, e & 1, m)
            m = m_next
        return m

    per_iter = 16 if n_maps >= 64 else (4 if n_maps >= 16 else 2)
    trips = (n_maps - 1) // per_iter
    m = lax.fori_loop(0, trips, lambda i, m: group(i * per_iter, m, per_iter), scores(0, 0))
    m = group(trips * per_iter, m, n_maps - 1 - trips * per_iter)
    values(n_maps - 1, (n_maps - 1) & 1, m)


def _mla_attn_kernel(qt_ref, k_ref, vt_ref, o_ref, s_scr, o_scr):
    n_tiles = qt_ref.shape[1]
    tile_of = lambda n: _shr(n, 3)
    head_of = lambda n: n & (MAPS_PER_TILE - 1)
    _softmax_maps(
        n_tiles * MLA_HEADS,
        lambda n: qt_ref[0, tile_of(n), pl.ds(_aligned(head_of(n) * LANES, LANES), LANES), :],
        lambda n, off, kc: k_ref[0, head_of(n), off:off + kc, :],
        lambda n, off, kc: vt_ref[0, pl.ds(_aligned(head_of(n) * MLA_V, MLA_V), MLA_V), off:off + kc],
        s_scr, o_scr)
    for t in range(n_tiles):
        for j in range(MLA_OUT // LANES):
            r = t * MLA_OUT + LANES * j
            o_ref[0, TQ * t:TQ * (t + 1), LANES * j:LANES * (j + 1)] = o_scr[r:r + LANES, :].T.astype(bf16)


def _diff_attn_kernel(lq1, lk1, lq2, lk2, g_ref, qt_ref, k_ref, vt_ref, o_ref, s_scr, o_scr, *, lam_init):
    n_tiles = qt_ref.shape[1]
    row = lax.broadcasted_iota(jnp.int32, (LANES, TQ), 0)
    per_lanes = LANES // DF_DIM
    tile_of = lambda n: _shr(n, 3)
    map_of = lambda n: n & (MAPS_PER_TILE - 1)
    lanes_of = lambda n: _shr(map_of(n), 2)

    def qt_of(n):
        lo = DF_DIM * (n & (per_lanes - 1))
        qt = qt_ref[0, tile_of(n), pl.ds(_aligned(lanes_of(n) * LANES, LANES), LANES), :]
        return jnp.where(row < lo, 0.0, jnp.where(row < lo + DF_DIM, qt, 0.0)).astype(bf16)

    def vt_of(n, off, kc):
        h = _shr(map_of(n), 1)
        return vt_ref[0, pl.ds(_aligned(h * DF_V, DF_V), DF_V), off:off + kc]

    _softmax_maps(n_tiles * 2 * DF_HEADS, qt_of, lambda n, off, kc: k_ref[0, lanes_of(n), off:off + kc, :], vt_of,
                  s_scr, o_scr)

    lam = (jnp.exp(jnp.sum(lq1[...] * lk1[...], axis=-1, keepdims=True))
           - jnp.exp(jnp.sum(lq2[...] * lk2[...], axis=-1, keepdims=True)) + lam_init)
    g = jnp.concatenate([g_ref[...]] * (TQ // LANES), axis=-1)
    for t in range(n_tiles):
        for j in range(DF_OUT // LANES):
            pair = []
            for h in (2 * j, 2 * j + 1):
                r = (t * MAPS_PER_TILE + 2 * h) * DF_V
                o = o_scr[r:r + DF_V, :] - lam * o_scr[r + DF_V:r + 2 * DF_V, :]
                ms = jnp.mean(o * o, axis=0, keepdims=True)
                pair.append(o * lax.rsqrt(ms + EPS) * g * (1.0 - lam_init))
            o_ref[0, TQ * t:TQ * (t + 1), LANES * j:LANES * (j + 1)] = (
                jnp.concatenate(pair, axis=0).T.astype(bf16))


def _attn_call(kernel, name, small, qt, k_all, vt_all, n_keys, key_block, dv):
    b, n_qt, f, _ = qt.shape
    f_out = vt_all.shape[1]
    tiles = next(t for t in (8, 4, 2, 1) if n_qt % t == 0)
    scratch = [pltpu.VMEM((2, n_keys, TQ), f32), pltpu.VMEM((tiles * MAPS_PER_TILE * dv, TQ), f32)]
    return pl.pallas_call(
        kernel,
        grid=(b, n_qt // tiles),
        in_specs=([_const_spec(a.shape) for a in small]
                  + [pl.BlockSpec((1, tiles, f, TQ), lambda i, j: (i, j, 0, 0)),
                     pl.BlockSpec((1, f // LANES, n_keys, LANES), lambda i, j: (i, 0, key_block, 0)),
                     pl.BlockSpec((1, f_out, n_keys), lambda i, j: (i, 0, key_block))]),
        out_specs=pl.BlockSpec((1, tiles * TQ, f_out), lambda i, j: (i, j, 0)),
        out_shape=jax.ShapeDtypeStruct((b, n_qt * TQ, f_out), bf16),
        scratch_shapes=scratch,
        compiler_params=_cparams("parallel", "parallel"),
        name=name,
    )(*small, qt, k_all, vt_all)


def _mla_attn(q, k_all, vt_all, n_keys, key_block):
    return _attn_call(_mla_attn_kernel, "mla_attn", [], q, k_all, vt_all, n_keys, key_block, MLA_V)


def _diff_attn(lams, g, q, k_all, vt_all, n_keys, key_block, lam_init):
    return _attn_call(functools.partial(_diff_attn_kernel, lam_init=lam_init), "diff_attn",
                      list(lams) + [g], q, k_all, vt_all, n_keys, key_block, DF_V)


def _hy_filter_kernel(z_ref, w1_ref, b1_ref, fr_ref, w2_ref, b2_ref, w3_ref, b3_ref, win_ref,
                      fc_ref, fs_ref, a_ref, a2_ref, bq_ref, h_scr):
    kt = pl.program_id(1)
    hp = lax.Precision.HIGHEST

    @pl.when(kt == 0)
    def _():
        fr = fr_ref[0]
        a = jnp.sin(fr * (jnp.dot(z_ref[...], w1_ref[0], precision=hp, preferred_element_type=f32) + b1_ref[0]))
        a = jnp.sin(fr * (jnp.dot(a, w2_ref[0], precision=hp, preferred_element_type=f32) + b2_ref[0]))
        h = jnp.dot(a, w3_ref[0], precision=hp, preferred_element_type=f32) + b3_ref[0]
        win = win_ref[...]
        row = lax.broadcasted_iota(jnp.int32, win.shape, 0)
        h_scr[:, :HY_WIDTH] = (h[:, :HY_WIDTH] * win).astype(bf16)
        h_scr[:, HY_WIDTH:] = jnp.where(row == 0, 0.0, h[:, HY_WIDTH:] * win).astype(bf16)

    hh = h_scr[...]
    c = _dot(fc_ref[...], hh)
    s = _dot(fs_ref[...], hh)
    tk = c.shape[0]
    row = lax.broadcasted_iota(jnp.int32, (tk, HY_WIDTH), 0) + kt * tk
    first = row == 0
    a = c[:, :HY_WIDTH] + c[:, HY_WIDTH:]
    second = jnp.where(first, s[:, :HY_WIDTH] + s[:, HY_WIDTH:], s[:, :HY_WIDTH] - s[:, HY_WIDTH:])
    a_ref[0] = a
    a2_ref[0] = jnp.where(first, second, a)
    bq_ref[0] = jnp.where(first, 0.0, second)


def _hy_filter(z, hw, win, fmat):
    l = z.shape[0]
    depth = hw[0].shape[0]
    tk = min(512, l)
    nk = l // tk
    per_layer = lambda a: pl.BlockSpec((1,) + a.shape[1:], lambda d, k: (d,) + (0,) * (a.ndim - 1))
    out = jax.ShapeDtypeStruct((depth, l, HY_WIDTH), f32)
    return pl.pallas_call(
        _hy_filter_kernel,
        grid=(depth, nk),
        in_specs=([_const_spec(z.shape)] + [per_layer(a) for a in hw] + [_const_spec(win.shape)]
                  + [pl.BlockSpec((tk, l), lambda d, k: (k, 0)), pl.BlockSpec((tk, l), lambda d, k: (k + nk, 0))]),
        out_specs=[pl.BlockSpec((1, tk, HY_WIDTH), lambda d, k: (d, k, 0))] * 3,
        out_shape=[out] * 3,
        scratch_shapes=[pltpu.VMEM((l, 2 * HY_WIDTH), bf16)],
        compiler_params=_cparams("parallel", "arbitrary"),
        name="hy_filter",
    )(z, *hw, win, fmat, fmat)


def _hy_conv_kernel(p_ref, cw_ref, cb_ref, skip_ref, fc_ref, fs_ref, gc_ref, gs_ref,
                    a_ref, a2_ref, bq_ref, o_ref, u_scr, ub_scr, x0_scr, acc_scr, *, bb):
    kt = pl.program_id(1)
    l = p_ref.shape[1]

    @pl.when(kt == 0)
    def _():
        row = lax.broadcasted_iota(jnp.int32, (l, 3 * HY_WIDTH), 0)
        cw = cw_ref[...]
        for i in range(bb):
            p = p_ref[i].astype(f32)
            prev = jnp.where(row == 0, 0.0, pltpu.roll(p, 1, axis=0))
            nxt = jnp.where(row == l - 1, 0.0, pltpu.roll(p, l - 1, axis=0))
            uc = cw[0:1] * prev + cw[1:2] * p + cw[2:3] * nxt + cb_ref[...]
            x0_scr[i] = uc[:, :HY_WIDTH]
            u = uc[:, 2 * HY_WIDTH:] * uc[:, HY_WIDTH:2 * HY_WIDTH]
            u_scr[i] = u
            ub_scr[i] = u.astype(bf16)
            acc_scr[i] = jnp.zeros((l, HY_WIDTH), f32)

    a, a2, bq = a_ref[...], a2_ref[...], bq_ref[...]
    spectra = [(_dot(fc_ref[...], ub_scr[i]), _dot(fs_ref[...], ub_scr[i])) for i in range(bb)]
    for i, (ur, ui) in enumerate(spectra):
        zr = (ur * a - ui * bq).astype(bf16)
        zi = (ur * bq + ui * a2).astype(bf16)
        acc_scr[i] += _dot(gc_ref[...], zr) + _dot(gs_ref[...], zi)

    @pl.when(kt == pl.num_programs(1) - 1)
    def _():
        for i in range(bb):
            o_ref[i] = (x0_scr[i] * (acc_scr[i] + skip_ref[...] * u_scr[i])).astype(bf16)


def _hy_conv(p_hy, lw, spec, fmat, gmat):
    b, l, _ = p_hy.shape
    bb = 2
    tk = min(512, l)
    nk = l // tk
    a, a2, bq = spec
    return pl.pallas_call(
        functools.partial(_hy_conv_kernel, bb=bb),
        grid=(b // bb, nk),
        in_specs=[pl.BlockSpec((bb, l, 3 * HY_WIDTH), lambda i, k: (i, 0, 0)),
                  _const_spec(lw['hy_conv_w'].shape), _const_spec(lw['hy_conv_b'].shape),
                  _const_spec(lw['hy_skip'].shape),
                  pl.BlockSpec((tk, l), lambda i, k: (k, 0)), pl.BlockSpec((tk, l), lambda i, k: (k + nk, 0)),
                  pl.BlockSpec((l, tk), lambda i, k: (0, k)), pl.BlockSpec((l, tk), lambda i, k: (0, k + nk))]
                 + [pl.BlockSpec((tk, HY_WIDTH), lambda i, k: (k, 0))] * 3,
        out_specs=pl.BlockSpec((bb, l, HY_WIDTH), lambda i, k: (i, 0, 0)),
        out_shape=jax.ShapeDtypeStruct((b, l, HY_WIDTH), bf16),
        scratch_shapes=[pltpu.VMEM((bb, l, HY_WIDTH), f32), pltpu.VMEM((bb, l, HY_WIDTH), bf16),
                        pltpu.VMEM((bb, l, HY_WIDTH), f32), pltpu.VMEM((bb, l, HY_WIDTH), f32)],
        compiler_params=_cparams("parallel", "arbitrary"),
        name="hy_conv",
    )(p_hy, lw['hy_conv_w'], lw['hy_conv_b'], lw['hy_skip'], fmat, fmat, gmat, gmat, a, a2, bq)


def _merge_kernel(x_ref, mod_ref, g_ref, ya_ref, yb_ref, yc_ref, wg_ref, wa_ref, wb_ref, wc_ref, wo_ref,
                  o_ref):
    md = mod_ref[0]
    d = D_MODEL
    for group in _sub_tile_groups(x_ref.shape[1]):
        staged = []
        for rows in group:
            h = (_rms(x_ref[0, rows, :]) * g_ref[...] * (1.0 + md[1:2]) + md[0:1]).astype(bf16)
            gates = [_dot(h, wg_ref[:, d * i:d * (i + 1)]) for i in range(N_BRANCH)]
            branches = [_dot(ya_ref[0, rows, :], wa_ref[...]), _dot(yb_ref[0, rows, :], wb_ref[...]),
                        _dot(yc_ref[0, rows, :], wc_ref[...])]
            staged.append((gates, branches))
        for rows, (gates, branches) in zip(group, staged):
            m = functools.reduce(jnp.add, [jax.nn.sigmoid(gt) * br for gt, br in zip(gates, branches)])
            o_ref[0, rows, :] = x_ref[0, rows, :] + md[2:3] * _dot(m.astype(bf16), wo_ref[...])


def _merge(x, mod, per_batch_mod, g, ya, yb, yc, lw):
    b, lx, d = x.shape
    tm = min(TOKEN_TILE, lx)
    tok = lambda n: pl.BlockSpec((1, tm, n), lambda i, j: (i, j, 0))
    mod_map = (lambda i, j: (i, 0, 0)) if per_batch_mod else (lambda i, j: (0, 0, 0))
    weights = [lw['w_gate'], lw['w_br_a'], lw['w_br_b'], lw['w_br_c'], lw['w_out']]
    return pl.pallas_call(
        _merge_kernel,
        grid=(b, lx // tm),
        in_specs=[tok(d), pl.BlockSpec((1, 6, d), mod_map), _const_spec(g.shape),
                  tok(MLA_OUT), tok(HY_WIDTH), tok(DF_OUT)] + [_resident_spec(w.shape) for w in weights],
        out_specs=tok(d),
        out_shape=jax.ShapeDtypeStruct(x.shape, f32),
        compiler_params=_cparams("parallel", "parallel"),
        name="merge",
    )(x, mod, g, ya, yb, yc, *weights)


def _merge_mlp_kernel(*refs, final):
    (x_ref, mod_ref, gm_ref, gf_ref, ya_ref, yb_ref, yc_ref, wg_ref, wa_ref, wb_ref, wc_ref, wo_ref,
     w1_ref, w2_ref) = refs[:14]
    o_ref, x1_scr = refs[-2:]
    md = mod_ref[0]
    d = D_MODEL
    for group in _sub_tile_groups(x_ref.shape[1]):
        staged = []
        for rows in group:
            h = (_rms(x_ref[0, rows, :]) * gm_ref[...] * (1.0 + md[1:2]) + md[0:1]).astype(bf16)
            gates = [_dot(h, wg_ref[:, d * i:d * (i + 1)]) for i in range(N_BRANCH)]
            branches = [_dot(ya_ref[0, rows, :], wa_ref[...]), _dot(yb_ref[0, rows, :], wb_ref[...]),
                        _dot(yc_ref[0, rows, :], wc_ref[...])]
            staged.append((gates, branches))
        for rows, (gates, branches) in zip(group, staged):
            m = functools.reduce(jnp.add, [jax.nn.sigmoid(gt) * br for gt, br in zip(gates, branches)])
            x1_scr[rows, :] = x_ref[0, rows, :] + md[2:3] * _dot(m.astype(bf16), wo_ref[...])
        hidden = []
        for rows in group:
            h = (_rms(x1_scr[rows, :]) * gf_ref[...] * (1.0 + md[4:5]) + md[3:4]).astype(bf16)
            hidden.append(_dot(h, w1_ref[...]))
        for rows, a in zip(group, hidden):
            a = jnp.maximum(a, 0.0)
            y = x1_scr[rows, :] + md[5:6] * _dot((a * a).astype(bf16), w2_ref[...])
            if final:
                y = _rms(y) * refs[14][...]
            o_ref[0, rows, :] = y


def _merge_mlp(x, mod, per_batch_mod, g_mix, g_ffn, ya, yb, yc, lw, final_g=None):
    b, lx, d = x.shape
    tm = min(FUSED_TOKEN_TILE, lx)
    tok = lambda n: pl.BlockSpec((1, tm, n), lambda i, j: (i, j, 0))
    mod_map = (lambda i, j: (i, 0, 0)) if per_batch_mod else (lambda i, j: (0, 0, 0))
    weights = [lw['w_gate'], lw['w_br_a'], lw['w_br_b'], lw['w_br_c'], lw['w_out'], lw['w_fc1'], lw['w_fc2']]
    final = final_g is not None
    in_specs = ([tok(d), pl.BlockSpec((1, 6, d), mod_map), _const_spec(g_mix.shape), _const_spec(g_ffn.shape),
                 tok(MLA_OUT), tok(HY_WIDTH), tok(DF_OUT)] + [_resident_spec(w.shape) for w in weights])
    args = [x, mod, g_mix, g_ffn, ya, yb, yc] + weights
    if final:
        in_specs.append(_const_spec(final_g.shape))
        args.append(final_g)
    return pl.pallas_call(
        functools.partial(_merge_mlp_kernel, final=final),
        grid=(b, lx // tm),
        in_specs=in_specs,
        out_specs=tok(d),
        out_shape=jax.ShapeDtypeStruct(x.shape, f32),
        scratch_shapes=[pltpu.VMEM((tm, d), f32)],
        compiler_params=_cparams("parallel", "parallel"),
        name="merge_mlp",
    )(*args)


def _mlp_kernel(*refs, final):
    x_ref, mod_ref, g_ref, w1_ref, w2_ref = refs[:5]
    o_ref = refs[-1]
    md = mod_ref[0]
    for group in _sub_tile_groups(x_ref.shape[1]):
        hidden = []
        for rows in group:
            h = (_rms(x_ref[0, rows, :]) * g_ref[...] * (1.0 + md[4:5]) + md[3:4]).astype(bf16)
            hidden.append(_dot(h, w1_ref[...]))
        for rows, a in zip(group, hidden):
            a = jnp.maximum(a, 0.0)
            y = x_ref[0, rows, :] + md[5:6] * _dot((a * a).astype(bf16), w2_ref[...])
            if final:
                y = _rms(y) * refs[5][...]
            o_ref[0, rows, :] = y


def _mlp(x, mod, per_batch_mod, g, w1, w2, final_g=None):
    b, lx, d = x.shape
    tm = min(TOKEN_TILE, lx)
    tok = pl.BlockSpec((1, tm, d), lambda i, j: (i, j, 0))
    mod_map = (lambda i, j: (i, 0, 0)) if per_batch_mod else (lambda i, j: (0, 0, 0))
    final = final_g is not None
    in_specs = [tok, pl.BlockSpec((1, 6, d), mod_map), _const_spec(g.shape),
                _resident_spec(w1.shape), _resident_spec(w2.shape)]
    args = [x, mod, g, w1, w2]
    if final:
        in_specs.append(_const_spec(final_g.shape))
        args.append(final_g)
    return pl.pallas_call(
        functools.partial(_mlp_kernel, final=final),
        grid=(b, lx // tm),
        in_specs=in_specs,
        out_specs=tok,
        out_shape=jax.ShapeDtypeStruct(x.shape, f32),
        compiler_params=_cparams("parallel", "parallel"),
        name="mlp",
    )(*args)


def _rope_tables(n_tokens):
    rows = n_tokens // GRID_W
    row = jnp.repeat(jnp.arange(rows), GRID_W).astype(f32)
    col = jnp.tile(jnp.arange(GRID_W), rows).astype(f32)
    nf = MLA_ROPE // 4
    inv = ROPE_BASE ** (-jnp.arange(nf, dtype=f32) / nf)
    ang = jnp.concatenate([row[:, None] * inv, col[:, None] * inv], axis=-1)
    cos, sin = jnp.cos(ang), jnp.sin(ang)
    one = jnp.ones((n_tokens, MLA_NOPE), f32)
    zero = jnp.zeros((n_tokens, MLA_NOPE), f32)
    pad = jnp.zeros((n_tokens, LANES - MLA_NOPE - MLA_ROPE), f32)
    cm = jnp.concatenate([one, cos, cos, pad], axis=-1)
    sm = jnp.concatenate([zero, -sin, sin, pad], axis=-1)
    kt = jnp.concatenate([cos, cos, -sin, sin, jnp.zeros((n_tokens, LANES - 2 * MLA_ROPE), f32)], axis=-1)
    cd = jnp.tile(jnp.concatenate([cos, cos], axis=-1), (1, LANES // DF_DIM))
    sd = jnp.tile(jnp.concatenate([-sin, sin], axis=-1), (1, LANES // DF_DIM))
    return cm, sm, kt, cd, sd


def _hy_positional(l):
    t = jnp.linspace(0.0, 1.0, l, dtype=f32)[:, None]
    w = (2.0 * math.pi / l) * jnp.arange(l, dtype=f32)[:, None]
    bands = jnp.linspace(1e-4, HY_BANDS - 1, HY_BANDS, dtype=f32)[None]
    z = jnp.concatenate([t, jnp.cos(bands * w), -jnp.sin(bands * w)], axis=-1)
    z = jnp.pad(z, ((0, 0), (0, LANES - HY_EMB)))
    deltas = jnp.linspace(math.log(HY_TARGET) / HY_SLOW, math.log(HY_TARGET) / HY_FAST, HY_WIDTH, dtype=f32)
    win = jnp.exp(-t * jnp.abs(deltas)) + HY_SHIFT
    return z, win


def _dft_tables(l):
    n2 = 2 * l
    n = jnp.arange(l, dtype=jnp.int32)[None, :]
    a = jnp.arange(l // GRID_W, dtype=jnp.int32)[:, None]
    b = jnp.arange(GRID_W, dtype=jnp.int32)[:, None]
    ang_a = ((GRID_W * a * n) % n2).astype(f32) * (2.0 * math.pi / n2)
    ang_b = ((b * n) % n2).astype(f32) * (2.0 * math.pi / n2)
    ca, sa, cb, sb = jnp.cos(ang_a), jnp.sin(ang_a), jnp.cos(ang_b), jnp.sin(ang_b)
    nyq = jnp.where(n[0] % 2 == 0, 1.0, -1.0)
    c = (ca[:, None, :] * cb[None] - sa[:, None, :] * sb[None]).reshape(l, l)
    s = -(sa[:, None, :] * cb[None] + ca[:, None, :] * sb[None]).reshape(l, l)
    fmat = jnp.concatenate([c, s.at[0].set(nyq)], axis=0)
    wgt = jnp.full((l,), 2.0 / n2, f32).at[0].set(1.0 / n2)
    cat, sat, cbt, sbt = ca.T, sa.T, cb.T, sb.T
    ct = (cat[:, :, None] * cbt[:, None, :] - sat[:, :, None] * sbt[:, None, :]).reshape(l, l)
    st = -(sat[:, :, None] * cbt[:, None, :] + cat[:, :, None] * sbt[:, None, :]).reshape(l, l)
    gmat = jnp.concatenate([ct * wgt[None], st.at[:, 0].set(nyq) * wgt[None]], axis=1)
    return fmat.astype(bf16), gmat.astype(bf16)


def _prep_weights(w_in, mla_q_norm_g, mla_w_uq, mla_kv_norm_g, mla_w_ukv, hy_conv_w, hy_conv_b, hy_w1, hy_b1,
                  hy_freq, hy_w2, hy_b2, hy_w3, hy_b3, hy_skip, df_subln_g, w_br_a, w_br_b, w_br_c, w_out,
                  w_fc1, w_fc2):
    depth = w_in.shape[0]
    o = [0]
    for n in (MLA_Q_RANK, MLA_KV_RANK, MLA_ROPE, 3 * HY_WIDTH, 2 * DF_HEADS * DF_DIM, 2 * DF_HEADS * DF_DIM,
              DF_OUT, N_BRANCH * D_MODEL):
        o.append(o[-1] + n)
    half = MLA_ROPE // 2
    w_kr = w_in[..., o[2]:o[3]]
    kr_pad = jnp.zeros((depth, D_MODEL, LANES - 2 * MLA_ROPE), f32)
    w_in1 = jnp.concatenate([w_in[..., o[0]:o[3]], w_kr[..., half:], w_kr[..., :half], kr_pad,
                             w_in[..., o[3]:o[7]]], axis=-1).astype(bf16)
    w_gate = w_in[..., o[7]:o[8]].astype(bf16)

    uq = mla_w_uq.reshape(depth, MLA_Q_RANK, MLA_HEADS, MLA_NOPE + MLA_ROPE)
    nope, rope = uq[..., :MLA_NOPE], uq[..., MLA_NOPE:]
    zn = jnp.zeros_like(nope)
    zp = jnp.zeros(uq.shape[:3] + (LANES - MLA_NOPE - MLA_ROPE,), f32)
    wq = jnp.concatenate([nope, rope, zp], axis=-1).reshape(depth, MLA_Q_RANK, MLA_HEADS * LANES)
    wqs = jnp.concatenate([zn, rope[..., half:], rope[..., :half], zp], axis=-1)
    wqs = wqs.reshape(depth, MLA_Q_RANK, MLA_HEADS * LANES)
    ukv = mla_w_ukv.reshape(depth, MLA_KV_RANK, MLA_HEADS, MLA_NOPE + MLA_V)
    wk = jnp.concatenate([ukv[..., :MLA_NOPE], jnp.zeros_like(ukv[..., MLA_NOPE:])], axis=-1)
    wk = wk.reshape(depth, MLA_KV_RANK, MLA_HEADS * LANES)
    wv = ukv[..., MLA_NOPE:].reshape(depth, MLA_KV_RANK, MLA_OUT)

    i = jnp.arange(LANES)[:, None]
    js = jnp.arange(MLA_HEADS * LANES)[None, :] % LANES - MLA_NOPE
    e = ((i < MLA_ROPE) & (js == i)).astype(f32)
    e_rope = ((i < 2 * MLA_ROPE) & (js == i % MLA_ROPE)).astype(f32)
    wke = jnp.concatenate([wk, jnp.broadcast_to(e, (depth,) + e.shape)], axis=1).astype(bf16)
    wke_rope = jnp.concatenate([wk, jnp.broadcast_to(e_rope, (depth,) + e.shape)], axis=1).astype(bf16)
    a = jnp.arange(2 * DF_HEADS * DF_DIM)
    dh = DF_DIM // 2
    swap = (a // DF_DIM) * DF_DIM + (a % DF_DIM + dh) % DF_DIM
    p = (a[:, None] == swap[None, :]).astype(bf16)

    pad_rows = lambda w, r: jnp.pad(w, ((0, 0), (0, r - w.shape[1]), (0, 0)))
    pad_cols = lambda w, c: jnp.pad(w, ((0, 0),) * (w.ndim - 1) + ((0, c - w.shape[-1]),))
    layers = []
    for l in range(depth):
        layers.append({
            'w_in1': w_in1[l], 'w_gate': w_gate[l],
            'q_norm_g': mla_q_norm_g[l][None], 'kv_norm_g': mla_kv_norm_g[l][None],
            'wq': wq[l].astype(bf16), 'wqs': wqs[l].astype(bf16), 'wke': wke[l], 'wke_rope': wke_rope[l],
            'wv': wv[l].astype(bf16), 'p': p,
            'hy_conv_w': hy_conv_w[l], 'hy_conv_b': hy_conv_b[l][None], 'hy_skip': hy_skip[l][None],
            'subln_g': jnp.broadcast_to(df_subln_g[l][:, None], (DF_V, LANES)),
            'w_br_a': w_br_a[l].astype(bf16), 'w_br_b': w_br_b[l].astype(bf16),
            'w_br_c': w_br_c[l].astype(bf16), 'w_out': w_out[l].astype(bf16),
            'w_fc1': w_fc1[l].astype(bf16), 'w_fc2': w_fc2[l].astype(bf16),
        })
    hy_filter_w = [pad_cols(pad_rows(hy_w1, LANES), LANES), pad_cols(hy_b1, LANES)[:, None],
                   pad_cols(hy_freq, LANES)[:, None], pad_cols(pad_rows(hy_w2, LANES), LANES),
                   pad_cols(hy_b2, LANES)[:, None], pad_rows(hy_w3, LANES), hy_b3[:, None]]
    return layers, hy_filter_w


def _token_mixers(lat, ctx, n_lat, n_ctx, lw, lams, lam_init, hy_lat, hy_ctx, with_ctx):
    qm, _, _, qd, _, _, hy = lat
    cqm, km, vm, cqd, kd, vd, chy = ctx
    n_all = n_lat + n_ctx
    ya = _mla_attn(qm, km, vm, n_all, 0)
    yc = _diff_attn(lams, lw['subln_g'], qd, kd, vd, n_all, 0, lam_init)
    yb = _hy_conv(hy, lw, *hy_lat)
    if not with_ctx:
        return (ya, yb, yc), None
    ya_c = _mla_attn(cqm, km, vm, n_ctx, n_lat // n_ctx)
    yc_c = _diff_attn(lams, lw['subln_g'], cqd, kd, vd, n_ctx, n_lat // n_ctx, lam_init)
    yb_c = _hy_conv(chy, lw, *hy_ctx)
    return (ya, yb, yc), (ya_c, yb_c, yc_c)


def kernel(x, c, ctx, c_ctx, norm_mix_g, norm_ffn_g, w_mod, b_mod, w_in, mla_q_norm_g, mla_w_uq, mla_kv_norm_g, mla_w_ukv, hy_conv_w, hy_conv_b, hy_w1, hy_b1, hy_freq, hy_w2, hy_b2, hy_w3, hy_b3, hy_skip, df_lq1, df_lk1, df_lq2, df_lk2, df_subln_g, w_br_a, w_br_b, w_br_c, w_out, w_fc1, w_fc2, final_norm_g):
    b, n_lat, d = x.shape
    n_ctx = ctx.shape[1]
    assert n_lat % n_ctx == 0 and n_ctx % TQ == 0
    depth = w_in.shape[0]
    layers, hy_filter_w = _prep_weights(w_in, mla_q_norm_g, mla_w_uq, mla_kv_norm_g, mla_w_ukv, hy_conv_w, hy_conv_b,
                           hy_w1, hy_b1, hy_freq, hy_w2, hy_b2, hy_w3, hy_b3, hy_skip, df_subln_g,
                           w_br_a, w_br_b, w_br_c, w_out, w_fc1, w_fc2)
    tables = _rope_tables(n_lat)
    z_lat, win_lat = _hy_positional(n_lat)
    z_ctx, win_ctx = _hy_positional(n_ctx)
    f_lat, g_lat = _dft_tables(n_lat)
    f_ctx, g_ctx = _dft_tables(n_ctx)
    spec_lat = _hy_filter(z_lat, hy_filter_w, win_lat, f_lat)
    spec_ctx = _hy_filter(z_ctx, hy_filter_w, win_ctx, f_ctx)

    rows = -(-(b + 1) // 8) * 8
    cc = jnp.concatenate([c, c_ctx[None], jnp.zeros((rows - b - 1, d), f32)], axis=0)
    mod = _modulation(cc, w_mod, b_mod).reshape(depth, rows, 6, d)

    xc = ctx
    for l in range(depth):
        lw = layers[l]
        with_ctx = l < depth - 1
        lam_init = 0.8 - 0.6 * math.exp(-0.3 * l)
        mod_lat, mod_ctx = mod[l, :b], mod[l, b:b + 1]
        g_mix, g_ffn = norm_mix_g[l][None], norm_ffn_g[l][None]
        lams = [df_lq1[l][None], df_lk1[l][None], df_lq2[l][None], df_lk2[l][None]]

        hy_lat = ([a[l] for a in spec_lat], f_lat, g_lat)
        hy_ctx = ([a[l] for a in spec_ctx], f_ctx, g_ctx) if with_ctx else None

        n_all = n_lat + n_ctx
        parts = _in_proj(x, mod_lat, True, g_mix, lw, tables, n_all, 0)
        kv_bufs = [parts[i] for i in (1, 2, 4, 5)]
        parts_c = _in_proj(xc, mod_ctx, False, g_mix, lw, None, n_all, n_lat // n_ctx, kv_bufs)
        ys, ys_c = _token_mixers(parts, parts_c, n_lat, n_ctx, lw, lams, lam_init, hy_lat, hy_ctx, with_ctx)

        x = _merge_mlp(x, mod_lat, True, g_mix, g_ffn, *ys, lw,
                       final_norm_g[None] if l == depth - 1 else None)
        if with_ctx:
            xc = _merge_mlp(xc, mod_ctx, False, g_mix, g_ffn, *ys_c, lw)
    return x
```
